```python
import math
import jax, jax.numpy as jnp
from jax import lax
import numpy as np

D_MODEL = 1024
BATCH = 4
SEQ = 4096
DEPTH = 1

CHUNK = 64
N_META = 16
Q_BLOCK = 128
NEG_INF = -1e30
RMS_EPS = 1e-6
ROPE_BASE = 10000.0

MLA_HEADS = 8
MLA_Q_LORA = 256
MLA_KV_LORA = 128
MLA_NOPE = 64
MLA_ROPE = 32
MLA_V = 64
DSA_HEADS = 8
DSA_HEAD_DIM = 64
IDX_HEADS = 8
IDX_DIM = 64
K_SEL_MAX = 256
REL_BUCKETS = 32
REL_MAX_DIST = 128
N_GROUPS = 4
EXPERTS_PER_GROUP = 8
N_EXPERTS = N_GROUPS * EXPERTS_PER_GROUP
TOP_K_EXPERTS = 2
D_EXPERT = 256
MOE_BLOCK = 128

IN_SPLITS = (
    MLA_Q_LORA,
    MLA_KV_LORA,
    MLA_ROPE,
    DSA_HEADS * DSA_HEAD_DIM,
    DSA_HEADS * DSA_HEAD_DIM,
    DSA_HEADS * DSA_HEAD_DIM,
    IDX_HEADS * IDX_DIM,
    IDX_DIM,
    IDX_HEADS,
    D_MODEL,
    D_MODEL,
)
IN_COLS = (MLA_Q_LORA + MLA_KV_LORA + MLA_ROPE + 3 * DSA_HEADS * DSA_HEAD_DIM
           + IDX_HEADS * IDX_DIM + IDX_DIM + IDX_HEADS + 2 * D_MODEL)

kernel_name = "hybrid_mla_dsa_hmoe_chunk_causal"


def rmsnorm(x, g):
    xf = x.astype(jnp.float32)
    y = xf * lax.rsqrt(jnp.mean(xf * xf, axis=-1, keepdims=True) + RMS_EPS)
    return (y * g.astype(jnp.float32)).astype(x.dtype)


def chunk_ids(pos):
    return jnp.where(pos < N_META, 0, 1 + (pos - N_META) // CHUNK)


def rope(x, pos):
    half = x.shape[-1] // 2
    inv = ROPE_BASE ** (-jnp.arange(half, dtype=jnp.float32) / half)
    ang = pos.astype(jnp.float32)[:, None] * inv[None, :]
    cos = jnp.cos(ang)[None, :, None, :].astype(x.dtype)
    sin = jnp.sin(ang)[None, :, None, :].astype(x.dtype)
    x1, x2 = x[..., :half], x[..., half:]
    return jnp.concatenate([x1 * cos - x2 * sin, x1 * sin + x2 * cos], axis=-1)


def t5_bucket(rel):
    nb = REL_BUCKETS // 2
    max_exact = nb // 2
    n = jnp.abs(rel)
    large = max_exact + (jnp.log(jnp.maximum(n, 1).astype(jnp.float32) / max_exact)
                         / math.log(REL_MAX_DIST / max_exact) * (nb - max_exact)).astype(jnp.int32)
    large = jnp.minimum(large, nb - 1)
    return jnp.where(rel > 0, nb, 0) + jnp.where(n < max_exact, n, large)


def to_blocks(t, n_blk):
    pad = n_blk * Q_BLOCK - t.shape[1]
    t = jnp.pad(t, [(0, 0), (0, pad)] + [(0, 0)] * (t.ndim - 2))
    t = t.reshape((t.shape[0], n_blk, Q_BLOCK) + t.shape[2:])
    return jnp.moveaxis(t, 1, 0)


def from_blocks(o, L):
    o = jnp.moveaxis(o, 0, 1)
    o = o.reshape((o.shape[0], o.shape[1] * o.shape[2]) + o.shape[3:])
    return o[:, :L]


def hybrid_mixer(h, w_in, g_cq, g_ckv, w_uq, w_ukv, w_br_a, w_br_b, w_o, rel_bias):
    B, L, _ = h.shape
    pos = jnp.arange(L, dtype=jnp.int32)
    cid = chunk_ids(pos)
    n_blk = -(-L // Q_BLOCK)
    pos_blocks = jnp.arange(n_blk * Q_BLOCK, dtype=jnp.int32).reshape(n_blk, Q_BLOCK)
    k_sel = min(K_SEL_MAX, SEQ // 4)

    offs = [int(o) for o in np.cumsum(IN_SPLITS)[:-1]]
    (c_q, c_kv, k_r, q_b, k_b, v_b, q_i, k_i, w_i, gate_a, gate_b) = jnp.split(h @ w_in, offs, axis=-1)

    q_a = (rmsnorm(c_q, g_cq) @ w_uq).reshape(B, L, MLA_HEADS, MLA_NOPE + MLA_ROPE)
    q_nope, q_rope = q_a[..., :MLA_NOPE], rope(q_a[..., MLA_NOPE:], pos)
    kv_a = (rmsnorm(c_kv, g_ckv) @ w_ukv).reshape(B, L, MLA_HEADS, MLA_NOPE + MLA_V)
    k_nope, v_a = kv_a[..., :MLA_NOPE], kv_a[..., MLA_NOPE:]
    k_rope = rope(k_r[:, :, None, :], pos)[:, :, 0]
    scale_a = (MLA_NOPE + MLA_ROPE) ** -0.5

    def mla_block(args):
        qn, qr, pq = args
        s = (jnp.einsum('bqhd,bkhd->bhqk', qn, k_nope)
             + jnp.einsum('bqhr,bkr->bhqk', qr, k_rope)) * scale_a
        mask = cid[None, :] <= chunk_ids(pq)[:, None]
        s = jnp.where(mask[None, None], s.astype(jnp.float32), NEG_INF)
        p = jax.nn.softmax(s, axis=-1).astype(v_a.dtype)
        return jnp.einsum('bhqk,bkhd->bqhd', p, v_a)

    o_a = from_blocks(lax.map(mla_block, (to_blocks(q_nope, n_blk), to_blocks(q_rope, n_blk), pos_blocks)), L)

    q_b = q_b.reshape(B, L, DSA_HEADS, DSA_HEAD_DIM)
    k_b = k_b.reshape(B, L, DSA_HEADS, DSA_HEAD_DIM)
    v_b = v_b.reshape(B, L, DSA_HEADS, DSA_HEAD_DIM)
    q_i = q_i.reshape(B, L, IDX_HEADS, IDX_DIM) * (IDX_DIM ** -0.5)
    w_i = w_i * (IDX_HEADS ** -0.5)
    scale_b = DSA_HEAD_DIM ** -0.5
    gather = jax.vmap(lambda t, i: t[i])

    def dsa_block(args):
        qb, qib, wb, pq = args
        cq = chunk_ids(pq)
        logits = jnp.einsum('bqhd,bkd->bqhk', qib, k_i)
        score = jnp.einsum('bqh,bqhk->bqk', wb, jax.nn.relu(logits)).astype(jnp.float32)
        adm = cid[None, :] <= cq[:, None]
        score = jnp.where(adm[None], score, NEG_INF)
        _, idx = lax.top_k(score, k_sel)
        ks = gather(k_b, idx)
        vs = gather(v_b, idx)
        s = jnp.einsum('bqhd,bqkhd->bhqk', qb, ks) * scale_b
        bias = rel_bias[t5_bucket(idx - pq[None, :, None])]
        s = s.astype(jnp.float32) + jnp.transpose(bias, (0, 3, 1, 2)).astype(jnp.float32)
        valid = cid[idx] <= cq[None, :, None]
        s = jnp.where(valid[:, None], s, NEG_INF)
        p = jax.nn.softmax(s, axis=-1).astype(vs.dtype)
        return jnp.einsum('bhqk,bqkhd->bqhd', p, vs)

    o_b = from_blocks(lax.map(dsa_block, (to_blocks(q_b, n_blk), to_blocks(q_i, n_blk),
                                          to_blocks(w_i, n_blk), pos_blocks)), L)

    y = (jax.nn.sigmoid(gate_a) * (o_a.reshape(B, L, -1) @ w_br_a)
         + jax.nn.sigmoid(gate_b) * (o_b.reshape(B, L, -1) @ w_br_b))
    return y @ w_o


def hier_moe(x2d, w_rg, b_rg, w_re, b_re, w_gate, w_up, w_down):
    N, D = x2d.shape
    g_logits = (x2d @ w_rg + b_rg).astype(jnp.float32)
    g_sel = jnp.argmax(g_logits, axis=-1)
    p_group = jnp.take_along_axis(jax.nn.softmax(g_logits, axis=-1), g_sel[:, None], axis=1)[:, 0]
    e_logits = (x2d @ w_re + b_re).astype(jnp.float32).reshape(N, N_GROUPS, EXPERTS_PER_GROUP)
    e_logits = jnp.take_along_axis(e_logits, g_sel[:, None, None], axis=1)[:, 0]
    top_p, top_e = lax.top_k(jax.nn.softmax(e_logits, axis=-1), TOP_K_EXPERTS)
    gate = p_group[:, None] * top_p / jnp.sum(top_p, axis=-1, keepdims=True)
    expert = g_sel[:, None] * EXPERTS_PER_GROUP + top_e

    A = N * TOP_K_EXPERTS
    flat_e = expert.reshape(A).astype(jnp.int32)
    flat_tok = jnp.repeat(jnp.arange(N, dtype=jnp.int32), TOP_K_EXPERTS)
    flat_w = gate.reshape(A)
    order = jnp.argsort(flat_e)
    e_sorted = flat_e[order]
    counts = jnp.bincount(flat_e, length=N_EXPERTS)
    padded = (counts + MOE_BLOCK - 1) // MOE_BLOCK * MOE_BLOCK
    pad_end = jnp.cumsum(padded)
    start = jnp.cumsum(counts) - counts
    dest = (pad_end - padded)[e_sorted] + jnp.arange(A, dtype=jnp.int32) - start[e_sorted]
    n_blk = -(-A // MOE_BLOCK) + N_EXPERTS
    S = n_blk * MOE_BLOCK
    slot_tok = jnp.full((S,), N, jnp.int32).at[dest].set(flat_tok[order])
    slot_w = jnp.zeros((S,), flat_w.dtype).at[dest].set(flat_w[order])
    blk_expert = jnp.minimum(jnp.searchsorted(pad_end, jnp.arange(n_blk, dtype=jnp.int32) * MOE_BLOCK,
                                              side='right'), N_EXPERTS - 1)
    x_pad = jnp.concatenate([x2d, jnp.zeros((1, D), x2d.dtype)], axis=0)
    xs = x_pad[slot_tok].reshape(n_blk, MOE_BLOCK, D)

    def expert_block(args):
        xb, e = args
        return (jax.nn.silu(xb @ w_gate[e]) * (xb @ w_up[e])) @ w_down[e]

    ys = lax.map(expert_block, (xs, blk_expert)).reshape(S, D)
    out = jnp.zeros((N + 1, D), ys.dtype).at[slot_tok].add(ys * slot_w[:, None].astype(ys.dtype))
    return out[:N]


def setup_inputs(seed: int = 0) -> dict:
    key = jax.random.key(seed)
    ks = jax.random.split(key, 24)
    f32 = jnp.float32
    nrm = lambda k, shape, fan: jax.random.normal(k, shape, f32) * (fan ** -0.5)
    gain = lambda k, shape: 1.0 + 0.05 * jax.random.normal(k, shape, f32)
    H_A_OUT = MLA_HEADS * MLA_V
    H_B_OUT = DSA_HEADS * DSA_HEAD_DIM
    return {
        "x": jax.random.normal(ks[0], (BATCH, SEQ, D_MODEL), f32),
        "meta_tokens": jax.random.normal(ks[1], (N_META, D_MODEL), f32),
        "norm_mix_g": gain(ks[2], (DEPTH, D_MODEL)),
        "w_in": nrm(ks[3], (DEPTH, D_MODEL, IN_COLS), D_MODEL),
        "mla_cq_norm_g": gain(ks[4], (DEPTH, MLA_Q_LORA)),
        "mla_ckv_norm_g": gain(ks[5], (DEPTH, MLA_KV_LORA)),
        "w_mla_uq": nrm(ks[6], (DEPTH, MLA_Q_LORA, MLA_HEADS * (MLA_NOPE + MLA_ROPE)), MLA_Q_LORA),
        "w_mla_ukv": nrm(ks[7], (DEPTH, MLA_KV_LORA, MLA_HEADS * (MLA_NOPE + MLA_V)), MLA_KV_LORA),
        "w_branch_a": nrm(ks[8], (DEPTH, H_A_OUT, D_MODEL), H_A_OUT),
        "w_branch_b": nrm(ks[9], (DEPTH, H_B_OUT, D_MODEL), H_B_OUT),
        "w_out": nrm(ks[10], (DEPTH, D_MODEL, D_MODEL), D_MODEL),
        "rel_bias": 0.5 * jax.random.normal(ks[11], (REL_BUCKETS, DSA_HEADS), f32),
        "norm_ffn_g": gain(ks[12], (DEPTH, D_MODEL)),
        "w_router_group": nrm(ks[13], (DEPTH, D_MODEL, N_GROUPS), D_MODEL),
        "b_router_group": 0.01 * jax.random.normal(ks[14], (DEPTH, N_GROUPS), f32),
        "w_router_expert": nrm(ks[15], (DEPTH, D_MODEL, N_EXPERTS), D_MODEL),
        "b_router_expert": 0.01 * jax.random.normal(ks[16], (DEPTH, N_EXPERTS), f32),
        "w_exp_gate": nrm(ks[17], (DEPTH, N_EXPERTS, D_MODEL, D_EXPERT), D_MODEL),
        "w_exp_up": nrm(ks[18], (DEPTH, N_EXPERTS, D_MODEL, D_EXPERT), D_MODEL),
        "w_exp_down": nrm(ks[19], (DEPTH, N_EXPERTS, D_EXPERT, D_MODEL), D_EXPERT),
        "norm_final_g": gain(ks[20], (D_MODEL,)),
    }


def reference(x, meta_tokens, norm_mix_g, w_in, mla_cq_norm_g, mla_ckv_norm_g, w_mla_uq, w_mla_ukv,
              w_branch_a, w_branch_b, w_out, rel_bias, norm_ffn_g, w_router_group, b_router_group,
              w_router_expert, b_router_expert, w_exp_gate, w_exp_up, w_exp_down, norm_final_g):
    B = x.shape[0]
    meta = jnp.broadcast_to(meta_tokens[None].astype(x.dtype), (B, N_META, D_MODEL))
    h = jnp.concatenate([meta, x], axis=1)
    L = h.shape[1]
    for layer in range(DEPTH):
        h = h + hybrid_mixer(rmsnorm(h, norm_mix_g[layer]), w_in[layer], mla_cq_norm_g[layer],
                             mla_ckv_norm_g[layer], w_mla_uq[layer], w_mla_ukv[layer],
                             w_branch_a[layer], w_branch_b[layer], w_out[layer], rel_bias)
        ffn = hier_moe(rmsnorm(h, norm_ffn_g[layer]).reshape(B * L, D_MODEL),
                       w_router_group[layer], b_router_group[layer], w_router_expert[layer],
                       b_router_expert[layer], w_exp_gate[layer], w_exp_up[layer], w_exp_down[layer])
        h = h + ffn.reshape(B, L, D_MODEL)
    return rmsnorm(h, norm_final_g)[:, N_META:]
```

```python
import functools
import math

import numpy as np
import jax
import jax.numpy as jnp
from jax import lax
from jax.experimental import pallas as pl
from jax.experimental.pallas import tpu as pltpu

CHUNK = 64
N_META = 16
NEG_INF = -1e30
RMS_EPS = 1e-6
ROPE_BASE = 10000.0
MLA_HEADS = 8
MLA_Q_LORA = 256
MLA_KV_LORA = 128
MLA_NOPE = 64
MLA_ROPE = 32
MLA_V = 64
DSA_HEADS = 8
DSA_HEAD_DIM = 64
IDX_HEADS = 8
IDX_DIM = 64
K_SEL_MAX = 256
REL_BUCKETS = 32
REL_MAX_DIST = 128
N_GROUPS = 4
EXPERTS_PER_GROUP = 8
N_EXPERTS = N_GROUPS * EXPERTS_PER_GROUP
D_EXPERT = 256

LANES = 128
BLK = 128
HEAD_PAD = 128
VMEM_LIMIT = 56 * 1024 * 1024
INT_MIN = -2 ** 31
PAD_CID = 2 ** 30

F32 = jnp.float32
BF16 = jnp.bfloat16


def _params(sem):
    return pltpu.CompilerParams(dimension_semantics=sem, vmem_limit_bytes=VMEM_LIMIT)


def _rms(x, g):
    return x * lax.rsqrt(jnp.mean(x * x, axis=-1, keepdims=True) + RMS_EPS) * g


def _dot(a, b):
    return jnp.dot(a, b, preferred_element_type=F32)


def _dot_t(a, b):
    return lax.dot_general(a, b, (((1,), (1,)), ((), ())), preferred_element_type=F32)


_C_CQ = (0, 256)
_C_CKV = (256, 384)
_C_KRX = (384, 512)
_C_KRY = (512, 640)
_C_QB = (640, 1152)
_C_KB = (1152, 1664)
_C_VB = (1664, 2176)
_C_QI = (2176, 2688)
_C_KI = (2688, 2816)
_C_WI = (2816, 2944)
_C_GA = (2944, 3968)
_C_GB = (3968, 4992)
_W1_COLS = 4992


def _proj_kernel(h_ref, g_ref, w1_ref, gcq_ref, gckv_ref, wuq_ref, wukv_ref, ct_ref, st_ref,
                 qm_ref, km_ref, vm_ref, qb_ref, kb_ref, vb_ref, qi_ref, ki_ref, wi_ref,
                 ga_ref, gb_ref):
    xb = _rms(h_ref[...], g_ref[...]).astype(BF16)

    def seg(c):
        return _dot(xb, w1_ref[:, c[0]:c[1]])

    ct = ct_ref[...]
    st = st_ref[...]
    nq = _rms(seg(_C_CQ), gcq_ref[...]).astype(BF16)
    qa = _dot(nq, wuq_ref[...])
    half = MLA_HEADS * HEAD_PAD
    for h in range(MLA_HEADS):
        lo, hi = h * HEAD_PAD, (h + 1) * HEAD_PAD
        qm_ref[:, lo:hi] = (qa[:, lo:hi] * ct + qa[:, half + lo:half + hi] * st).astype(BF16)
    nkv = _rms(seg(_C_CKV), gckv_ref[...]).astype(BF16)
    kva = _dot(nkv, wukv_ref[...])
    kr = seg(_C_KRX) * ct + seg(_C_KRY) * st
    for h in range(MLA_HEADS):
        lo, hi = h * HEAD_PAD, (h + 1) * HEAD_PAD
        km_ref[:, lo:hi] = (kva[:, lo:hi] + kr).astype(BF16)
    vm_ref[...] = kva[:, half:].astype(BF16)
    qb_ref[...] = seg(_C_QB).astype(BF16)
    kb_ref[...] = seg(_C_KB).astype(BF16)
    vb_ref[...] = seg(_C_VB).astype(BF16)
    qi_ref[...] = seg(_C_QI).astype(BF16)
    ki_ref[...] = seg(_C_KI).astype(BF16)
    wi_ref[...] = seg(_C_WI)
    ga_ref[...] = seg(_C_GA)
    gb_ref[...] = seg(_C_GB)


def _pack_w1(w_in):
    D = w_in.shape[0]
    offs = np.cumsum([0, MLA_Q_LORA, MLA_KV_LORA, MLA_ROPE, 512, 512, 512, 512, IDX_DIM, IDX_HEADS, D, D])
    c = lambda i: w_in[:, offs[i]:offs[i + 1]]
    z = lambda n: jnp.zeros((D, n), w_in.dtype)
    k_r = c(2)
    hr = MLA_ROPE // 2
    k_r_sw = jnp.concatenate([k_r[:, hr:], k_r[:, :hr]], axis=1)
    k_i = c(7)
    cols = [
        c(0), c(1),
        z(MLA_NOPE), k_r, z(LANES - MLA_NOPE - MLA_ROPE),
        z(MLA_NOPE), k_r_sw, z(LANES - MLA_NOPE - MLA_ROPE),
        c(3), c(4), c(5),
        c(6) * (IDX_DIM ** -0.5),
        k_i, k_i,
        c(8), z(LANES - IDX_HEADS),
        c(9), c(10),
    ]
    w1 = jnp.concatenate(cols, axis=1)
    assert w1.shape[1] == _W1_COLS
    return w1.astype(BF16)


def _pack_wuq(w_uq):
    R = w_uq.shape[0]
    w = w_uq.reshape(R, MLA_HEADS, MLA_NOPE + MLA_ROPE)
    nope, rope = w[..., :MLA_NOPE], w[..., MLA_NOPE:]
    hr = MLA_ROPE // 2
    rope_sw = jnp.concatenate([rope[..., hr:], rope[..., :hr]], axis=-1)
    zt = jnp.zeros((R, MLA_HEADS, HEAD_PAD - MLA_NOPE - MLA_ROPE), w.dtype)
    a = jnp.concatenate([nope, rope, zt], axis=-1).reshape(R, MLA_HEADS * HEAD_PAD)
    b = jnp.concatenate([jnp.zeros_like(nope), rope_sw, zt], axis=-1).reshape(R, MLA_HEADS * HEAD_PAD)
    return jnp.concatenate([a, b], axis=1).astype(BF16)


def _pack_wukv(w_ukv):
    R = w_ukv.shape[0]
    w = w_ukv.reshape(R, MLA_HEADS, MLA_NOPE + MLA_V)
    k = jnp.concatenate([w[..., :MLA_NOPE], jnp.zeros((R, MLA_HEADS, HEAD_PAD - MLA_NOPE), w.dtype)], axis=-1)
    v = w[..., MLA_NOPE:]
    return jnp.concatenate([k.reshape(R, -1), v.reshape(R, -1)], axis=1).astype(BF16)


def _rope_tables(pos):
    half = MLA_ROPE // 2
    inv = ROPE_BASE ** (-jnp.arange(half, dtype=F32) / half)
    ang = pos.astype(F32)[:, None] * inv[None, :]
    cos, sin = jnp.cos(ang), jnp.sin(ang)
    n = pos.shape[0]
    tail = jnp.zeros((n, HEAD_PAD - MLA_NOPE - MLA_ROPE), F32)
    ct = jnp.concatenate([jnp.ones((n, MLA_NOPE), F32), cos, cos, tail], axis=1)
    st = jnp.concatenate([jnp.zeros((n, MLA_NOPE), F32), -sin, sin, tail], axis=1)
    return ct, st


def _proj_call(hp, g, w1, gcq, gckv, wuq, wukv, ct, st, tm):
    Np, D = hp.shape
    Lp = ct.shape[0]
    per_b = Lp // tm
    row = lambda w: pl.BlockSpec((tm, w), lambda i: (i, 0))
    full = lambda a: pl.BlockSpec(a.shape, lambda i: (0, 0))
    tab = pl.BlockSpec((tm, HEAD_PAD), lambda i: (i % per_b, 0))
    widths = [(MLA_HEADS * HEAD_PAD, BF16), (MLA_HEADS * HEAD_PAD, BF16), (MLA_HEADS * MLA_V, BF16),
              (512, BF16), (512, BF16), (512, BF16), (512, BF16), (LANES, BF16), (LANES, F32),
              (D, F32), (D, F32)]
    return pl.pallas_call(
        _proj_kernel,
        grid=(Np // tm,),
        in_specs=[row(D), full(g), full(w1), full(gcq), full(gckv), full(wuq), full(wukv), tab, tab],
        out_specs=[row(w) for w, _ in widths],
        out_shape=[jax.ShapeDtypeStruct((Np, w), dt) for w, dt in widths],
        compiler_params=_params(("parallel",)),
        name="proj",
    )(hp, g, w1, gcq, gckv, wuq, wukv, ct, st)


def _softmax_step(s, m, l, acc, v):
    m_new = jnp.maximum(m, jnp.max(s, axis=1, keepdims=True))
    alpha = jnp.exp(m - m_new)
    p = jnp.exp(s - m_new)
    l = alpha * l + jnp.sum(p, axis=1, keepdims=True)
    acc = alpha * acc + _dot(p.astype(BF16), v)
    return m_new, l, acc


def _mla_kernel(cq_ref, ck_ref, q_ref, k_ref, v_ref, o_ref):
    j = pl.program_id(2)
    cq = cq_ref[...]
    scale = (MLA_NOPE + MLA_ROPE) ** -0.5
    outs = []
    for hh in range(2):
        q = q_ref[:, hh * HEAD_PAD:(hh + 1) * HEAD_PAD]

        def body(kb, carry, q=q, hh=hh):
            ks = pl.multiple_of(kb * BLK, BLK)
            k = k_ref[pl.ds(ks, BLK), hh * HEAD_PAD:(hh + 1) * HEAD_PAD]
            s = _dot_t(q, k) * scale
            s = jnp.where(ck_ref[kb] <= cq, s, NEG_INF)
            return _softmax_step(s, *carry, v_ref[pl.ds(ks, BLK), :])

        init = (jnp.full((BLK, 1), -jnp.inf, F32), jnp.zeros((BLK, 1), F32), jnp.zeros((BLK, LANES), F32))
        _, l, acc = lax.fori_loop(0, j + 1, body, init)
        outs.append(acc / l)
    lane = lax.broadcasted_iota(jnp.int32, (BLK, LANES), 1)
    o_ref[...] = jnp.where(lane < MLA_V, outs[0], outs[1]).astype(BF16)


def _mla_call(cid_q, cid_k, qm, km, vm, B, Lp):
    nb = Lp // BLK
    pairs = MLA_HEADS // 2
    return pl.pallas_call(
        _mla_kernel,
        grid=(B, pairs, nb),
        in_specs=[
            pl.BlockSpec((BLK, 1), lambda b, p, j: (j, 0)),
            pl.BlockSpec((nb, 1, BLK), lambda b, p, j: (0, 0, 0)),
            pl.BlockSpec((BLK, 2 * HEAD_PAD), lambda b, p, j: (b * nb + j, p)),
            pl.BlockSpec((Lp, 2 * HEAD_PAD), lambda b, p, j: (b, p)),
            pl.BlockSpec((Lp, 2 * MLA_V), lambda b, p, j: (b, p)),
        ],
        out_specs=pl.BlockSpec((BLK, 2 * MLA_V), lambda b, p, j: (b * nb + j, p)),
        out_shape=jax.ShapeDtypeStruct((B * Lp, MLA_HEADS * MLA_V), BF16),
        compiler_params=_params(("parallel", "parallel", "arbitrary")),
        name="mla_attn",
    )(cid_q, cid_k, qm, km, vm)


_SEL_UNROLL = 4
_IDX_BITS = 13


def _dsa_kernel(cq_ref, ck_ref, qi_ref, wi_ref, qb_ref, ki_ref, kb_ref, vb_ref, bias_ref, o_ref,
                keys_scr, sel_scr, qs_scr, wb_scr, mstar_scr, *, k_sel):
    j = pl.program_id(1)
    nkb = j + 1
    nch = (j + _SEL_UNROLL) // _SEL_UNROLL
    cq = cq_ref[...]
    lane = lax.broadcasted_iota(jnp.int32, (BLK, LANES), 1)
    low = lane < DSA_HEAD_DIM

    for p in range(IDX_HEADS // 2):
        qp = qi_ref[:, p * LANES:(p + 1) * LANES]
        qs_scr[2 * p] = jnp.where(low, qp, jnp.zeros_like(qp))
        qs_scr[2 * p + 1] = jnp.where(low, jnp.zeros_like(qp), qp)
    wi = wi_ref[...] * (IDX_HEADS ** -0.5)
    for h in range(IDX_HEADS):
        wb_scr[h] = jnp.broadcast_to(wi[:, h:h + 1], (BLK, LANES))

    def score_body(kb, c):
        kt = ki_ref[pl.ds(pl.multiple_of(kb * BLK, BLK), BLK), :]
        score = jnp.zeros((BLK, LANES), F32)
        for h in range(IDX_HEADS):
            score = score + wb_scr[h] * jnp.maximum(_dot_t(qs_scr[h], kt), 0.0)
        score = score + 0.0
        bits = pltpu.bitcast(score, jnp.int32)
        key = bits ^ ((bits >> 31) & 0x7FFFFFFF)
        keys_scr[kb] = jnp.where(ck_ref[kb] <= cq, key, INT_MIN)
        return c

    lax.fori_loop(0, nkb, score_body, 0)

    def fill_body(kb, c):
        keys_scr[kb] = jnp.full((BLK, LANES), INT_MIN, jnp.int32)
        return c

    lax.fori_loop(nkb, nch * _SEL_UNROLL, fill_body, 0)

    def count(pred):
        def cbody(c, acc):
            for u in range(_SEL_UNROLL):
                kbi = c * _SEL_UNROLL + u
                acc = acc + pred(keys_scr[kbi], kbi)
            return acc
        acc = lax.fori_loop(0, nch, cbody, jnp.zeros((BLK, LANES), jnp.int32))
        return jnp.sum(acc, axis=1, keepdims=True)

    one = lambda m: jnp.where(m, 1, 0)

    t0 = jnp.where(count(lambda b, _: one(b >= 0)) >= k_sel, 0, INT_MIN)

    def bs_body(i, t):
        cand = t | lax.shift_left(jnp.int32(1), 30 - i)
        return jnp.where(count(lambda b, _: one(b >= cand)) >= k_sel, cand, t)

    t = lax.fori_loop(0, 31, bs_body, t0)
    cnt_gt = count(lambda b, _: one(b > t))
    cnt_ge = count(lambda b, _: one(b >= t))
    need = k_sel - cnt_gt

    mstar_scr[...] = jnp.full((BLK, 1), 2 ** _IDX_BITS, jnp.int32)
    tied = jnp.max(jnp.where((cnt_ge > k_sel) & (t != INT_MIN), 1, 0))

    @pl.when(tied > 0)
    def _():
        def tie_body(i, r):
            cand = r | lax.shift_left(jnp.int32(1), _IDX_BITS - 1 - i)
            f = count(lambda b, kbi: jnp.where(b == t, one(kbi * BLK + lane < cand), 0))
            return jnp.where(f < need, cand, r)
        r = lax.fori_loop(0, _IDX_BITS, tie_body, jnp.zeros((BLK, 1), jnp.int32))
        mstar_scr[...] = r + 1

    mstar = mstar_scr[...]

    def sel_body(kb, c):
        b = keys_scr[kb]
        tie_ok = jnp.where(b == t, one(kb * BLK + lane < mstar), 0)
        sel = jnp.where(b == INT_MIN, 0, jnp.where(b > t, 1, tie_ok))
        sel_scr[kb] = jnp.where(sel > 0, 0.0, NEG_INF)
        return c

    lax.fori_loop(0, nkb, sel_body, 0)

    scale = DSA_HEAD_DIM ** -0.5
    for p in range(DSA_HEADS // 2):
        qp = qb_ref[:, p * LANES:(p + 1) * LANES]
        outs = []
        for hh in range(2):
            h = 2 * p + hh
            q = jnp.where(low, qp, jnp.zeros_like(qp)) if hh == 0 else jnp.where(low, jnp.zeros_like(qp), qp)

            def body(kb, carry, q=q, h=h, p=p):
                ks = pl.multiple_of(kb * BLK, BLK)
                k = kb_ref[pl.ds(ks, BLK), p * LANES:(p + 1) * LANES]
                ti = jnp.where(kb == j, jnp.where(j == 0, 3, 0),
                               jnp.where(kb == j - 1, jnp.where(j == 1, 2, 1), 4))
                s = _dot_t(q, k) * scale + bias_ref[ti, h] + sel_scr[kb]
                return _softmax_step(s, *carry, vb_ref[pl.ds(ks, BLK), p * LANES:(p + 1) * LANES])

            init = (jnp.full((BLK, 1), -jnp.inf, F32), jnp.zeros((BLK, 1), F32), jnp.zeros((BLK, LANES), F32))
            _, l, acc = lax.fori_loop(0, nkb, body, init)
            outs.append(acc / l)
        o_ref[:, p * LANES:(p + 1) * LANES] = jnp.where(low, outs[0], outs[1]).astype(BF16)


def _dsa_call(cid_q, cid_k, qi, wi, qb, ki, kb, vb, bias, B, Lp, k_sel):
    nb = Lp // BLK
    nbw = -(-nb // _SEL_UNROLL) * _SEL_UNROLL
    W = DSA_HEADS * DSA_HEAD_DIM
    qrow = lambda w: pl.BlockSpec((BLK, w), lambda b, j: (b * nb + j, 0))
    kv = lambda w: pl.BlockSpec((Lp, w), lambda b, j: (b, 0))
    return pl.pallas_call(
        functools.partial(_dsa_kernel, k_sel=k_sel),
        grid=(B, nb),
        in_specs=[
            pl.BlockSpec((BLK, 1), lambda b, j: (j, 0)),
            pl.BlockSpec((nb, 1, BLK), lambda b, j: (0, 0, 0)),
            qrow(W), qrow(LANES), qrow(W), kv(LANES), kv(W), kv(W),
            pl.BlockSpec(bias.shape, lambda b, j: (0, 0, 0, 0)),
        ],
        out_specs=qrow(W),
        out_shape=jax.ShapeDtypeStruct((B * Lp, W), BF16),
        scratch_shapes=[
            pltpu.VMEM((nbw, BLK, LANES), jnp.int32),
            pltpu.VMEM((nbw, BLK, LANES), F32),
            pltpu.VMEM((IDX_HEADS, BLK, LANES), BF16),
            pltpu.VMEM((IDX_HEADS, BLK, LANES), F32),
            pltpu.VMEM((BLK, 1), jnp.int32),
        ],
        compiler_params=_params(("parallel", "arbitrary")),
        name="dsa",
    )(cid_q, cid_k, qi, wi, qb, ki, kb, vb, bias)


def _t5_bucket(rel):
    nb = REL_BUCKETS // 2
    max_exact = nb // 2
    n = jnp.abs(rel)
    large = max_exact + (jnp.log(jnp.maximum(n, 1).astype(F32) / max_exact)
                         / math.log(REL_MAX_DIST / max_exact) * (nb - max_exact)).astype(jnp.int32)
    large = jnp.minimum(large, nb - 1)
    return jnp.where(rel > 0, nb, 0) + jnp.where(n < max_exact, n, large)


def _bias_tiles(rel_bias):
    q = jnp.arange(BLK, dtype=jnp.int32)[:, None]
    k = jnp.arange(BLK, dtype=jnp.int32)[None, :]
    rels = jnp.stack([k - q, k - q - BLK, k - (q + N_META), k - q,
                      jnp.full((BLK, BLK), -(BLK + 1), jnp.int32)])
    return jnp.transpose(rel_bias[_t5_bucket(rels)], (0, 3, 1, 2)).astype(F32)


def _merge_kernel(h_ref, oa_ref, ob_ref, ga_ref, gb_ref, wa_ref, wb_ref, wo_ref, gf_ref, wr_ref, br_ref,
                  h1_ref, xn_ref, gate_ref):
    y = (jax.nn.sigmoid(ga_ref[...]) * _dot(oa_ref[...], wa_ref[...])
         + jax.nn.sigmoid(gb_ref[...]) * _dot(ob_ref[...], wb_ref[...]))
    h1 = h_ref[...] + _dot(y.astype(BF16), wo_ref[...])
    h1_ref[...] = h1
    xn = _rms(h1, gf_ref[...])
    xn_ref[...] = xn.astype(BF16)
    logits = jnp.dot(xn, wr_ref[...], preferred_element_type=F32, precision=lax.Precision.HIGHEST) + br_ref[...]
    tm = logits.shape[0]
    lane = lax.broadcasted_iota(jnp.int32, (tm, LANES), 1)
    ninf = -jnp.inf
    gl = jnp.where((lane >= N_EXPERTS) & (lane < N_EXPERTS + N_GROUPS), logits, ninf)
    gmax = jnp.max(gl, axis=1, keepdims=True)
    gsel = jnp.min(jnp.where(gl == gmax, lane, LANES), axis=1, keepdims=True) - N_EXPERTS
    p_group = 1.0 / jnp.sum(jnp.exp(gl - gmax), axis=1, keepdims=True)
    lo = gsel * EXPERTS_PER_GROUP
    el = jnp.where((lane >= lo) & (lane < lo + EXPERTS_PER_GROUP), logits, ninf)
    m1 = jnp.max(el, axis=1, keepdims=True)
    i1 = jnp.min(jnp.where(el == m1, lane, LANES), axis=1, keepdims=True)
    el2 = jnp.where(lane == i1, ninf, el)
    m2 = jnp.max(el2, axis=1, keepdims=True)
    i2 = jnp.min(jnp.where(el2 == m2, lane, LANES), axis=1, keepdims=True)
    e2 = jnp.exp(m2 - m1)
    w1 = p_group / (1.0 + e2)
    w2 = p_group * e2 / (1.0 + e2)
    gate_ref[...] = jnp.where(lane == i1, w1, 0.0) + jnp.where(lane == i2, w2, 0.0)


def _merge_call(hp, oa, ob, ga, gb, wa, wb, wo, gf, wr, br, tm):
    Np, D = hp.shape
    row = lambda w: pl.BlockSpec((tm, w), lambda i: (i, 0))
    full = lambda a: pl.BlockSpec(a.shape, lambda i: (0, 0))
    return pl.pallas_call(
        _merge_kernel,
        grid=(Np // tm,),
        in_specs=[row(D), row(oa.shape[1]), row(ob.shape[1]), row(D), row(D),
                  full(wa), full(wb), full(wo), full(gf), full(wr), full(br)],
        out_specs=[row(D), row(D), row(LANES)],
        out_shape=[jax.ShapeDtypeStruct((Np, D), F32), jax.ShapeDtypeStruct((Np, D), BF16),
                   jax.ShapeDtypeStruct((Np, LANES), F32)],
        compiler_params=_params(("parallel",)),
        name="merge_route",
    )(hp, oa, ob, ga, gb, wa, wb, wo, gf, wr, br)


def _moe_kernel(x_ref, gate_ref, h1_ref, wg_ref, wu_ref, wd_ref, gfin_ref, o_ref, acc_ref):
    e = pl.program_id(1)

    @pl.when(e == 0)
    def _():
        acc_ref[...] = jnp.zeros_like(acc_ref)

    x = x_ref[...]
    a = _dot(x, wg_ref[0].astype(BF16))
    u = _dot(x, wu_ref[0].astype(BF16))
    hmid = (a * jax.nn.sigmoid(a) * u).astype(BF16)
    y = _dot(hmid, wd_ref[0].astype(BF16))
    gates = gate_ref[...]
    lane = lax.broadcasted_iota(jnp.int32, gates.shape, 1)
    w = jnp.sum(jnp.where(lane == e, gates, 0.0), axis=1, keepdims=True)
    acc_ref[...] += w * y

    @pl.when(e == pl.num_programs(1) - 1)
    def _():
        o_ref[...] = _rms(h1_ref[...] + acc_ref[...], gfin_ref[...])


def _moe_call(xn, gates, h1, wg, wu, wd, gfin, tm):
    Np, D = h1.shape
    E = wg.shape[0]
    row = lambda w: pl.BlockSpec((tm, w), lambda i, e: (i, 0))
    return pl.pallas_call(
        _moe_kernel,
        grid=(Np // tm, E),
        in_specs=[row(D), row(LANES), row(D),
                  pl.BlockSpec((1, D, D_EXPERT), lambda i, e: (e, 0, 0)),
                  pl.BlockSpec((1, D, D_EXPERT), lambda i, e: (e, 0, 0)),
                  pl.BlockSpec((1, D_EXPERT, D), lambda i, e: (e, 0, 0)),
                  pl.BlockSpec(gfin.shape, lambda i, e: (0, 0))],
        out_specs=row(D),
        out_shape=jax.ShapeDtypeStruct((Np, D), F32),
        scratch_shapes=[pltpu.VMEM((tm, D), F32)],
        compiler_params=_params(("parallel", "arbitrary")),
        name="moe",
    )(xn, gates, h1, wg, wu, wd, gfin)


def _row_tile(Lp, cap):
    nb = Lp // BLK
    best = 1
    for d in range(1, nb + 1):
        if nb % d == 0 and d * BLK <= cap:
            best = d
    return best * BLK


def kernel(x, meta_tokens, norm_mix_g, w_in, mla_cq_norm_g, mla_ckv_norm_g, w_mla_uq, w_mla_ukv,
           w_branch_a, w_branch_b, w_out, rel_bias, norm_ffn_g, w_router_group, b_router_group,
           w_router_expert, b_router_expert, w_exp_gate, w_exp_up, w_exp_down, norm_final_g):
    B, S, D = x.shape
    assert S % BLK == 0 and norm_mix_g.shape[0] == 1
    Lp = BLK + S
    Np = B * Lp
    k_sel = min(K_SEL_MAX, S // 4)

    meta_blk = jnp.concatenate([meta_tokens.astype(x.dtype), jnp.zeros((BLK - N_META, D), x.dtype)], axis=0)
    hp = jnp.concatenate([jnp.broadcast_to(meta_blk[None], (B, BLK, D)), x], axis=1).reshape(Np, D)
    idx = np.arange(Lp)
    valid = (idx < N_META) | (idx >= BLK)
    pos = np.where(idx < BLK, idx, idx - BLK + N_META)
    cid = np.where(pos < N_META, 0, 1 + (pos - N_META) // CHUNK)
    cid_q = jnp.asarray(np.where(valid, cid, 0).astype(np.int32).reshape(Lp, 1))
    cid_k = jnp.asarray(np.where(valid, cid, PAD_CID).astype(np.int32).reshape(Lp // BLK, 1, BLK))
    ct, st = _rope_tables(jnp.asarray(pos.astype(np.int32)))

    tm = _row_tile(Lp, 512)
    qm, km, vm, qb, kb, vb, qi, ki, wi, ga, gb = _proj_call(
        hp, norm_mix_g.reshape(1, D), _pack_w1(w_in[0]), mla_cq_norm_g.reshape(1, -1),
        mla_ckv_norm_g.reshape(1, -1), _pack_wuq(w_mla_uq[0]), _pack_wukv(w_mla_ukv[0]), ct, st, tm)

    o_a = _mla_call(cid_q, cid_k, qm, km, vm, B, Lp)
    o_b = _dsa_call(cid_q, cid_k, qi, wi, qb, ki, kb, vb, _bias_tiles(rel_bias), B, Lp, k_sel)

    w_r = jnp.concatenate([w_router_expert[0], w_router_group[0],
                           jnp.zeros((D, LANES - N_EXPERTS - N_GROUPS), F32)], axis=1)
    b_r = jnp.concatenate([b_router_expert[0], b_router_group[0],
                           jnp.zeros((LANES - N_EXPERTS - N_GROUPS,), F32)]).reshape(1, LANES)
    h1, xn, gates = _merge_call(hp, o_a, o_b, ga, gb, w_branch_a[0].astype(BF16), w_branch_b[0].astype(BF16),
                                w_out[0].astype(BF16), norm_ffn_g.reshape(1, D), w_r, b_r, tm)

    out = _moe_call(xn, gates, h1, w_exp_gate[0], w_exp_up[0], w_exp_down[0],
                    norm_final_g.reshape(1, D), _row_tile(Lp, 1536))
    return out.reshape(B, Lp, D)[:, BLK:]
```

```python
import functools
import math

import numpy as np
import jax
import jax.numpy as jnp
from jax import lax
from jax.experimental import pallas as pl
from jax.experimental.pallas import tpu as pltpu

CHUNK = 64
N_META = 16
NEG_INF = -1e30
RMS_EPS = 1e-6
ROPE_BASE = 10000.0
MLA_HEADS = 8
MLA_Q_LORA = 256
MLA_KV_LORA = 128
MLA_NOPE = 64
MLA_ROPE = 32
MLA_V = 64
DSA_HEADS = 8
DSA_HEAD_DIM = 64
IDX_HEADS = 8
IDX_DIM = 64
K_SEL_MAX = 256
REL_BUCKETS = 32
REL_MAX_DIST = 128
N_GROUPS = 4
EXPERTS_PER_GROUP = 8
N_EXPERTS = N_GROUPS * EXPERTS_PER_GROUP
D_EXPERT = 256

LANES = 128
TQ = 256
HEAD_PAD = 128
VMEM_LIMIT = 56 * 1024 * 1024
INT_MIN = -2 ** 31
LOG2E = math.log2(math.e)

F32 = jnp.float32
BF16 = jnp.bfloat16


def _params(sem):
    return pltpu.CompilerParams(dimension_semantics=sem, vmem_limit_bytes=VMEM_LIMIT)


def _rms(x, g):
    return x * lax.rsqrt(jnp.mean(x * x, axis=-1, keepdims=True) + RMS_EPS) * g


def _dot(a, b):
    return jnp.dot(a, b, preferred_element_type=F32)


def _dot_t(a, b):
    return lax.dot_general(a, b, (((1,), (1,)), ((), ())), preferred_element_type=F32)


_C_CQ = (0, 256)
_C_CKV = (256, 384)
_C_KRX = (384, 512)
_C_KRY = (512, 640)
_C_QB = (640, 1152)
_C_KB = (1152, 1664)
_C_VB = (1664, 2176)
_C_QI = (2176, 2688)
_C_KI = (2688, 2816)
_C_WI = (2816, 2944)
_C_GA = (2944, 3968)
_C_GB = (3968, 4992)
_W1_COLS = 4992


def _proj_kernel(x_ref, meta_ref, g_ref, w1_ref, gcq_ref, gckv_ref, wuq_ref, wukv_ref, ct_ref, st_ref,
                 qm_ref, km_ref, vm_ref, qb_ref, kb_ref, vb_ref, qi_ref, ki_ref, wi_ref,
                 ga_ref, gb_ref):
    is_meta = pl.program_id(0) == pl.num_programs(0) - 1
    h = jnp.where(is_meta, meta_ref[...], x_ref[...])
    xb = _rms(h, g_ref[...]).astype(BF16)

    def seg(c):
        return _dot(xb, w1_ref[:, c[0]:c[1]])

    ct = ct_ref[...]
    st = st_ref[...]
    nq = _rms(seg(_C_CQ), gcq_ref[...]).astype(BF16)
    qa = _dot(nq, wuq_ref[...])
    half = MLA_HEADS * HEAD_PAD
    for hd in range(MLA_HEADS):
        lo, hi = hd * HEAD_PAD, (hd + 1) * HEAD_PAD
        qm_ref[:, lo:hi] = (qa[:, lo:hi] * ct + qa[:, half + lo:half + hi] * st).astype(BF16)
    nkv = _rms(seg(_C_CKV), gckv_ref[...]).astype(BF16)
    kva = _dot(nkv, wukv_ref[...])
    kr = seg(_C_KRX) * ct + seg(_C_KRY) * st
    for hd in range(MLA_HEADS):
        lo, hi = hd * HEAD_PAD, (hd + 1) * HEAD_PAD
        km_ref[:, lo:hi] = (kva[:, lo:hi] + kr).astype(BF16)
    vm_ref[...] = kva[:, half:].astype(BF16)
    qb_ref[...] = seg(_C_QB).astype(BF16)
    kb_ref[...] = seg(_C_KB).astype(BF16)
    vb_ref[...] = seg(_C_VB).astype(BF16)
    qi_ref[...] = seg(_C_QI).astype(BF16)
    ki_ref[...] = seg(_C_KI).astype(BF16)
    wi_ref[...] = seg(_C_WI)
    ga_ref[...] = seg(_C_GA)
    gb_ref[...] = seg(_C_GB)


def _pack_w1(w_in):
    D = w_in.shape[0]
    offs = np.cumsum([0, MLA_Q_LORA, MLA_KV_LORA, MLA_ROPE, 512, 512, 512, 512, IDX_DIM, IDX_HEADS, D, D])
    c = lambda i: w_in[:, offs[i]:offs[i + 1]]
    z = lambda n: jnp.zeros((D, n), w_in.dtype)
    k_r = c(2)
    hr = MLA_ROPE // 2
    k_r_sw = jnp.concatenate([k_r[:, hr:], k_r[:, :hr]], axis=1)
    cols = [
        c(0), c(1),
        z(MLA_NOPE), k_r, z(LANES - MLA_NOPE - MLA_ROPE),
        z(MLA_NOPE), k_r_sw, z(LANES - MLA_NOPE - MLA_ROPE),
        c(3), c(4), c(5),
        c(6) * (IDX_DIM ** -0.5),
        c(7), c(7),
        c(8), z(LANES - IDX_HEADS),
        c(9), c(10),
    ]
    w1 = jnp.concatenate(cols, axis=1)
    assert w1.shape[1] == _W1_COLS
    return w1.astype(BF16)


def _pack_wuq(w_uq):
    R = w_uq.shape[0]
    w = w_uq.reshape(R, MLA_HEADS, MLA_NOPE + MLA_ROPE)
    nope, rope = w[..., :MLA_NOPE], w[..., MLA_NOPE:]
    hr = MLA_ROPE // 2
    rope_sw = jnp.concatenate([rope[..., hr:], rope[..., :hr]], axis=-1)
    zt = jnp.zeros((R, MLA_HEADS, HEAD_PAD - MLA_NOPE - MLA_ROPE), w.dtype)
    a = jnp.concatenate([nope, rope, zt], axis=-1).reshape(R, MLA_HEADS * HEAD_PAD)
    b = jnp.concatenate([jnp.zeros_like(nope), rope_sw, zt], axis=-1).reshape(R, MLA_HEADS * HEAD_PAD)
    return jnp.concatenate([a, b], axis=1).astype(BF16)


def _pack_wukv(w_ukv):
    R = w_ukv.shape[0]
    w = w_ukv.reshape(R, MLA_HEADS, MLA_NOPE + MLA_V)
    k = jnp.concatenate([w[..., :MLA_NOPE], jnp.zeros((R, MLA_HEADS, HEAD_PAD - MLA_NOPE), w.dtype)], axis=-1)
    v = w[..., MLA_NOPE:]
    return jnp.concatenate([k.reshape(R, -1), v.reshape(R, -1)], axis=1).astype(BF16)


def _rope_tables(pos):
    half = MLA_ROPE // 2
    inv = ROPE_BASE ** (-jnp.arange(half, dtype=F32) / half)
    ang = pos.astype(F32)[:, None] * inv[None, :]
    cos, sin = jnp.cos(ang), jnp.sin(ang)
    n = pos.shape[0]
    tail = jnp.zeros((n, HEAD_PAD - MLA_NOPE - MLA_ROPE), F32)
    ct = jnp.concatenate([jnp.ones((n, MLA_NOPE), F32), cos, cos, tail], axis=1)
    st = jnp.concatenate([jnp.zeros((n, MLA_NOPE), F32), -sin, sin, tail], axis=1)
    return ct, st


def _proj_call(x2d, meta_tile, g, w1, gcq, gckv, wuq, wukv, ct, st, S):
    BS, D = x2d.shape
    nt = BS // TQ + 1
    per_b = S // TQ
    row = lambda w: pl.BlockSpec((TQ, w), lambda i: (i, 0))
    full = lambda a: pl.BlockSpec(a.shape, lambda i: (0, 0))
    tab = pl.BlockSpec((TQ, HEAD_PAD), lambda i: (jnp.where(i == nt - 1, per_b, i % per_b), 0))
    widths = [(MLA_HEADS * HEAD_PAD, BF16), (MLA_HEADS * HEAD_PAD, BF16), (MLA_HEADS * MLA_V, BF16),
              (512, BF16), (512, BF16), (512, BF16), (512, BF16), (LANES, BF16), (LANES, F32),
              (D, F32), (D, F32)]
    return pl.pallas_call(
        _proj_kernel,
        grid=(nt,),
        in_specs=[pl.BlockSpec((TQ, D), lambda i: (jnp.minimum(i, nt - 2), 0)), full(meta_tile),
                  full(g), full(w1), full(gcq), full(gckv), full(wuq), full(wukv), tab, tab],
        out_specs=[row(w) for w, _ in widths],
        out_shape=[jax.ShapeDtypeStruct((nt * TQ, w), dt) for w, dt in widths],
        compiler_params=_params(("parallel",)),
        name="proj",
    )(x2d, meta_tile, g, w1, gcq, gckv, wuq, wukv, ct, st)


def _first(z, v):
    m = jnp.max(z, axis=1, keepdims=True)
    p = jnp.exp2(z - m)
    return m, jnp.sum(p, axis=1, keepdims=True), _dot(p.astype(BF16), v)


def _online(z, m, l, acc, v):
    m_new = jnp.maximum(m, jnp.max(z, axis=1, keepdims=True))
    alpha = jnp.exp2(m - m_new)
    p = jnp.exp2(z - m_new)
    l = alpha * l + jnp.sum(p, axis=1, keepdims=True)
    acc = alpha * acc + _dot(p.astype(BF16), v)
    return m_new, l, acc


def _tile_ids(nq):
    g = pl.program_id(1)
    is_meta = g == pl.num_programs(1) - 1
    return is_meta, jnp.where(is_meta, 0, g % nq)


def _chunk_mask():
    row = lax.broadcasted_iota(jnp.int32, (TQ, TQ), 0)
    col = lax.broadcasted_iota(jnp.int32, (TQ, TQ), 1)
    shift = CHUNK.bit_length() - 1
    return (col >> shift) <= (row >> shift)


def _mla_kernel(q_ref, k_ref, v_ref, km_ref, vmt_ref, o_ref, *, nq):
    is_meta, j = _tile_ids(nq)
    n_full = jnp.where(is_meta, 0, j)
    n_diag = jnp.where(is_meta, 0, 1)
    c = (MLA_NOPE + MLA_ROPE) ** -0.5 * LOG2E
    qs = [q_ref[:, hh * HEAD_PAD:(hh + 1) * HEAD_PAD] for hh in range(2)]
    meta_ok = lax.broadcasted_iota(jnp.int32, (TQ, LANES), 1) < N_META
    diag_ok = _chunk_mask()

    state = []
    for hh in range(2):
        z = jnp.where(meta_ok, _dot_t(qs[hh], km_ref[:, hh * HEAD_PAD:(hh + 1) * HEAD_PAD]) * c, NEG_INF)
        state += list(_first(z, vmt_ref[...]))

    def tile(t, st, masked):
        ks = pl.multiple_of(t * TQ, TQ)
        v = v_ref[pl.ds(ks, TQ), :]
        out = []
        for hh in range(2):
            z = _dot_t(qs[hh], k_ref[pl.ds(ks, TQ), hh * HEAD_PAD:(hh + 1) * HEAD_PAD]) * c
            if masked:
                z = jnp.where(diag_ok, z, NEG_INF)
            out += list(_online(z, *st[3 * hh:3 * hh + 3], v))
        return tuple(out)

    st = lax.fori_loop(0, n_full, lambda t, s: tile(t, s, False), tuple(state))
    st = lax.fori_loop(n_full, n_full + n_diag, lambda t, s: tile(t, s, True), st)
    lane = lax.broadcasted_iota(jnp.int32, (TQ, LANES), 1)
    o_ref[...] = jnp.where(lane < MLA_V, st[2] / st[1], st[5] / st[4]).astype(BF16)


def _mla_call(qm, km, vm, B, S):
    nq = S // TQ
    ng = B * nq + 1
    pairs = MLA_HEADS // 2
    bidx = lambda g: jnp.minimum(g // nq, B - 1)
    mrow = B * S // LANES
    return pl.pallas_call(
        functools.partial(_mla_kernel, nq=nq),
        grid=(pairs, ng),
        in_specs=[
            pl.BlockSpec((TQ, 2 * HEAD_PAD), lambda p, g: (g, p)),
            pl.BlockSpec((S, 2 * HEAD_PAD), lambda p, g: (bidx(g), p)),
            pl.BlockSpec((S, 2 * MLA_V), lambda p, g: (bidx(g), p)),
            pl.BlockSpec((LANES, 2 * HEAD_PAD), lambda p, g: (mrow, p)),
            pl.BlockSpec((LANES, 2 * MLA_V), lambda p, g: (mrow, p)),
        ],
        out_specs=pl.BlockSpec((TQ, 2 * MLA_V), lambda p, g: (g, p)),
        out_shape=jax.ShapeDtypeStruct((ng * TQ, MLA_HEADS * MLA_V), BF16),
        compiler_params=_params(("parallel", "arbitrary")),
        name="mla_attn",
    )(qm, km, vm, km, vm)


_IDX_BITS = 13
_B_DIAG, _B_PREV, _B_META_FIRST, _B_META_META = 0, 1, 2, 3


def _dsa_kernel(qi_ref, wi_ref, qb_ref, ki_ref, kb_ref, vb_ref, kim_ref, kbm_ref, vbm_ref, bias_ref, o_ref,
                keys_scr, keym_scr, sel_scr, selm_scr, qs_scr, wb_scr, mstar_scr, *, nq, k_sel):
    is_meta, j = _tile_ids(nq)
    n_full = jnp.where(is_meta, 0, j)
    n_vis = jnp.where(is_meta, 0, j + 1)
    lane = lax.broadcasted_iota(jnp.int32, (TQ, LANES), 1)
    low = lane < DSA_HEAD_DIM
    meta_ok = lane < N_META
    diag_ok = _chunk_mask()

    for p in range(IDX_HEADS // 2):
        qp = qi_ref[:, p * LANES:(p + 1) * LANES]
        qs_scr[2 * p] = jnp.where(low, qp, jnp.zeros_like(qp))
        qs_scr[2 * p + 1] = jnp.where(low, jnp.zeros_like(qp), qp)
    wi = wi_ref[...] * (IDX_HEADS ** -0.5)
    for h in range(IDX_HEADS):
        wb_scr[h] = jnp.broadcast_to(wi[:, h:h + 1], (TQ, LANES))

    def sort_key(score):
        bits = pltpu.bitcast(score + 0.0, jnp.int32)
        return bits ^ ((bits >> 31) & 0x7FFFFFFF)

    def score_of(kt):
        n = kt.shape[0]
        score = jnp.zeros((TQ, n), F32)
        for h in range(IDX_HEADS):
            w = wb_scr[h]
            if n > LANES:
                w = jnp.concatenate([w] * (n // LANES), axis=1)
            score = score + w * jnp.maximum(_dot_t(qs_scr[h], kt), 0.0)
        return sort_key(score)

    keym_scr[...] = jnp.where(meta_ok, score_of(kim_ref[...]), INT_MIN)

    def score_tile(t, masked):
        key = score_of(ki_ref[pl.ds(pl.multiple_of(t * TQ, TQ), TQ), :])
        keys_scr[t] = jnp.where(diag_ok, key, INT_MIN) if masked else key

    def _loop(lo, hi, fn):
        lax.fori_loop(lo, hi, lambda t, c: (fn(t), c)[1], 0)

    _loop(0, n_full, lambda t: score_tile(t, False))
    _loop(n_full, n_vis, lambda t: score_tile(t, True))

    one = lambda m: jnp.where(m, 1, 0)
    lane2 = lax.broadcasted_iota(jnp.int32, (TQ, TQ), 1)

    def count(pred):
        def cbody(t, acc):
            pv = pred(keys_scr[t], LANES + t * TQ + lane2)
            return acc + pv[:, :LANES] + pv[:, LANES:]
        acc = lax.fori_loop(0, n_vis, cbody, pred(keym_scr[...], lane))
        return jnp.sum(acc, axis=1, keepdims=True)

    t0 = jnp.where(count(lambda b, _: one(b >= 0)) >= k_sel, 0, INT_MIN)

    def bs_body(i, t):
        cand = t | lax.shift_left(jnp.int32(1), 30 - i)
        return jnp.where(count(lambda b, _: one(b >= cand)) >= k_sel, cand, t)

    t = lax.fori_loop(0, 31, bs_body, t0)
    cnt_gt = count(lambda b, _: one(b > t))
    cnt_ge = count(lambda b, _: one(b >= t))
    need = k_sel - cnt_gt

    mstar_scr[...] = jnp.full((TQ, 1), 2 ** _IDX_BITS, jnp.int32)
    tied = jnp.max(jnp.where((cnt_ge > k_sel) & (t != INT_MIN), 1, 0))

    @pl.when(tied > 0)
    def _():
        def tie_body(i, r):
            cand = r | lax.shift_left(jnp.int32(1), _IDX_BITS - 1 - i)
            f = count(lambda b, idx: jnp.where(b == t, one(idx < cand), 0))
            return jnp.where(f < need, cand, r)
        r = lax.fori_loop(0, _IDX_BITS, tie_body, jnp.zeros((TQ, 1), jnp.int32))
        mstar_scr[...] = r + 1

    mstar = mstar_scr[...]

    def sel_mask(b, idx):
        tie_ok = jnp.where(b == t, one(idx < mstar), 0)
        sel = jnp.where(b == INT_MIN, 0, jnp.where(b > t, 1, tie_ok))
        return jnp.where(sel > 0, 0.0, NEG_INF)

    selm_scr[...] = sel_mask(keym_scr[...], lane)

    def sel_tile(t):
        sel_scr[t] = sel_mask(keys_scr[t], LANES + t * TQ + lane2)

    _loop(0, n_vis, sel_tile)

    c = DSA_HEAD_DIM ** -0.5 * LOG2E
    n_far = jnp.maximum(n_full - 1, 0)
    use_mb = jnp.where(is_meta | (j == 0), 1.0, 0.0)
    mb_idx = jnp.where(is_meta, _B_META_META, _B_META_FIRST)
    for p in range(DSA_HEADS // 2):
        qp = qb_ref[:, p * LANES:(p + 1) * LANES]
        qs = [jnp.where(low, qp, jnp.zeros_like(qp)), jnp.where(low, jnp.zeros_like(qp), qp)]
        cols = slice(p * LANES, (p + 1) * LANES)
        selm = selm_scr[...]
        state = []
        for hh in range(2):
            z = _dot_t(qs[hh], kbm_ref[:, cols]) * c + bias_ref[2 * p + hh, mb_idx][:, :LANES] * use_mb + selm
            state += list(_first(z, vbm_ref[:, cols]))

        def tile(t, st, near, qs=qs, cols=cols, p=p):
            ks = pl.multiple_of(t * TQ, TQ)
            k = kb_ref[pl.ds(ks, TQ), cols]
            v = vb_ref[pl.ds(ks, TQ), cols]
            sel = sel_scr[t]
            out = []
            for hh in range(2):
                z = _dot_t(qs[hh], k) * c + sel
                if near is not None:
                    z = z + bias_ref[2 * p + hh, near]
                out += list(_online(z, *st[3 * hh:3 * hh + 3], v))
            return tuple(out)

        st = lax.fori_loop(0, n_far, lambda t, s: tile(t, s, None), tuple(state))
        st = lax.fori_loop(n_far, n_full, lambda t, s: tile(t, s, _B_PREV), st)
        st = lax.fori_loop(n_full, n_vis, lambda t, s: tile(t, s, _B_DIAG), st)
        o_ref[:, cols] = jnp.where(low, st[2] / st[1], st[5] / st[4]).astype(BF16)


def _dsa_call(qi, wi, qb, ki, kb, vb, bias, B, S, k_sel):
    nq = S // TQ
    ng = B * nq + 1
    W = DSA_HEADS * DSA_HEAD_DIM
    bidx = lambda g: jnp.minimum(g // nq, B - 1)
    mrow = B * S // LANES
    qrow = lambda w: pl.BlockSpec((TQ, w), lambda _, g: (g, 0))
    kv = lambda w: pl.BlockSpec((S, w), lambda _, g: (bidx(g), 0))
    kvm = lambda w: pl.BlockSpec((LANES, w), lambda _, g: (mrow, 0))
    return pl.pallas_call(
        functools.partial(_dsa_kernel, nq=nq, k_sel=k_sel),
        grid=(1, ng),
        in_specs=[qrow(W), qrow(LANES), qrow(W), kv(LANES), kv(W), kv(W), kvm(LANES), kvm(W), kvm(W),
                  pl.BlockSpec(bias.shape, lambda _, g: (0, 0, 0, 0))],
        out_specs=qrow(W),
        out_shape=jax.ShapeDtypeStruct((ng * TQ, W), BF16),
        scratch_shapes=[
            pltpu.VMEM((nq, TQ, TQ), jnp.int32),
            pltpu.VMEM((TQ, LANES), jnp.int32),
            pltpu.VMEM((nq, TQ, TQ), F32),
            pltpu.VMEM((TQ, LANES), F32),
            pltpu.VMEM((IDX_HEADS, TQ, LANES), BF16),
            pltpu.VMEM((IDX_HEADS, TQ, LANES), F32),
            pltpu.VMEM((TQ, 1), jnp.int32),
        ],
        compiler_params=_params(("arbitrary", "arbitrary")),
        name="dsa",
    )(qi, wi, qb, ki, kb, vb, ki, kb, vb, bias)


def _t5_bucket(rel):
    nb = REL_BUCKETS // 2
    max_exact = nb // 2
    n = jnp.abs(rel)
    large = max_exact + (jnp.log(jnp.maximum(n, 1).astype(F32) / max_exact)
                         / math.log(REL_MAX_DIST / max_exact) * (nb - max_exact)).astype(jnp.int32)
    large = jnp.minimum(large, nb - 1)
    return jnp.where(rel > 0, nb, 0) + jnp.where(n < max_exact, n, large)


def _bias_kernel(bucket_ref, rb_ref, o_ref, *, far_bucket):
    h = pl.program_id(0)
    far = rb_ref[far_bucket, h]
    for tile in range(bucket_ref.shape[0]):
        bkt = bucket_ref[tile]
        acc = jnp.zeros(bkt.shape, F32)
        for b in range(REL_BUCKETS):
            acc = jnp.where(bkt == b, rb_ref[b, h], acc)
        o_ref[0, tile] = (acc - far) * LOG2E


def _bias_tiles(rel_bias):
    q = jnp.arange(TQ, dtype=jnp.int32)[:, None]
    k = jnp.arange(TQ, dtype=jnp.int32)[None, :]
    rels = jnp.stack([k - q, k - q - TQ, k - (q + N_META), k - q])
    far_bucket = REL_BUCKETS // 2 - 1
    return pl.pallas_call(
        functools.partial(_bias_kernel, far_bucket=far_bucket),
        grid=(DSA_HEADS,),
        in_specs=[pl.BlockSpec((4, TQ, TQ), lambda h: (0, 0, 0)),
                  pl.BlockSpec(memory_space=pltpu.SMEM)],
        out_specs=pl.BlockSpec((1, 4, TQ, TQ), lambda h: (h, 0, 0, 0)),
        out_shape=jax.ShapeDtypeStruct((DSA_HEADS, 4, TQ, TQ), F32),
        compiler_params=_params(("parallel",)),
        name="bias_tiles",
    )(_t5_bucket(rels), rel_bias)


def _merge_kernel(x_ref, meta_ref, oa_ref, ob_ref, ga_ref, gb_ref, wa_ref, wb_ref, wo_ref, gf_ref, wr_ref, br_ref,
                  h1_ref, xn_ref, gate_ref):
    is_meta = pl.program_id(0) == pl.num_programs(0) - 1
    h = jnp.where(is_meta, meta_ref[...], x_ref[...])
    y = (jax.nn.sigmoid(ga_ref[...]) * _dot(oa_ref[...], wa_ref[...])
         + jax.nn.sigmoid(gb_ref[...]) * _dot(ob_ref[...], wb_ref[...]))
    h1 = h + _dot(y.astype(BF16), wo_ref[...])
    h1_ref[...] = h1
    xn = _rms(h1, gf_ref[...])
    xn_ref[...] = xn.astype(BF16)
    logits = jnp.dot(xn, wr_ref[...], preferred_element_type=F32, precision=lax.Precision.HIGHEST) + br_ref[...]
    tm = logits.shape[0]
    lane = lax.broadcasted_iota(jnp.int32, (tm, LANES), 1)
    ninf = -jnp.inf
    gl = jnp.where((lane >= N_EXPERTS) & (lane < N_EXPERTS + N_GROUPS), logits, ninf)
    gmax = jnp.max(gl, axis=1, keepdims=True)
    gsel = jnp.min(jnp.where(gl == gmax, lane, LANES), axis=1, keepdims=True) - N_EXPERTS
    p_group = 1.0 / jnp.sum(jnp.exp(gl - gmax), axis=1, keepdims=True)
    lo = gsel * EXPERTS_PER_GROUP
    el = jnp.where((lane >= lo) & (lane < lo + EXPERTS_PER_GROUP), logits, ninf)
    m1 = jnp.max(el, axis=1, keepdims=True)
    i1 = jnp.min(jnp.where(el == m1, lane, LANES), axis=1, keepdims=True)
    el2 = jnp.where(lane == i1, ninf, el)
    m2 = jnp.max(el2, axis=1, keepdims=True)
    i2 = jnp.min(jnp.where(el2 == m2, lane, LANES), axis=1, keepdims=True)
    e2 = jnp.exp(m2 - m1)
    w1 = p_group / (1.0 + e2)
    w2 = p_group * e2 / (1.0 + e2)
    gate_ref[...] = jnp.where(lane == i1, w1, 0.0) + jnp.where(lane == i2, w2, 0.0)


def _merge_call(x2d, meta_tile, oa, ob, ga, gb, wa, wb, wo, gf, wr, br):
    BS, D = x2d.shape
    nt = BS // TQ + 1
    row = lambda w: pl.BlockSpec((TQ, w), lambda i: (i, 0))
    full = lambda a: pl.BlockSpec(a.shape, lambda i: (0, 0))
    return pl.pallas_call(
        _merge_kernel,
        grid=(nt,),
        in_specs=[pl.BlockSpec((TQ, D), lambda i: (jnp.minimum(i, nt - 2), 0)), full(meta_tile),
                  row(oa.shape[1]), row(ob.shape[1]), row(D), row(D),
                  full(wa), full(wb), full(wo), full(gf), full(wr), full(br)],
        out_specs=[row(D), row(D), row(LANES)],
        out_shape=[jax.ShapeDtypeStruct((nt * TQ, D), F32), jax.ShapeDtypeStruct((nt * TQ, D), BF16),
                   jax.ShapeDtypeStruct((nt * TQ, LANES), F32)],
        compiler_params=_params(("parallel",)),
        name="merge_route",
    )(x2d, meta_tile, oa, ob, ga, gb, wa, wb, wo, gf, wr, br)


def _moe_kernel(x_ref, gate_ref, h1_ref, wg_ref, wu_ref, wd_ref, gfin_ref, o_ref, acc_ref):
    e = pl.program_id(1)

    @pl.when(e == 0)
    def _():
        acc_ref[...] = jnp.zeros_like(acc_ref)

    x = x_ref[...]
    a = _dot(x, wg_ref[0].astype(BF16))
    u = _dot(x, wu_ref[0].astype(BF16))
    hmid = (a * jax.nn.sigmoid(a) * u).astype(BF16)
    y = _dot(hmid, wd_ref[0].astype(BF16))
    gates = gate_ref[...]
    lane = lax.broadcasted_iota(jnp.int32, gates.shape, 1)
    w = jnp.sum(jnp.where(lane == e, gates, 0.0), axis=1, keepdims=True)
    acc_ref[...] += w * y

    @pl.when(e == pl.num_programs(1) - 1)
    def _():
        o_ref[...] = _rms(h1_ref[...] + acc_ref[...], gfin_ref[...])


def _moe_call(xn, gates, h1, wg, wu, wd, gfin, n_out, tm):
    R, D = h1.shape
    E = wg.shape[0]
    row = lambda w: pl.BlockSpec((tm, w), lambda i, e: (i, 0))
    return pl.pallas_call(
        _moe_kernel,
        grid=(R // tm, E),
        in_specs=[row(D), row(LANES), row(D),
                  pl.BlockSpec((1, D, D_EXPERT), lambda i, e: (e, 0, 0)),
                  pl.BlockSpec((1, D, D_EXPERT), lambda i, e: (e, 0, 0)),
                  pl.BlockSpec((1, D_EXPERT, D), lambda i, e: (e, 0, 0)),
                  pl.BlockSpec(gfin.shape, lambda i, e: (0, 0))],
        out_specs=row(D),
        out_shape=jax.ShapeDtypeStruct((n_out, D), F32),
        scratch_shapes=[pltpu.VMEM((tm, D), F32)],
        compiler_params=_params(("parallel", "arbitrary")),
        name="moe",
    )(xn, gates, h1, wg, wu, wd, gfin)


def _moe_tile(n_tiles, cap):
    return max(d for d in range(1, cap + 1) if n_tiles % d == 0)


def kernel(x, meta_tokens, norm_mix_g, w_in, mla_cq_norm_g, mla_ckv_norm_g, w_mla_uq, w_mla_ukv,
           w_branch_a, w_branch_b, w_out, rel_bias, norm_ffn_g, w_router_group, b_router_group,
           w_router_expert, b_router_expert, w_exp_gate, w_exp_up, w_exp_down, norm_final_g):
    B, S, D = x.shape
    assert S % TQ == 0 and norm_mix_g.shape[0] == 1
    k_sel = min(K_SEL_MAX, S // 4)
    x2d = x.reshape(B * S, D)
    meta_tile = jnp.concatenate([meta_tokens.astype(x.dtype), jnp.zeros((TQ - N_META, D), x.dtype)], axis=0)

    pos = np.concatenate([N_META + np.arange(S), np.minimum(np.arange(TQ), N_META)]).astype(np.int32)
    ct, st = _rope_tables(jnp.asarray(pos))

    qm, km, vm, qb, kb, vb, qi, ki, wi, ga, gb = _proj_call(
        x2d, meta_tile, norm_mix_g.reshape(1, D), _pack_w1(w_in[0]), mla_cq_norm_g.reshape(1, -1),
        mla_ckv_norm_g.reshape(1, -1), _pack_wuq(w_mla_uq[0]), _pack_wukv(w_mla_ukv[0]), ct, st, S)

    o_a = _mla_call(qm, km, vm, B, S)
    o_b = _dsa_call(qi, wi, qb, ki, kb, vb, _bias_tiles(rel_bias), B, S, k_sel)

    w_r = jnp.concatenate([w_router_expert[0], w_router_group[0],
                           jnp.zeros((D, LANES - N_EXPERTS - N_GROUPS), F32)], axis=1)
    b_r = jnp.concatenate([b_router_expert[0], b_router_group[0],
                           jnp.zeros((LANES - N_EXPERTS - N_GROUPS,), F32)]).reshape(1, LANES)
    h1, xn, gates = _merge_call(x2d, meta_tile, o_a, o_b, ga, gb, w_branch_a[0].astype(BF16),
                                w_branch_b[0].astype(BF16), w_out[0].astype(BF16),
                                norm_ffn_g.reshape(1, D), w_r, b_r)

    n_tiles = B * S // TQ + 1
    out = _moe_call(xn, gates, h1, w_exp_gate[0], w_exp_up[0], w_exp_down[0],
                    norm_final_g.reshape(1, D), B * S, _moe_tile(n_tiles, 6) * TQ)
    return out.reshape(B, S, D)
```

```python
import functools
import math

import numpy as np
import jax
import jax.numpy as jnp
from jax import lax
from jax.experimental import pallas as pl
from jax.experimental.pallas import tpu as pltpu

CHUNK = 64
N_META = 16
NEG_INF = -1e30
RMS_EPS = 1e-6
ROPE_BASE = 10000.0
MLA_HEADS = 8
MLA_Q_LORA = 256
MLA_KV_LORA = 128
MLA_NOPE = 64
MLA_ROPE = 32
MLA_V = 64
DSA_HEADS = 8
DSA_HEAD_DIM = 64
IDX_HEADS = 8
IDX_DIM = 64
K_SEL_MAX = 256
REL_BUCKETS = 32
REL_MAX_DIST = 128
N_GROUPS = 4
EXPERTS_PER_GROUP = 8
N_EXPERTS = N_GROUPS * EXPERTS_PER_GROUP
D_EXPERT = 256

LANES = 128
TQ = 256
HEAD_PAD = 128
VMEM_LIMIT = 56 * 1024 * 1024
INT_MIN = -2 ** 31
LOG2E = math.log2(math.e)

F32 = jnp.float32
BF16 = jnp.bfloat16


def _params(sem):
    return pltpu.CompilerParams(dimension_semantics=sem, vmem_limit_bytes=VMEM_LIMIT)


def _rms(x, g):
    return x * lax.rsqrt(jnp.mean(x * x, axis=-1, keepdims=True) + RMS_EPS) * g


def _dot(a, b):
    return jnp.dot(a, b, preferred_element_type=F32)


def _dot_t(a, b):
    return lax.dot_general(a, b, (((1,), (1,)), ((), ())), preferred_element_type=F32)


_C_CQ = (0, 256)
_C_CKV = (256, 384)
_C_KRX = (384, 512)
_C_KRY = (512, 640)
_C_QB = (640, 1152)
_C_KB = (1152, 1664)
_C_VB = (1664, 2176)
_C_QI = (2176, 2688)
_C_KI = (2688, 2816)
_C_WI = (2816, 2944)
_C_GA = (2944, 3968)
_C_GB = (3968, 4992)
_W1_COLS = 4992


def _proj_kernel(x_ref, meta_ref, g_ref, w1_ref, gcq_ref, gckv_ref, wuq_ref, wukv_ref, ct_ref, st_ref,
                 qm_ref, km_ref, vm_ref, qb_ref, kb_ref, vb_ref, qi_ref, ki_ref, wi_ref,
                 ga_ref, gb_ref):
    is_meta = pl.program_id(0) == pl.num_programs(0) - 1
    h = jnp.where(is_meta, meta_ref[...], x_ref[...])
    xb = _rms(h, g_ref[...]).astype(BF16)

    def seg(c):
        return _dot(xb, w1_ref[:, c[0]:c[1]])

    ct = ct_ref[...]
    st = st_ref[...]
    nq = _rms(seg(_C_CQ), gcq_ref[...]).astype(BF16)
    qa = _dot(nq, wuq_ref[...])
    half = MLA_HEADS * HEAD_PAD
    for hd in range(MLA_HEADS):
        lo, hi = hd * HEAD_PAD, (hd + 1) * HEAD_PAD
        qm_ref[:, lo:hi] = (qa[:, lo:hi] * ct + qa[:, half + lo:half + hi] * st).astype(BF16)
    nkv = _rms(seg(_C_CKV), gckv_ref[...]).astype(BF16)
    kva = _dot(nkv, wukv_ref[...])
    kr = seg(_C_KRX) * ct + seg(_C_KRY) * st
    for hd in range(MLA_HEADS):
        lo, hi = hd * HEAD_PAD, (hd + 1) * HEAD_PAD
        km_ref[:, lo:hi] = (kva[:, lo:hi] + kr).astype(BF16)
    vm_ref[0] = kva[:, half:].T.astype(BF16)
    qb_ref[...] = seg(_C_QB).astype(BF16)
    kb_ref[...] = seg(_C_KB).astype(BF16)
    vb_ref[0] = seg(_C_VB).T.astype(BF16)
    qi_ref[...] = seg(_C_QI).astype(BF16)
    ki_ref[...] = seg(_C_KI).astype(BF16)
    wi_ref[...] = seg(_C_WI)
    ga_ref[...] = seg(_C_GA)
    gb_ref[...] = seg(_C_GB)


def _pack_w1(w_in):
    D = w_in.shape[0]
    offs = np.cumsum([0, MLA_Q_LORA, MLA_KV_LORA, MLA_ROPE, 512, 512, 512, 512, IDX_DIM, IDX_HEADS, D, D])
    c = lambda i: w_in[:, offs[i]:offs[i + 1]]
    z = lambda n: jnp.zeros((D, n), w_in.dtype)
    k_r = c(2)
    hr = MLA_ROPE // 2
    k_r_sw = jnp.concatenate([k_r[:, hr:], k_r[:, :hr]], axis=1)
    cols = [
        c(0), c(1),
        z(MLA_NOPE), k_r, z(LANES - MLA_NOPE - MLA_ROPE),
        z(MLA_NOPE), k_r_sw, z(LANES - MLA_NOPE - MLA_ROPE),
        c(3), c(4), c(5),
        c(6) * (IDX_DIM ** -0.5),
        c(7), c(7),
        c(8), z(LANES - IDX_HEADS),
        c(9), c(10),
    ]
    w1 = jnp.concatenate(cols, axis=1)
    assert w1.shape[1] == _W1_COLS
    return w1.astype(BF16)


def _pack_wuq(w_uq):
    R = w_uq.shape[0]
    w = w_uq.reshape(R, MLA_HEADS, MLA_NOPE + MLA_ROPE)
    nope, rope = w[..., :MLA_NOPE], w[..., MLA_NOPE:]
    hr = MLA_ROPE // 2
    rope_sw = jnp.concatenate([rope[..., hr:], rope[..., :hr]], axis=-1)
    zt = jnp.zeros((R, MLA_HEADS, HEAD_PAD - MLA_NOPE - MLA_ROPE), w.dtype)
    a = jnp.concatenate([nope, rope, zt], axis=-1).reshape(R, MLA_HEADS * HEAD_PAD)
    b = jnp.concatenate([jnp.zeros_like(nope), rope_sw, zt], axis=-1).reshape(R, MLA_HEADS * HEAD_PAD)
    return jnp.concatenate([a, b], axis=1).astype(BF16)


def _pack_wukv(w_ukv):
    R = w_ukv.shape[0]
    w = w_ukv.reshape(R, MLA_HEADS, MLA_NOPE + MLA_V)
    k = jnp.concatenate([w[..., :MLA_NOPE], jnp.zeros((R, MLA_HEADS, HEAD_PAD - MLA_NOPE), w.dtype)], axis=-1)
    v = w[..., MLA_NOPE:]
    return jnp.concatenate([k.reshape(R, -1), v.reshape(R, -1)], axis=1).astype(BF16)


def _rope_tables(pos):
    half = MLA_ROPE // 2
    inv = ROPE_BASE ** (-jnp.arange(half, dtype=F32) / half)
    ang = pos.astype(F32)[:, None] * inv[None, :]
    cos, sin = jnp.cos(ang), jnp.sin(ang)
    n = pos.shape[0]
    tail = jnp.zeros((n, HEAD_PAD - MLA_NOPE - MLA_ROPE), F32)
    ct = jnp.concatenate([jnp.ones((n, MLA_NOPE), F32), cos, cos, tail], axis=1)
    st = jnp.concatenate([jnp.zeros((n, MLA_NOPE), F32), -sin, sin, tail], axis=1)
    return ct, st


def _proj_call(x2d, meta_tile, g, w1, gcq, gckv, wuq, wukv, ct, st, S):
    BS, D = x2d.shape
    nt = BS // TQ + 1
    per_b = S // TQ
    row = lambda w: pl.BlockSpec((TQ, w), lambda i: (i, 0))
    full = lambda a: pl.BlockSpec(a.shape, lambda i: (0, 0))
    tab = pl.BlockSpec((TQ, HEAD_PAD), lambda i: (jnp.where(i == nt - 1, per_b, i % per_b), 0))
    widths = [(MLA_HEADS * HEAD_PAD, BF16), (MLA_HEADS * HEAD_PAD, BF16), (None, BF16),
              (512, BF16), (512, BF16), (None, BF16), (512, BF16), (LANES, BF16), (LANES, F32),
              (D, F32), (D, F32)]
    vt_rows = MLA_HEADS * MLA_V
    vt_spec = pl.BlockSpec((1, vt_rows, TQ), lambda i: (i, 0, 0))
    vt_shape = jax.ShapeDtypeStruct((nt, vt_rows, TQ), BF16)
    return pl.pallas_call(
        _proj_kernel,
        grid=(nt,),
        in_specs=[pl.BlockSpec((TQ, D), lambda i: (jnp.minimum(i, nt - 2), 0)), full(meta_tile),
                  full(g), full(w1), full(gcq), full(gckv), full(wuq), full(wukv), tab, tab],
        out_specs=[vt_spec if w is None else row(w) for w, _ in widths],
        out_shape=[vt_shape if w is None else jax.ShapeDtypeStruct((nt * TQ, w), dt) for w, dt in widths],
        compiler_params=_params(("parallel",)),
        name="proj",
    )(x2d, meta_tile, g, w1, gcq, gckv, wuq, wukv, ct, st)


def _colmax(z):
    return jnp.max(z, axis=0, keepdims=True)


def _pv(vt, z, m, l, acc):
    p = jnp.exp2(z - m)
    return l + jnp.sum(p, axis=0, keepdims=True), acc + _dot(vt, p.astype(BF16))


def _tile_ids(nq):
    g = pl.program_id(1)
    is_meta = g == pl.num_programs(1) - 1
    return is_meta, jnp.where(is_meta, 0, g % nq)


def _chunk_mask():
    key = lax.broadcasted_iota(jnp.int32, (TQ, TQ), 0)
    qry = lax.broadcasted_iota(jnp.int32, (TQ, TQ), 1)
    shift = CHUNK.bit_length() - 1
    return (key >> shift) <= (qry >> shift)


def _fori_pairs(lo, hi, pair_body, body, carry):
    pairs = (hi - lo) // 2
    carry = lax.fori_loop(0, pairs, lambda u, c: pair_body(lo + 2 * u, c), carry)
    return lax.fori_loop(lo + 2 * pairs, hi, body, carry)


def _fori2(lo, hi, body, carry):
    return _fori_pairs(lo, hi, lambda t, c: body(t + 1, body(t, c)), body, carry)


def _loop(lo, hi, fn):
    _fori2(lo, hi, lambda t, c: (fn(t), c)[1], 0)


def _mla_kernel(q_ref, k_ref, vt_ref, km_ref, vtm_ref, o_ref, s_scr, sm_scr, *, nq):
    is_meta, j = _tile_ids(nq)
    n_full = jnp.where(is_meta, 0, j)
    n_vis = jnp.where(is_meta, 0, j + 1)
    c = (MLA_NOPE + MLA_ROPE) ** -0.5 * LOG2E
    qs = [q_ref[:, hh * HEAD_PAD:(hh + 1) * HEAD_PAD] for hh in range(2)]
    hcols = [slice(hh * HEAD_PAD, (hh + 1) * HEAD_PAD) for hh in range(2)]
    vrows = [slice(hh * MLA_V, (hh + 1) * MLA_V) for hh in range(2)]
    meta_ok = lax.broadcasted_iota(jnp.int32, (LANES, TQ), 0) < N_META
    diag_ok = _chunk_mask()

    ms = []
    for hh in range(2):
        z = jnp.where(meta_ok, _dot_t(km_ref[:, hcols[hh]], qs[hh]) * c, NEG_INF)
        sm_scr[hh] = z
        ms.append(_colmax(z))

    def score_tiles(t, ms, n, masked):
        ks = pl.multiple_of(t * TQ, TQ)
        out = []
        for hh in range(2):
            z = _dot_t(k_ref[pl.ds(ks, n * TQ), hcols[hh]], qs[hh]) * c
            if masked:
                z = jnp.where(diag_ok, z, NEG_INF)
            for i in range(n):
                s_scr[hh, t + i] = z[i * TQ:(i + 1) * TQ]
            out.append(jnp.maximum(ms[hh], _colmax(z)))
        return tuple(out)

    ms = _fori_pairs(0, n_full, lambda t, m: score_tiles(t, m, 2, False),
                     lambda t, m: score_tiles(t, m, 1, False), tuple(ms))
    ms = lax.fori_loop(n_full, n_vis, lambda t, m: score_tiles(t, m, 1, True), ms)

    st = []
    for hh in range(2):
        st += list(_pv(vtm_ref[0, vrows[hh], 0:LANES], sm_scr[hh], ms[hh],
                       jnp.zeros((1, TQ), F32), jnp.zeros((MLA_V, TQ), F32)))

    def pv_tile(t, st):
        out = []
        for hh in range(2):
            out += list(_pv(vt_ref[t, vrows[hh], :], s_scr[hh, t], ms[hh], st[2 * hh], st[2 * hh + 1]))
        return tuple(out)

    st = _fori2(0, n_vis, pv_tile, tuple(st))
    ot = jnp.concatenate([st[1] / st[0], st[3] / st[2]], axis=0)
    o_ref[...] = ot.T.astype(BF16)


def _mla_call(qm, km, vmt, B, S):
    nq = S // TQ
    ng = B * nq + 1
    pairs = MLA_HEADS // 2
    bidx = lambda g: jnp.minimum(g // nq, B - 1)
    mrow = B * S // LANES
    return pl.pallas_call(
        functools.partial(_mla_kernel, nq=nq),
        grid=(pairs, ng),
        in_specs=[
            pl.BlockSpec((TQ, 2 * HEAD_PAD), lambda p, g: (g, p)),
            pl.BlockSpec((S, 2 * HEAD_PAD), lambda p, g: (bidx(g), p)),
            pl.BlockSpec((nq, 2 * MLA_V, TQ), lambda p, g: (bidx(g), p, 0)),
            pl.BlockSpec((LANES, 2 * HEAD_PAD), lambda p, g: (mrow, p)),
            pl.BlockSpec((1, 2 * MLA_V, TQ), lambda p, g: (ng - 1, p, 0)),
        ],
        out_specs=pl.BlockSpec((TQ, 2 * MLA_V), lambda p, g: (g, p)),
        out_shape=jax.ShapeDtypeStruct((ng * TQ, MLA_HEADS * MLA_V), BF16),
        scratch_shapes=[pltpu.VMEM((2, nq, TQ, TQ), F32),
                        pltpu.VMEM((2, LANES, TQ), F32)],
        compiler_params=_params(("parallel", "arbitrary")),
        name="mla_attn",
    )(qm, km, vmt, km, vmt)


_IDX_BITS = 13
_B_DIAG, _B_PREV, _B_META_FIRST, _B_META_META = 0, 1, 2, 3


def _dsa_kernel(qi_ref, wi_ref, qb_ref, ki_ref, kb_ref, vbt_ref, kim_ref, kbm_ref, vbtm_ref, bias_ref, o_ref,
                keys_scr, keym_scr, sel_scr, selm_scr, s_scr, sm_scr, qs_scr, mstar_scr, *, nq, k_sel):
    is_meta, j = _tile_ids(nq)
    n_full = jnp.where(is_meta, 0, j)
    n_vis = jnp.where(is_meta, 0, j + 1)
    lane = lax.broadcasted_iota(jnp.int32, (TQ, LANES), 1)
    low = lane < DSA_HEAD_DIM
    krow_m = lax.broadcasted_iota(jnp.int32, (LANES, TQ), 0)
    krow = lax.broadcasted_iota(jnp.int32, (TQ, TQ), 0)
    meta_ok = krow_m < N_META
    diag_ok = _chunk_mask()

    for p in range(IDX_HEADS // 2):
        qp = qi_ref[:, p * LANES:(p + 1) * LANES]
        qs_scr[2 * p] = jnp.where(low, qp, jnp.zeros_like(qp))
        qs_scr[2 * p + 1] = jnp.where(low, jnp.zeros_like(qp), qp)
    wt = (wi_ref[...] * (IDX_HEADS ** -0.5)).T

    def sort_key(score):
        bits = pltpu.bitcast(score + 0.0, jnp.int32)
        return bits ^ ((bits >> 31) & 0x7FFFFFFF)

    def score_of(kt):
        score = jnp.zeros((kt.shape[0], TQ), F32)
        for h in range(IDX_HEADS):
            score = score + wt[h:h + 1, :] * jnp.maximum(_dot_t(kt, qs_scr[h]), 0.0)
        return sort_key(score)

    keym_scr[...] = jnp.where(meta_ok, score_of(kim_ref[...]), INT_MIN)

    def score_tiles(t, n, masked):
        key = score_of(ki_ref[pl.ds(pl.multiple_of(t * TQ, TQ), n * TQ), :])
        if masked:
            key = jnp.where(diag_ok, key, INT_MIN)
        for i in range(n):
            keys_scr[t + i] = key[i * TQ:(i + 1) * TQ]

    nop = lambda fn: (lambda t, c: (fn(t), c)[1])
    _fori_pairs(0, n_full, nop(lambda t: score_tiles(t, 2, False)), nop(lambda t: score_tiles(t, 1, False)), 0)
    lax.fori_loop(n_full, n_vis, nop(lambda t: score_tiles(t, 1, True)), 0)

    one = lambda m: jnp.where(m, 1, 0)
    fold = lambda a: jnp.sum(a.reshape(-1, 8, TQ), axis=0)

    def count(pred):
        def cbody(t, acc):
            return acc + fold(pred(keys_scr[t], LANES + t * TQ + krow))
        acc = _fori2(0, n_vis, cbody, fold(pred(keym_scr[...], krow_m)))
        return jnp.sum(acc, axis=0, keepdims=True)

    t0 = jnp.where(count(lambda b, _: one(b >= 0)) >= k_sel, 0, INT_MIN)

    def bs_body(i, t):
        cand = t | lax.shift_left(jnp.int32(1), 30 - i)
        return jnp.where(count(lambda b, _: one(b >= cand)) >= k_sel, cand, t)

    t = lax.fori_loop(0, 31, bs_body, t0)
    cnt_gt = count(lambda b, _: one(b > t))
    cnt_ge = count(lambda b, _: one(b >= t))
    need = k_sel - cnt_gt

    mstar_scr[...] = jnp.full((1, TQ), 2 ** _IDX_BITS, jnp.int32)
    tied = jnp.max(jnp.where((cnt_ge > k_sel) & (t != INT_MIN), 1, 0))

    @pl.when(tied > 0)
    def _():
        def tie_body(i, r):
            cand = r | lax.shift_left(jnp.int32(1), _IDX_BITS - 1 - i)
            f = count(lambda b, idx: jnp.where(b == t, one(idx < cand), 0))
            return jnp.where(f < need, cand, r)
        r = lax.fori_loop(0, _IDX_BITS, tie_body, jnp.zeros((1, TQ), jnp.int32))
        mstar_scr[...] = r + 1

    mstar = mstar_scr[...]

    def sel_mask(b, idx):
        tie_ok = jnp.where(b == t, one(idx < mstar), 0)
        sel = jnp.where(b == INT_MIN, 0, jnp.where(b > t, 1, tie_ok))
        return jnp.where(sel > 0, 0.0, NEG_INF)

    selm_scr[...] = sel_mask(keym_scr[...], krow_m)
    _loop(0, n_vis, lambda t: sel_scr.__setitem__(t, sel_mask(keys_scr[t], LANES + t * TQ + krow)))

    c = DSA_HEAD_DIM ** -0.5 * LOG2E
    n_far = jnp.maximum(n_full - 1, 0)
    use_mb = jnp.where(is_meta | (j == 0), 1.0, 0.0)
    mb_idx = jnp.where(is_meta, _B_META_META, _B_META_FIRST)
    for p in range(DSA_HEADS // 2):
        qp = qb_ref[:, p * LANES:(p + 1) * LANES]
        qs = [jnp.where(low, qp, jnp.zeros_like(qp)), jnp.where(low, jnp.zeros_like(qp), qp)]
        cols = slice(p * LANES, (p + 1) * LANES)
        vrows = [slice((2 * p + hh) * DSA_HEAD_DIM, (2 * p + hh + 1) * DSA_HEAD_DIM) for hh in range(2)]

        ms = []
        for hh in range(2):
            z = (_dot_t(kbm_ref[:, cols], qs[hh]) * c + bias_ref[2 * p + hh, mb_idx][:LANES, :] * use_mb
                 + selm_scr[...])
            sm_scr[hh] = z
            ms.append(_colmax(z))

        def score_tiles(t, ms, n, near, qs=qs, cols=cols, p=p):
            k = kb_ref[pl.ds(pl.multiple_of(t * TQ, TQ), n * TQ), cols]
            out = []
            for hh in range(2):
                zz = _dot_t(k, qs[hh]) * c
                m = ms[hh]
                for i in range(n):
                    z = zz[i * TQ:(i + 1) * TQ] + sel_scr[t + i]
                    if near is not None:
                        z = z + bias_ref[2 * p + hh, near]
                    s_scr[hh, t + i] = z
                    m = jnp.maximum(m, _colmax(z))
                out.append(m)
            return tuple(out)

        ms = _fori_pairs(0, n_far, lambda t, m: score_tiles(t, m, 2, None),
                         lambda t, m: score_tiles(t, m, 1, None), tuple(ms))
        ms = lax.fori_loop(n_far, n_full, lambda t, m: score_tiles(t, m, 1, _B_PREV), ms)
        ms = lax.fori_loop(n_full, n_vis, lambda t, m: score_tiles(t, m, 1, _B_DIAG), ms)

        st = []
        for hh in range(2):
            st += list(_pv(vbtm_ref[0, vrows[hh], 0:LANES], sm_scr[hh], ms[hh],
                           jnp.zeros((1, TQ), F32), jnp.zeros((DSA_HEAD_DIM, TQ), F32)))

        def pv_tile(t, st, ms=ms, vrows=vrows):
            out = []
            for hh in range(2):
                out += list(_pv(vbt_ref[t, vrows[hh], :], s_scr[hh, t], ms[hh], st[2 * hh], st[2 * hh + 1]))
            return tuple(out)

        st = _fori2(0, n_vis, pv_tile, tuple(st))
        ot = jnp.concatenate([st[1] / st[0], st[3] / st[2]], axis=0)
        o_ref[:, cols] = ot.T.astype(BF16)


def _dsa_call(qi, wi, qb, ki, kb, vbt, bias, B, S, k_sel):
    nq = S // TQ
    ng = B * nq + 1
    W = DSA_HEADS * DSA_HEAD_DIM
    bidx = lambda g: jnp.minimum(g // nq, B - 1)
    mrow = B * S // LANES
    qrow = lambda w: pl.BlockSpec((TQ, w), lambda _, g: (g, 0))
    kv = lambda w: pl.BlockSpec((S, w), lambda _, g: (bidx(g), 0))
    kvm = lambda w: pl.BlockSpec((LANES, w), lambda _, g: (mrow, 0))
    return pl.pallas_call(
        functools.partial(_dsa_kernel, nq=nq, k_sel=k_sel),
        grid=(1, ng),
        in_specs=[qrow(W), qrow(LANES), qrow(W), kv(LANES), kv(W),
                  pl.BlockSpec((nq, W, TQ), lambda _, g: (bidx(g), 0, 0)),
                  kvm(LANES), kvm(W),
                  pl.BlockSpec((1, W, TQ), lambda _, g: (ng - 1, 0, 0)),
                  pl.BlockSpec(bias.shape, lambda _, g: (0, 0, 0, 0))],
        out_specs=qrow(W),
        out_shape=jax.ShapeDtypeStruct((ng * TQ, W), BF16),
        scratch_shapes=[
            pltpu.VMEM((nq, TQ, TQ), jnp.int32),
            pltpu.VMEM((LANES, TQ), jnp.int32),
            pltpu.VMEM((nq, TQ, TQ), F32),
            pltpu.VMEM((LANES, TQ), F32),
            pltpu.VMEM((2, nq, TQ, TQ), F32),
            pltpu.VMEM((2, LANES, TQ), F32),
            pltpu.VMEM((IDX_HEADS, TQ, LANES), BF16),
            pltpu.VMEM((1, TQ), jnp.int32),
        ],
        compiler_params=_params(("arbitrary", "arbitrary")),
        name="dsa",
    )(qi, wi, qb, ki, kb, vbt, ki, kb, vbt, bias)


def _t5_bucket(rel):
    nb = REL_BUCKETS // 2
    max_exact = nb // 2
    n = jnp.abs(rel)
    large = max_exact + (jnp.log(jnp.maximum(n, 1).astype(F32) / max_exact)
                         / math.log(REL_MAX_DIST / max_exact) * (nb - max_exact)).astype(jnp.int32)
    large = jnp.minimum(large, nb - 1)
    return jnp.where(rel > 0, nb, 0) + jnp.where(n < max_exact, n, large)


def _bias_kernel(bucket_ref, rb_ref, o_ref, *, far_bucket):
    h = pl.program_id(0)
    far = rb_ref[far_bucket, h]
    for tile in range(bucket_ref.shape[0]):
        bkt = bucket_ref[tile]
        acc = jnp.zeros(bkt.shape, F32)
        for b in range(REL_BUCKETS):
            acc = jnp.where(bkt == b, rb_ref[b, h], acc)
        o_ref[0, tile] = (acc - far) * LOG2E


def _bias_tiles(rel_bias):
    k = jnp.arange(TQ, dtype=jnp.int32)[:, None]
    q = jnp.arange(TQ, dtype=jnp.int32)[None, :]
    rels = jnp.stack([k - q, k - q - TQ, k - (q + N_META), k - q])
    far_bucket = REL_BUCKETS // 2 - 1
    return pl.pallas_call(
        functools.partial(_bias_kernel, far_bucket=far_bucket),
        grid=(DSA_HEADS,),
        in_specs=[pl.BlockSpec((4, TQ, TQ), lambda h: (0, 0, 0)),
                  pl.BlockSpec(memory_space=pltpu.SMEM)],
        out_specs=pl.BlockSpec((1, 4, TQ, TQ), lambda h: (h, 0, 0, 0)),
        out_shape=jax.ShapeDtypeStruct((DSA_HEADS, 4, TQ, TQ), F32),
        compiler_params=_params(("parallel",)),
        name="bias_tiles",
    )(_t5_bucket(rels), rel_bias)


def _merge_kernel(x_ref, meta_ref, oa_ref, ob_ref, ga_ref, gb_ref, wa_ref, wb_ref, wo_ref, gf_ref, wr_ref, br_ref,
                  h1_ref, xn_ref, gate_ref):
    is_meta = pl.program_id(0) == pl.num_programs(0) - 1
    h = jnp.where(is_meta, meta_ref[...], x_ref[...])
    y = (jax.nn.sigmoid(ga_ref[...]) * _dot(oa_ref[...], wa_ref[...])
         + jax.nn.sigmoid(gb_ref[...]) * _dot(ob_ref[...], wb_ref[...]))
    h1 = h + _dot(y.astype(BF16), wo_ref[...])
    h1_ref[...] = h1
    xn = _rms(h1, gf_ref[...])
    xn_ref[...] = xn.astype(BF16)
    logits = jnp.dot(xn, wr_ref[...], preferred_element_type=F32, precision=lax.Precision.HIGHEST) + br_ref[...]
    tm = logits.shape[0]
    lane = lax.broadcasted_iota(jnp.int32, (tm, LANES), 1)
    ninf = -jnp.inf
    gl = jnp.where((lane >= N_EXPERTS) & (lane < N_EXPERTS + N_GROUPS), logits, ninf)
    gmax = jnp.max(gl, axis=1, keepdims=True)
    gsel = jnp.min(jnp.where(gl == gmax, lane, LANES), axis=1, keepdims=True) - N_EXPERTS
    p_group = 1.0 / jnp.sum(jnp.exp(gl - gmax), axis=1, keepdims=True)
    lo = gsel * EXPERTS_PER_GROUP
    el = jnp.where((lane >= lo) & (lane < lo + EXPERTS_PER_GROUP), logits, ninf)
    m1 = jnp.max(el, axis=1, keepdims=True)
    i1 = jnp.min(jnp.where(el == m1, lane, LANES), axis=1, keepdims=True)
    el2 = jnp.where(lane == i1, ninf, el)
    m2 = jnp.max(el2, axis=1, keepdims=True)
    i2 = jnp.min(jnp.where(el2 == m2, lane, LANES), axis=1, keepdims=True)
    e2 = jnp.exp(m2 - m1)
    w1 = p_group / (1.0 + e2)
    w2 = p_group * e2 / (1.0 + e2)
    gate_ref[...] = jnp.where(lane == i1, w1, 0.0) + jnp.where(lane == i2, w2, 0.0)


def _merge_call(x2d, meta_tile, oa, ob, ga, gb, wa, wb, wo, gf, wr, br):
    BS, D = x2d.shape
    nt = BS // TQ + 1
    row = lambda w: pl.BlockSpec((TQ, w), lambda i: (i, 0))
    full = lambda a: pl.BlockSpec(a.shape, lambda i: (0, 0))
    return pl.pallas_call(
        _merge_kernel,
        grid=(nt,),
        in_specs=[pl.BlockSpec((TQ, D), lambda i: (jnp.minimum(i, nt - 2), 0)), full(meta_tile),
                  row(oa.shape[1]), row(ob.shape[1]), row(D), row(D),
                  full(wa), full(wb), full(wo), full(gf), full(wr), full(br)],
        out_specs=[row(D), row(D), row(LANES)],
        out_shape=[jax.ShapeDtypeStruct((nt * TQ, D), F32), jax.ShapeDtypeStruct((nt * TQ, D), BF16),
                   jax.ShapeDtypeStruct((nt * TQ, LANES), F32)],
        compiler_params=_params(("parallel",)),
        name="merge_route",
    )(x2d, meta_tile, oa, ob, ga, gb, wa, wb, wo, gf, wr, br)


def _moe_kernel(x_ref, gate_ref, h1_ref, wg_ref, wu_ref, wd_ref, gfin_ref, o_ref, acc_ref):
    e = pl.program_id(1)

    @pl.when(e == 0)
    def _():
        acc_ref[...] = jnp.zeros_like(acc_ref)

    x = x_ref[...]
    a = _dot(x, wg_ref[0].astype(BF16))
    u = _dot(x, wu_ref[0].astype(BF16))
    hmid = (a * jax.nn.sigmoid(a) * u).astype(BF16)
    y = _dot(hmid, wd_ref[0].astype(BF16))
    gates = gate_ref[...]
    lane = lax.broadcasted_iota(jnp.int32, gates.shape, 1)
    w = jnp.sum(jnp.where(lane == e, gates, 0.0), axis=1, keepdims=True)
    acc_ref[...] += w * y

    @pl.when(e == pl.num_programs(1) - 1)
    def _():
        o_ref[...] = _rms(h1_ref[...] + acc_ref[...], gfin_ref[...])


def _moe_call(xn, gates, h1, wg, wu, wd, gfin, n_out, tm):
    R, D = h1.shape
    E = wg.shape[0]
    row = lambda w: pl.BlockSpec((tm, w), lambda i, e: (i, 0))
    return pl.pallas_call(
        _moe_kernel,
        grid=(R // tm, E),
        in_specs=[row(D), row(LANES), row(D),
                  pl.BlockSpec((1, D, D_EXPERT), lambda i, e: (e, 0, 0)),
                  pl.BlockSpec((1, D, D_EXPERT), lambda i, e: (e, 0, 0)),
                  pl.BlockSpec((1, D_EXPERT, D), lambda i, e: (e, 0, 0)),
                  pl.BlockSpec(gfin.shape, lambda i, e: (0, 0))],
        out_specs=row(D),
        out_shape=jax.ShapeDtypeStruct((n_out, D), F32),
        scratch_shapes=[pltpu.VMEM((tm, D), F32)],
        compiler_params=_params(("parallel", "arbitrary")),
        name="moe",
    )(xn, gates, h1, wg, wu, wd, gfin)


def _moe_tile(n_tiles, cap):
    return max(d for d in range(1, cap + 1) if n_tiles % d == 0)


def kernel(x, meta_tokens, norm_mix_g, w_in, mla_cq_norm_g, mla_ckv_norm_g, w_mla_uq, w_mla_ukv,
           w_branch_a, w_branch_b, w_out, rel_bias, norm_ffn_g, w_router_group, b_router_group,
           w_router_expert, b_router_expert, w_exp_gate, w_exp_up, w_exp_down, norm_final_g):
    B, S, D = x.shape
    assert S % TQ == 0 and norm_mix_g.shape[0] == 1
    k_sel = min(K_SEL_MAX, S // 4)
    x2d = x.reshape(B * S, D)
    meta_tile = jnp.concatenate([meta_tokens.astype(x.dtype), jnp.zeros((TQ - N_META, D), x.dtype)], axis=0)

    pos = np.concatenate([N_META + np.arange(S), np.minimum(np.arange(TQ), N_META)]).astype(np.int32)
    ct, st = _rope_tables(jnp.asarray(pos))

    qm, km, vmt, qb, kb, vbt, qi, ki, wi, ga, gb = _proj_call(
        x2d, meta_tile, norm_mix_g.reshape(1, D), _pack_w1(w_in[0]), mla_cq_norm_g.reshape(1, -1),
        mla_ckv_norm_g.reshape(1, -1), _pack_wuq(w_mla_uq[0]), _pack_wukv(w_mla_ukv[0]), ct, st, S)

    o_a = _mla_call(qm, km, vmt, B, S)
    o_b = _dsa_call(qi, wi, qb, ki, kb, vbt, _bias_tiles(rel_bias), B, S, k_sel)

    w_r = jnp.concatenate([w_router_expert[0], w_router_group[0],
                           jnp.zeros((D, LANES - N_EXPERTS - N_GROUPS), F32)], axis=1)
    b_r = jnp.concatenate([b_router_expert[0], b_router_group[0],
                           jnp.zeros((LANES - N_EXPERTS - N_GROUPS,), F32)]).reshape(1, LANES)
    h1, xn, gates = _merge_call(x2d, meta_tile, o_a, o_b, ga, gb, w_branch_a[0].astype(BF16),
                                w_branch_b[0].astype(BF16), w_out[0].astype(BF16),
                                norm_ffn_g.reshape(1, D), w_r, b_r)

    n_tiles = B * S // TQ + 1
    out = _moe_call(xn, gates, h1, w_exp_gate[0], w_exp_up[0], w_exp_down[0],
                    norm_final_g.reshape(1, D), B * S, _moe_tile(n_tiles, 6) * TQ)
    return out.reshape(B, S, D)
```

```python
import functools
import math

import numpy as np
import jax
import jax.numpy as jnp
from jax import lax
from jax.experimental import pallas as pl
from jax.experimental.pallas import tpu as pltpu

CHUNK = 64
N_META = 16
NEG_INF = -1e30
RMS_EPS = 1e-6
ROPE_BASE = 10000.0
MLA_HEADS = 8
MLA_Q_LORA = 256
MLA_KV_LORA = 128
MLA_NOPE = 64
MLA_ROPE = 32
MLA_V = 64
DSA_HEADS = 8
DSA_HEAD_DIM = 64
IDX_HEADS = 8
IDX_DIM = 64
K_SEL_MAX = 256
REL_BUCKETS = 32
REL_MAX_DIST = 128
N_GROUPS = 4
EXPERTS_PER_GROUP = 8
N_EXPERTS = N_GROUPS * EXPERTS_PER_GROUP
D_EXPERT = 256

LANES = 128
TQ = 256
HEAD_PAD = 128
VMEM_LIMIT = 56 * 1024 * 1024
INT_MIN = -2 ** 31
LOG2E = math.log2(math.e)

F32 = jnp.float32
BF16 = jnp.bfloat16


def _params(sem):
    return pltpu.CompilerParams(dimension_semantics=sem, vmem_limit_bytes=VMEM_LIMIT)


def _rms(x, g):
    return x * lax.rsqrt(jnp.mean(x * x, axis=-1, keepdims=True) + RMS_EPS) * g


def _dot(a, b):
    return jnp.dot(a, b, preferred_element_type=F32)


def _dot_t(a, b):
    return lax.dot_general(a, b, (((1,), (1,)), ((), ())), preferred_element_type=F32)


_C_CQ = (0, 256)
_C_CKV = (256, 384)
_C_KRX = (384, 512)
_C_KRY = (512, 640)
_C_QB = (640, 1152)
_C_KB = (1152, 1664)
_C_VB = (1664, 2176)
_C_QI = (2176, 2688)
_C_KI = (2688, 2816)
_C_WI = (2816, 2944)
_C_GA = (2944, 3968)
_C_GB = (3968, 4992)
_W1_COLS = 4992


def _proj_kernel(x_ref, meta_ref, g_ref, w1_ref, gcq_ref, gckv_ref, wuq_ref, wukv_ref, ct_ref, st_ref,
                 qm_ref, km_ref, vm_ref, qb_ref, kb_ref, vb_ref, qi_ref, ki_ref, wi_ref,
                 ga_ref, gb_ref):
    is_meta = pl.program_id(0) == pl.num_programs(0) - 1
    h = jnp.where(is_meta, meta_ref[...], x_ref[...])
    xb = _rms(h, g_ref[...]).astype(BF16)

    def seg(c):
        return _dot(xb, w1_ref[:, c[0]:c[1]])

    ct = ct_ref[...]
    st = st_ref[...]
    nq = _rms(seg(_C_CQ), gcq_ref[...]).astype(BF16)
    qa = _dot(nq, wuq_ref[...])
    half = MLA_HEADS * HEAD_PAD
    for hd in range(MLA_HEADS):
        lo, hi = hd * HEAD_PAD, (hd + 1) * HEAD_PAD
        qm_ref[:, lo:hi] = (qa[:, lo:hi] * ct + qa[:, half + lo:half + hi] * st).astype(BF16)
    nkv = _rms(seg(_C_CKV), gckv_ref[...]).astype(BF16)
    kva = _dot(nkv, wukv_ref[...])
    kr = seg(_C_KRX) * ct + seg(_C_KRY) * st
    for hd in range(MLA_HEADS):
        lo, hi = hd * HEAD_PAD, (hd + 1) * HEAD_PAD
        km_ref[:, lo:hi] = (kva[:, lo:hi] + kr).astype(BF16)
    vm_ref[0] = kva[:, half:].T.astype(BF16)
    qb_ref[...] = seg(_C_QB).astype(BF16)
    kb_ref[...] = seg(_C_KB).astype(BF16)
    vb_ref[0] = seg(_C_VB).T.astype(BF16)
    qi_ref[...] = seg(_C_QI).astype(BF16)
    ki_ref[...] = seg(_C_KI).astype(BF16)
    wi_ref[...] = seg(_C_WI)
    ga_ref[...] = seg(_C_GA)
    gb_ref[...] = seg(_C_GB)


def _pack_w1(w_in):
    D = w_in.shape[0]
    offs = np.cumsum([0, MLA_Q_LORA, MLA_KV_LORA, MLA_ROPE, 512, 512, 512, 512, IDX_DIM, IDX_HEADS, D, D])
    c = lambda i: w_in[:, offs[i]:offs[i + 1]]
    z = lambda n: jnp.zeros((D, n), w_in.dtype)
    k_r = c(2)
    hr = MLA_ROPE // 2
    k_r_sw = jnp.concatenate([k_r[:, hr:], k_r[:, :hr]], axis=1)
    cols = [
        c(0), c(1),
        z(MLA_NOPE), k_r, z(LANES - MLA_NOPE - MLA_ROPE),
        z(MLA_NOPE), k_r_sw, z(LANES - MLA_NOPE - MLA_ROPE),
        c(3), c(4), c(5),
        c(6) * (IDX_DIM ** -0.5),
        c(7), c(7),
        c(8), z(LANES - IDX_HEADS),
        c(9), c(10),
    ]
    w1 = jnp.concatenate(cols, axis=1)
    assert w1.shape[1] == _W1_COLS
    return w1.astype(BF16)


def _pack_wuq(w_uq):
    R = w_uq.shape[0]
    w = w_uq.reshape(R, MLA_HEADS, MLA_NOPE + MLA_ROPE)
    nope, rope = w[..., :MLA_NOPE], w[..., MLA_NOPE:]
    hr = MLA_ROPE // 2
    rope_sw = jnp.concatenate([rope[..., hr:], rope[..., :hr]], axis=-1)
    zt = jnp.zeros((R, MLA_HEADS, HEAD_PAD - MLA_NOPE - MLA_ROPE), w.dtype)
    a = jnp.concatenate([nope, rope, zt], axis=-1).reshape(R, MLA_HEADS * HEAD_PAD)
    b = jnp.concatenate([jnp.zeros_like(nope), rope_sw, zt], axis=-1).reshape(R, MLA_HEADS * HEAD_PAD)
    return jnp.concatenate([a, b], axis=1).astype(BF16)


def _pack_wukv(w_ukv):
    R = w_ukv.shape[0]
    w = w_ukv.reshape(R, MLA_HEADS, MLA_NOPE + MLA_V)
    k = jnp.concatenate([w[..., :MLA_NOPE], jnp.zeros((R, MLA_HEADS, HEAD_PAD - MLA_NOPE), w.dtype)], axis=-1)
    v = w[..., MLA_NOPE:]
    return jnp.concatenate([k.reshape(R, -1), v.reshape(R, -1)], axis=1).astype(BF16)


def _rope_tables(pos):
    half = MLA_ROPE // 2
    inv = ROPE_BASE ** (-jnp.arange(half, dtype=F32) / half)
    ang = pos.astype(F32)[:, None] * inv[None, :]
    cos, sin = jnp.cos(ang), jnp.sin(ang)
    n = pos.shape[0]
    tail = jnp.zeros((n, HEAD_PAD - MLA_NOPE - MLA_ROPE), F32)
    ct = jnp.concatenate([jnp.ones((n, MLA_NOPE), F32), cos, cos, tail], axis=1)
    st = jnp.concatenate([jnp.zeros((n, MLA_NOPE), F32), -sin, sin, tail], axis=1)
    return ct, st


def _proj_call(x2d, meta_tile, g, w1, gcq, gckv, wuq, wukv, ct, st, S):
    BS, D = x2d.shape
    nt = BS // TQ + 1
    per_b = S // TQ
    row = lambda w: pl.BlockSpec((TQ, w), lambda i: (i, 0))
    full = lambda a: pl.BlockSpec(a.shape, lambda i: (0, 0))
    tab = pl.BlockSpec((TQ, HEAD_PAD), lambda i: (jnp.where(i == nt - 1, per_b, i % per_b), 0))
    widths = [(MLA_HEADS * HEAD_PAD, BF16), (MLA_HEADS * HEAD_PAD, BF16), (None, BF16),
              (512, BF16), (512, BF16), (None, BF16), (512, BF16), (LANES, BF16), (LANES, F32),
              (D, F32), (D, F32)]
    vt_rows = MLA_HEADS * MLA_V
    vt_spec = pl.BlockSpec((1, vt_rows, TQ), lambda i: (i, 0, 0))
    vt_shape = jax.ShapeDtypeStruct((nt, vt_rows, TQ), BF16)
    return pl.pallas_call(
        _proj_kernel,
        grid=(nt,),
        in_specs=[pl.BlockSpec((TQ, D), lambda i: (jnp.minimum(i, nt - 2), 0)), full(meta_tile),
                  full(g), full(w1), full(gcq), full(gckv), full(wuq), full(wukv), tab, tab],
        out_specs=[vt_spec if w is None else row(w) for w, _ in widths],
        out_shape=[vt_shape if w is None else jax.ShapeDtypeStruct((nt * TQ, w), dt) for w, dt in widths],
        compiler_params=_params(("parallel",)),
        name="proj",
    )(x2d, meta_tile, g, w1, gcq, gckv, wuq, wukv, ct, st)


def _colmax(z):
    return jnp.max(z, axis=0, keepdims=True)


def _pv(vt, z, m, l, acc):
    p = jnp.exp2(z - m)
    return l + jnp.sum(p, axis=0, keepdims=True), acc + _dot(vt, p.astype(BF16))


def _tile_ids(nq):
    g = pl.program_id(1)
    is_meta = g == pl.num_programs(1) - 1
    return is_meta, jnp.where(is_meta, 0, g % nq)


def _chunk_mask():
    key = lax.broadcasted_iota(jnp.int32, (TQ, TQ), 0)
    qry = lax.broadcasted_iota(jnp.int32, (TQ, TQ), 1)
    shift = CHUNK.bit_length() - 1
    return (key >> shift) <= (qry >> shift)


_CHUNKS = (4, 2, 1)


def _fori_chunks(lo, hi, body_n, carry):
    for n in _CHUNKS:
        cnt = (hi - lo) // n
        carry = lax.fori_loop(0, cnt, lambda u, c, lo=lo, n=n: body_n(lo + n * u, c, n), carry)
        lo = lo + cnt * n
    return carry


def _fori_each(lo, hi, body, carry):
    def body_n(t, c, n):
        for i in range(n):
            c = body(t + i, c)
        return c
    return _fori_chunks(lo, hi, body_n, carry)


def _loop(lo, hi, fn):
    _fori_each(lo, hi, lambda t, c: (fn(t), c)[1], 0)


def _mla_kernel(q_ref, k_ref, vt_ref, km_ref, vtm_ref, o_ref, s_scr, sm_scr, *, nq):
    is_meta, j = _tile_ids(nq)
    n_full = jnp.where(is_meta, 0, j)
    n_vis = jnp.where(is_meta, 0, j + 1)
    c = (MLA_NOPE + MLA_ROPE) ** -0.5 * LOG2E
    qs = [q_ref[:, hh * HEAD_PAD:(hh + 1) * HEAD_PAD] for hh in range(2)]
    hcols = [slice(hh * HEAD_PAD, (hh + 1) * HEAD_PAD) for hh in range(2)]
    vrows = [slice(hh * MLA_V, (hh + 1) * MLA_V) for hh in range(2)]
    meta_ok = lax.broadcasted_iota(jnp.int32, (LANES, TQ), 0) < N_META
    diag_ok = _chunk_mask()

    ms = []
    for hh in range(2):
        z = jnp.where(meta_ok, _dot_t(km_ref[:, hcols[hh]], qs[hh]) * c, NEG_INF)
        sm_scr[hh] = z
        ms.append(_colmax(z))

    def score_tiles(t, ms, n, masked):
        ks = pl.multiple_of(t * TQ, TQ)
        out = []
        for hh in range(2):
            z = _dot_t(k_ref[pl.ds(ks, n * TQ), hcols[hh]], qs[hh]) * c
            if masked:
                z = jnp.where(diag_ok, z, NEG_INF)
            for i in range(n):
                s_scr[hh, t + i] = z[i * TQ:(i + 1) * TQ]
            out.append(jnp.maximum(ms[hh], _colmax(z)))
        return tuple(out)

    ms = _fori_chunks(0, n_full, lambda t, m, n: score_tiles(t, m, n, False), tuple(ms))
    ms = lax.fori_loop(n_full, n_vis, lambda t, m: score_tiles(t, m, 1, True), ms)

    st = []
    for hh in range(2):
        st += list(_pv(vtm_ref[0, vrows[hh], 0:LANES], sm_scr[hh], ms[hh],
                       jnp.zeros((1, TQ), F32), jnp.zeros((MLA_V, TQ), F32)))

    def pv_tile(t, st):
        out = []
        for hh in range(2):
            out += list(_pv(vt_ref[t, vrows[hh], :], s_scr[hh, t], ms[hh], st[2 * hh], st[2 * hh + 1]))
        return tuple(out)

    st = _fori_each(0, n_vis, pv_tile, tuple(st))
    ot = jnp.concatenate([st[1] / st[0], st[3] / st[2]], axis=0)
    o_ref[...] = ot.T.astype(BF16)


def _mla_call(qm, km, vmt, B, S):
    nq = S // TQ
    ng = B * nq + 1
    pairs = MLA_HEADS // 2
    bidx = lambda g: jnp.minimum(g // nq, B - 1)
    mrow = B * S // LANES
    return pl.pallas_call(
        functools.partial(_mla_kernel, nq=nq),
        grid=(pairs, ng),
        in_specs=[
            pl.BlockSpec((TQ, 2 * HEAD_PAD), lambda p, g: (g, p)),
            pl.BlockSpec((S, 2 * HEAD_PAD), lambda p, g: (bidx(g), p)),
            pl.BlockSpec((nq, 2 * MLA_V, TQ), lambda p, g: (bidx(g), p, 0)),
            pl.BlockSpec((LANES, 2 * HEAD_PAD), lambda p, g: (mrow, p)),
            pl.BlockSpec((1, 2 * MLA_V, TQ), lambda p, g: (ng - 1, p, 0)),
        ],
        out_specs=pl.BlockSpec((TQ, 2 * MLA_V), lambda p, g: (g, p)),
        out_shape=jax.ShapeDtypeStruct((ng * TQ, MLA_HEADS * MLA_V), BF16),
        scratch_shapes=[pltpu.VMEM((2, nq, TQ, TQ), F32),
                        pltpu.VMEM((2, LANES, TQ), F32)],
        compiler_params=_params(("parallel", "arbitrary")),
        name="mla_attn",
    )(qm, km, vmt, km, vmt)


_IDX_BITS = 13
_B_DIAG, _B_PREV, _B_META_FIRST, _B_META_META = 0, 1, 2, 3


def _dsa_kernel(qi_ref, wi_ref, qb_ref, ki_ref, kb_ref, vbt_ref, kim_ref, kbm_ref, vbtm_ref, bias_ref, o_ref,
                keys_scr, keym_scr, sel_scr, selm_scr, s_scr, sm_scr, qs_scr, mstar_scr, *, nq, k_sel):
    is_meta, j = _tile_ids(nq)
    n_full = jnp.where(is_meta, 0, j)
    n_vis = jnp.where(is_meta, 0, j + 1)
    lane = lax.broadcasted_iota(jnp.int32, (TQ, LANES), 1)
    low = lane < DSA_HEAD_DIM
    krow_m = lax.broadcasted_iota(jnp.int32, (LANES, TQ), 0)
    krow = lax.broadcasted_iota(jnp.int32, (TQ, TQ), 0)
    meta_ok = krow_m < N_META
    diag_ok = _chunk_mask()

    for p in range(IDX_HEADS // 2):
        qp = qi_ref[:, p * LANES:(p + 1) * LANES]
        qs_scr[2 * p] = jnp.where(low, qp, jnp.zeros_like(qp))
        qs_scr[2 * p + 1] = jnp.where(low, jnp.zeros_like(qp), qp)
    wt = (wi_ref[...] * (IDX_HEADS ** -0.5)).T

    def sort_key(score):
        bits = pltpu.bitcast(score + 0.0, jnp.int32)
        return bits ^ ((bits >> 31) & 0x7FFFFFFF)

    def score_of(kt):
        score = jnp.zeros((kt.shape[0], TQ), F32)
        for h in range(IDX_HEADS):
            score = score + wt[h:h + 1, :] * jnp.maximum(_dot_t(kt, qs_scr[h]), 0.0)
        return sort_key(score)

    keym_scr[...] = jnp.where(meta_ok, score_of(kim_ref[...]), INT_MIN)

    def score_tiles(t, n, masked):
        key = score_of(ki_ref[pl.ds(pl.multiple_of(t * TQ, TQ), n * TQ), :])
        if masked:
            key = jnp.where(diag_ok, key, INT_MIN)
        for i in range(n):
            keys_scr[t + i] = key[i * TQ:(i + 1) * TQ]

    _fori_chunks(0, n_full, lambda t, c, n: (score_tiles(t, n, False), c)[1], 0)
    lax.fori_loop(n_full, n_vis, lambda t, c: (score_tiles(t, 1, True), c)[1], 0)

    one = lambda m: jnp.where(m, 1, 0)
    fold = lambda a: jnp.sum(a.reshape(-1, 8, TQ), axis=0)

    def count(pred):
        def cbody(t, acc):
            return acc + fold(pred(keys_scr[t], LANES + t * TQ + krow))
        acc = _fori_each(0, n_vis, cbody, fold(pred(keym_scr[...], krow_m)))
        return jnp.sum(acc, axis=0, keepdims=True)

    t0 = jnp.where(count(lambda b, _: one(b >= 0)) >= k_sel, 0, INT_MIN)

    def bs_body(i, t):
        cand = t | lax.shift_left(jnp.int32(1), 30 - i)
        return jnp.where(count(lambda b, _: one(b >= cand)) >= k_sel, cand, t)

    t = lax.fori_loop(0, 31, bs_body, t0)
    cnt_gt = count(lambda b, _: one(b > t))
    cnt_ge = count(lambda b, _: one(b >= t))
    need = k_sel - cnt_gt

    mstar_scr[...] = jnp.full((1, TQ), 2 ** _IDX_BITS, jnp.int32)
    tied = jnp.max(jnp.where((cnt_ge > k_sel) & (t != INT_MIN), 1, 0))

    @pl.when(tied > 0)
    def _():
        def tie_body(i, r):
            cand = r | lax.shift_left(jnp.int32(1), _IDX_BITS - 1 - i)
            f = count(lambda b, idx: jnp.where(b == t, one(idx < cand), 0))
            return jnp.where(f < need, cand, r)
        r = lax.fori_loop(0, _IDX_BITS, tie_body, jnp.zeros((1, TQ), jnp.int32))
        mstar_scr[...] = r + 1

    mstar = mstar_scr[...]

    def sel_mask(b, idx):
        tie_ok = jnp.where(b == t, one(idx < mstar), 0)
        sel = jnp.where(b == INT_MIN, 0, jnp.where(b > t, 1, tie_ok))
        return jnp.where(sel > 0, 0.0, NEG_INF)

    selm_scr[...] = sel_mask(keym_scr[...], krow_m)
    _loop(0, n_vis, lambda t: sel_scr.__setitem__(t, sel_mask(keys_scr[t], LANES + t * TQ + krow)))

    c = DSA_HEAD_DIM ** -0.5 * LOG2E
    n_far = jnp.maximum(n_full - 1, 0)
    use_mb = jnp.where(is_meta | (j == 0), 1.0, 0.0)
    mb_idx = jnp.where(is_meta, _B_META_META, _B_META_FIRST)
    for p in range(DSA_HEADS // 2):
        qp = qb_ref[:, p * LANES:(p + 1) * LANES]
        qs = [jnp.where(low, qp, jnp.zeros_like(qp)), jnp.where(low, jnp.zeros_like(qp), qp)]
        cols = slice(p * LANES, (p + 1) * LANES)
        vrows = [slice((2 * p + hh) * DSA_HEAD_DIM, (2 * p + hh + 1) * DSA_HEAD_DIM) for hh in range(2)]

        ms = []
        for hh in range(2):
            z = (_dot_t(kbm_ref[:, cols], qs[hh]) * c + bias_ref[2 * p + hh, mb_idx][:LANES, :] * use_mb
                 + selm_scr[...])
            sm_scr[hh] = z
            ms.append(_colmax(z))

        def score_tiles(t, ms, n, near, qs=qs, cols=cols, p=p):
            k = kb_ref[pl.ds(pl.multiple_of(t * TQ, TQ), n * TQ), cols]
            out = []
            for hh in range(2):
                zz = _dot_t(k, qs[hh]) * c
                m = ms[hh]
                for i in range(n):
                    z = zz[i * TQ:(i + 1) * TQ] + sel_scr[t + i]
                    if near is not None:
                        z = z + bias_ref[2 * p + hh, near]
                    s_scr[hh, t + i] = z
                    m = jnp.maximum(m, _colmax(z))
                out.append(m)
            return tuple(out)

        ms = _fori_chunks(0, n_far, lambda t, m, n: score_tiles(t, m, n, None), tuple(ms))
        ms = lax.fori_loop(n_far, n_full, lambda t, m: score_tiles(t, m, 1, _B_PREV), ms)
        ms = lax.fori_loop(n_full, n_vis, lambda t, m: score_tiles(t, m, 1, _B_DIAG), ms)

        st = []
        for hh in range(2):
            st += list(_pv(vbtm_ref[0, vrows[hh], 0:LANES], sm_scr[hh], ms[hh],
                           jnp.zeros((1, TQ), F32), jnp.zeros((DSA_HEAD_DIM, TQ), F32)))

        def pv_tile(t, st, ms=ms, vrows=vrows):
            out = []
            for hh in range(2):
                out += list(_pv(vbt_ref[t, vrows[hh], :], s_scr[hh, t], ms[hh], st[2 * hh], st[2 * hh + 1]))
            return tuple(out)

        st = _fori_each(0, n_vis, pv_tile, tuple(st))
        ot = jnp.concatenate([st[1] / st[0], st[3] / st[2]], axis=0)
        o_ref[:, cols] = ot.T.astype(BF16)


def _dsa_call(qi, wi, qb, ki, kb, vbt, bias, B, S, k_sel):
    nq = S // TQ
    ng = B * nq + 1
    W = DSA_HEADS * DSA_HEAD_DIM
    bidx = lambda g: jnp.minimum(g // nq, B - 1)
    mrow = B * S // LANES
    qrow = lambda w: pl.BlockSpec((TQ, w), lambda _, g: (g, 0))
    kv = lambda w: pl.BlockSpec((S, w), lambda _, g: (bidx(g), 0))
    kvm = lambda w: pl.BlockSpec((LANES, w), lambda _, g: (mrow, 0))
    return pl.pallas_call(
        functools.partial(_dsa_kernel, nq=nq, k_sel=k_sel),
        grid=(1, ng),
        in_specs=[qrow(W), qrow(LANES), qrow(W), kv(LANES), kv(W),
                  pl.BlockSpec((nq, W, TQ), lambda _, g: (bidx(g), 0, 0)),
                  kvm(LANES), kvm(W),
                  pl.BlockSpec((1, W, TQ), lambda _, g: (ng - 1, 0, 0)),
                  pl.BlockSpec(bias.shape, lambda _, g: (0, 0, 0, 0))],
        out_specs=qrow(W),
        out_shape=jax.ShapeDtypeStruct((ng * TQ, W), BF16),
        scratch_shapes=[
            pltpu.VMEM((nq, TQ, TQ), jnp.int32),
            pltpu.VMEM((LANES, TQ), jnp.int32),
            pltpu.VMEM((nq, TQ, TQ), F32),
            pltpu.VMEM((LANES, TQ), F32),
            pltpu.VMEM((2, nq, TQ, TQ), F32),
            pltpu.VMEM((2, LANES, TQ), F32),
            pltpu.VMEM((IDX_HEADS, TQ, LANES), BF16),
            pltpu.VMEM((1, TQ), jnp.int32),
        ],
        compiler_params=_params(("arbitrary", "arbitrary")),
        name="dsa",
    )(qi, wi, qb, ki, kb, vbt, ki, kb, vbt, bias)


def _t5_bucket(rel):
    nb = REL_BUCKETS // 2
    max_exact = nb // 2
    n = jnp.abs(rel)
    large = max_exact + (jnp.log(jnp.maximum(n, 1).astype(F32) / max_exact)
                         / math.log(REL_MAX_DIST / max_exact) * (nb - max_exact)).astype(jnp.int32)
    large = jnp.minimum(large, nb - 1)
    return jnp.where(rel > 0, nb, 0) + jnp.where(n < max_exact, n, large)


def _bias_kernel(bucket_ref, rb_ref, o_ref, *, far_bucket):
    h = pl.program_id(0)
    far = rb_ref[far_bucket, h]
    for tile in range(bucket_ref.shape[0]):
        bkt = bucket_ref[tile]
        acc = jnp.zeros(bkt.shape, F32)
        for b in range(REL_BUCKETS):
            acc = jnp.where(bkt == b, rb_ref[b, h], acc)
        o_ref[0, tile] = (acc - far) * LOG2E


def _bias_tiles(rel_bias):
    k = jnp.arange(TQ, dtype=jnp.int32)[:, None]
    q = jnp.arange(TQ, dtype=jnp.int32)[None, :]
    rels = jnp.stack([k - q, k - q - TQ, k - (q + N_META), k - q])
    far_bucket = REL_BUCKETS // 2 - 1
    return pl.pallas_call(
        functools.partial(_bias_kernel, far_bucket=far_bucket),
        grid=(DSA_HEADS,),
        in_specs=[pl.BlockSpec((4, TQ, TQ), lambda h: (0, 0, 0)),
                  pl.BlockSpec(memory_space=pltpu.SMEM)],
        out_specs=pl.BlockSpec((1, 4, TQ, TQ), lambda h: (h, 0, 0, 0)),
        out_shape=jax.ShapeDtypeStruct((DSA_HEADS, 4, TQ, TQ), F32),
        compiler_params=_params(("parallel",)),
        name="bias_tiles",
    )(_t5_bucket(rels), rel_bias)


_ROUTER_ROWS = 40
_MERGE_ROWS = 128


def _merge_kernel(x_ref, meta_ref, oa_ref, ob_ref, ga_ref, gb_ref, wa_ref, wb_ref, wo_ref, gf_ref, wrt_ref, brt_ref,
                  h1_ref, xn_ref, gate_ref):
    is_meta = pl.program_id(0) == pl.num_programs(0) - 1
    for rows in (slice(s, s + _MERGE_ROWS) for s in range(0, x_ref.shape[0], _MERGE_ROWS)):
        h = jnp.where(is_meta, meta_ref[rows, :], x_ref[rows, :])
        y = (jax.nn.sigmoid(ga_ref[rows, :]) * _dot(oa_ref[rows, :], wa_ref[...])
             + jax.nn.sigmoid(gb_ref[rows, :]) * _dot(ob_ref[rows, :], wb_ref[...]))
        h1 = h + _dot(y.astype(BF16), wo_ref[...])
        h1_ref[rows, :] = h1
        xn = _rms(h1, gf_ref[...])
        xn_ref[rows, :] = xn.astype(BF16)
        gate_ref[rows, :] = _route(xn, wrt_ref[...], brt_ref[...])


def _route(xn, wrt, brt):
    nr = _ROUTER_ROWS
    tm = xn.shape[0]
    lt = lax.dot_general(wrt, xn, (((1,), (1,)), ((), ())), preferred_element_type=F32,
                         precision=lax.Precision.HIGHEST) + brt
    row = lax.broadcasted_iota(jnp.int32, (nr, tm), 0)
    ninf = -jnp.inf
    cmax = lambda a: jnp.max(a, axis=0, keepdims=True)
    cmin = lambda a: jnp.min(a, axis=0, keepdims=True)
    gl = jnp.where((row >= N_EXPERTS) & (row < N_EXPERTS + N_GROUPS), lt, ninf)
    gmax = cmax(gl)
    gsel = cmin(jnp.where(gl == gmax, row, nr)) - N_EXPERTS
    p_group = 1.0 / jnp.sum(jnp.exp(gl - gmax), axis=0, keepdims=True)
    lo = gsel * EXPERTS_PER_GROUP
    el = jnp.where((row >= lo) & (row < lo + EXPERTS_PER_GROUP), lt, ninf)
    m1 = cmax(el)
    i1 = cmin(jnp.where(el == m1, row, nr))
    el2 = jnp.where(row == i1, ninf, el)
    m2 = cmax(el2)
    i2 = cmin(jnp.where(el2 == m2, row, nr))
    e2 = jnp.exp(m2 - m1)
    w1 = p_group / (1.0 + e2)
    w2 = p_group * e2 / (1.0 + e2)
    gt = jnp.where(row == i1, w1, 0.0) + jnp.where(row == i2, w2, 0.0)
    return jnp.concatenate([gt, jnp.zeros((LANES - nr, tm), F32)], axis=0).T


def _merge_call(x2d, meta_tile, oa, ob, ga, gb, wa, wb, wo, gf, wr, br):
    BS, D = x2d.shape
    nt = BS // TQ + 1
    row = lambda w: pl.BlockSpec((TQ, w), lambda i: (i, 0))
    full = lambda a: pl.BlockSpec(a.shape, lambda i: (0, 0))
    return pl.pallas_call(
        _merge_kernel,
        grid=(nt,),
        in_specs=[pl.BlockSpec((TQ, D), lambda i: (jnp.minimum(i, nt - 2), 0)), full(meta_tile),
                  row(oa.shape[1]), row(ob.shape[1]), row(D), row(D),
                  full(wa), full(wb), full(wo), full(gf), full(wr), full(br)],
        out_specs=[row(D), row(D), row(LANES)],
        out_shape=[jax.ShapeDtypeStruct((nt * TQ, D), F32), jax.ShapeDtypeStruct((nt * TQ, D), BF16),
                   jax.ShapeDtypeStruct((nt * TQ, LANES), F32)],
        compiler_params=_params(("parallel",)),
        name="merge_route",
    )(x2d, meta_tile, oa, ob, ga, gb, wa, wb, wo, gf, wr, br)


def _moe_kernel(x_ref, gate_ref, h1_ref, wg_ref, wu_ref, wd_ref, gfin_ref, o_ref, acc_ref):
    e = pl.program_id(1)

    @pl.when(e == 0)
    def _():
        acc_ref[...] = jnp.zeros_like(acc_ref)

    x = x_ref[...]
    a = _dot(x, wg_ref[0].astype(BF16))
    u = _dot(x, wu_ref[0].astype(BF16))
    hmid = (a * jax.nn.sigmoid(a) * u).astype(BF16)
    y = _dot(hmid, wd_ref[0].astype(BF16))
    gates = gate_ref[...]
    lane = lax.broadcasted_iota(jnp.int32, gates.shape, 1)
    w = jnp.sum(jnp.where(lane == e, gates, 0.0), axis=1, keepdims=True)
    acc_ref[...] += w * y

    @pl.when(e == pl.num_programs(1) - 1)
    def _():
        o_ref[...] = _rms(h1_ref[...] + acc_ref[...], gfin_ref[...])


def _moe_call(xn, gates, h1, wg, wu, wd, gfin, n_out, tm):
    R, D = h1.shape
    E = wg.shape[0]
    row = lambda w: pl.BlockSpec((tm, w), lambda i, e: (i, 0))
    return pl.pallas_call(
        _moe_kernel,
        grid=(R // tm, E),
        in_specs=[row(D), row(LANES), row(D),
                  pl.BlockSpec((1, D, D_EXPERT), lambda i, e: (e, 0, 0)),
                  pl.BlockSpec((1, D, D_EXPERT), lambda i, e: (e, 0, 0)),
                  pl.BlockSpec((1, D_EXPERT, D), lambda i, e: (e, 0, 0)),
                  pl.BlockSpec(gfin.shape, lambda i, e: (0, 0))],
        out_specs=row(D),
        out_shape=jax.ShapeDtypeStruct((n_out, D), F32),
        scratch_shapes=[pltpu.VMEM((tm, D), F32)],
        compiler_params=_params(("parallel", "arbitrary")),
        name="moe",
    )(xn, gates, h1, wg, wu, wd, gfin)


def _moe_tile(n_tiles, cap):
    return max(d for d in range(1, cap + 1) if n_tiles % d == 0)


def kernel(x, meta_tokens, norm_mix_g, w_in, mla_cq_norm_g, mla_ckv_norm_g, w_mla_uq, w_mla_ukv,
           w_branch_a, w_branch_b, w_out, rel_bias, norm_ffn_g, w_router_group, b_router_group,
           w_router_expert, b_router_expert, w_exp_gate, w_exp_up, w_exp_down, norm_final_g):
    B, S, D = x.shape
    assert S % TQ == 0 and norm_mix_g.shape[0] == 1
    k_sel = min(K_SEL_MAX, S // 4)
    x2d = x.reshape(B * S, D)
    meta_tile = jnp.concatenate([meta_tokens.astype(x.dtype), jnp.zeros((TQ - N_META, D), x.dtype)], axis=0)

    pos = np.concatenate([N_META + np.arange(S), np.minimum(np.arange(TQ), N_META)]).astype(np.int32)
    ct, st = _rope_tables(jnp.asarray(pos))

    qm, km, vmt, qb, kb, vbt, qi, ki, wi, ga, gb = _proj_call(
        x2d, meta_tile, norm_mix_g.reshape(1, D), _pack_w1(w_in[0]), mla_cq_norm_g.reshape(1, -1),
        mla_ckv_norm_g.reshape(1, -1), _pack_wuq(w_mla_uq[0]), _pack_wukv(w_mla_ukv[0]), ct, st, S)

    o_a = _mla_call(qm, km, vmt, B, S)
    o_b = _dsa_call(qi, wi, qb, ki, kb, vbt, _bias_tiles(rel_bias), B, S, k_sel)

    pad_r = _ROUTER_ROWS - N_EXPERTS - N_GROUPS
    w_r = jnp.concatenate([w_router_expert[0].T, w_router_group[0].T, jnp.zeros((pad_r, D), F32)], axis=0)
    b_r = jnp.concatenate([b_router_expert[0], b_router_group[0],
                           jnp.zeros((pad_r,), F32)]).reshape(_ROUTER_ROWS, 1)
    h1, xn, gates = _merge_call(x2d, meta_tile, o_a, o_b, ga, gb, w_branch_a[0].astype(BF16),
                                w_branch_b[0].astype(BF16), w_out[0].astype(BF16),
                                norm_ffn_g.reshape(1, D), w_r, b_r)

    n_tiles = B * S // TQ + 1
    out = _moe_call(xn, gates, h1, w_exp_gate[0], w_exp_up[0], w_exp_down[0],
                    norm_final_g.reshape(1, D), B * S, _moe_tile(n_tiles, 6) * TQ)
    return out.reshape(B, S, D)
```

```python
import functools
import math

import numpy as np
import jax
import jax.numpy as jnp
from jax import lax
from jax.experimental import pallas as pl
from jax.experimental.pallas import tpu as pltpu

CHUNK = 64
N_META = 16
NEG_INF = -1e30
RMS_EPS = 1e-6
ROPE_BASE = 10000.0
MLA_HEADS = 8
MLA_Q_LORA = 256
MLA_KV_LORA = 128
MLA_NOPE = 64
MLA_ROPE = 32
MLA_V = 64
DSA_HEADS = 8
DSA_HEAD_DIM = 64
IDX_HEADS = 8
IDX_DIM = 64
K_SEL_MAX = 256
REL_BUCKETS = 32
REL_MAX_DIST = 128
N_GROUPS = 4
EXPERTS_PER_GROUP = 8
N_EXPERTS = N_GROUPS * EXPERTS_PER_GROUP
D_EXPERT = 256

LANES = 128
TQ = 256
HEAD_PAD = 128
VMEM_LIMIT = 56 * 1024 * 1024
INT_MIN = -2 ** 31
LOG2E = math.log2(math.e)

F32 = jnp.float32
BF16 = jnp.bfloat16


def _params(sem):
    return pltpu.CompilerParams(dimension_semantics=sem, vmem_limit_bytes=VMEM_LIMIT)


def _rms(x, g):
    return x * lax.rsqrt(jnp.mean(x * x, axis=-1, keepdims=True) + RMS_EPS) * g


def _dot(a, b):
    return jnp.dot(a, b, preferred_element_type=F32)


def _dot_t(a, b):
    return lax.dot_general(a, b, (((1,), (1,)), ((), ())), preferred_element_type=F32)


_C_CQ = (0, 256)
_C_CKV = (256, 384)
_C_KRX = (384, 512)
_C_KRY = (512, 640)
_C_QB = (640, 1152)
_C_KB = (1152, 1664)
_C_VB = (1664, 2176)
_C_QI = (2176, 2688)
_C_KI = (2688, 2816)
_C_WI = (2816, 2944)
_C_GA = (2944, 3968)
_C_GB = (3968, 4992)
_W1_COLS = 4992


def _proj_kernel(x_ref, meta_ref, g_ref, w1_ref, gcq_ref, gckv_ref, wuq_ref, wukv_ref, ct_ref, st_ref,
                 qm_ref, km_ref, vm_ref, qb_ref, kb_ref, vb_ref, qi_ref, ki_ref, wi_ref,
                 ga_ref, gb_ref):
    is_meta = pl.program_id(0) == pl.num_programs(0) - 1
    h = jnp.where(is_meta, meta_ref[...], x_ref[...])
    xb = _rms(h, g_ref[...]).astype(BF16)

    def seg(c):
        return _dot(xb, w1_ref[:, c[0]:c[1]])

    ct = ct_ref[...]
    st = st_ref[...]
    nq = _rms(seg(_C_CQ), gcq_ref[...]).astype(BF16)
    qa = _dot(nq, wuq_ref[...])
    half = MLA_HEADS * HEAD_PAD
    for hd in range(MLA_HEADS):
        lo, hi = hd * HEAD_PAD, (hd + 1) * HEAD_PAD
        qm_ref[:, lo:hi] = (qa[:, lo:hi] * ct + qa[:, half + lo:half + hi] * st).astype(BF16)
    nkv = _rms(seg(_C_CKV), gckv_ref[...]).astype(BF16)
    kva = _dot(nkv, wukv_ref[...])
    kr = seg(_C_KRX) * ct + seg(_C_KRY) * st
    for hd in range(MLA_HEADS):
        lo, hi = hd * HEAD_PAD, (hd + 1) * HEAD_PAD
        km_ref[:, lo:hi] = (kva[:, lo:hi] + kr).astype(BF16)
    vm_ref[0] = kva[:, half:].T.astype(BF16)
    qb_ref[...] = seg(_C_QB).astype(BF16)
    kb_ref[...] = seg(_C_KB).astype(BF16)
    vb_ref[0] = seg(_C_VB).T.astype(BF16)
    qi_ref[...] = seg(_C_QI).astype(BF16)
    ki_ref[...] = seg(_C_KI).astype(BF16)
    wi_ref[...] = seg(_C_WI)
    ga_ref[...] = seg(_C_GA)
    gb_ref[...] = seg(_C_GB)


def _pack_w1(w_in):
    D = w_in.shape[0]
    offs = np.cumsum([0, MLA_Q_LORA, MLA_KV_LORA, MLA_ROPE, 512, 512, 512, 512, IDX_DIM, IDX_HEADS, D, D])
    c = lambda i: w_in[:, offs[i]:offs[i + 1]]
    z = lambda n: jnp.zeros((D, n), w_in.dtype)
    k_r = c(2)
    hr = MLA_ROPE // 2
    k_r_sw = jnp.concatenate([k_r[:, hr:], k_r[:, :hr]], axis=1)
    cols = [
        c(0), c(1),
        z(MLA_NOPE), k_r, z(LANES - MLA_NOPE - MLA_ROPE),
        z(MLA_NOPE), k_r_sw, z(LANES - MLA_NOPE - MLA_ROPE),
        c(3), c(4), c(5),
        c(6) * (IDX_DIM ** -0.5),
        c(7), c(7),
        c(8), z(LANES - IDX_HEADS),
        c(9), c(10),
    ]
    w1 = jnp.concatenate(cols, axis=1)
    assert w1.shape[1] == _W1_COLS
    return w1.astype(BF16)


def _pack_wuq(w_uq):
    R = w_uq.shape[0]
    w = w_uq.reshape(R, MLA_HEADS, MLA_NOPE + MLA_ROPE)
    nope, rope = w[..., :MLA_NOPE], w[..., MLA_NOPE:]
    hr = MLA_ROPE // 2
    rope_sw = jnp.concatenate([rope[..., hr:], rope[..., :hr]], axis=-1)
    zt = jnp.zeros((R, MLA_HEADS, HEAD_PAD - MLA_NOPE - MLA_ROPE), w.dtype)
    a = jnp.concatenate([nope, rope, zt], axis=-1).reshape(R, MLA_HEADS * HEAD_PAD)
    b = jnp.concatenate([jnp.zeros_like(nope), rope_sw, zt], axis=-1).reshape(R, MLA_HEADS * HEAD_PAD)
    return jnp.concatenate([a, b], axis=1).astype(BF16)


def _pack_wukv(w_ukv):
    R = w_ukv.shape[0]
    w = w_ukv.reshape(R, MLA_HEADS, MLA_NOPE + MLA_V)
    k = jnp.concatenate([w[..., :MLA_NOPE], jnp.zeros((R, MLA_HEADS, HEAD_PAD - MLA_NOPE), w.dtype)], axis=-1)
    v = w[..., MLA_NOPE:]
    return jnp.concatenate([k.reshape(R, -1), v.reshape(R, -1)], axis=1).astype(BF16)


def _rope_tables(pos):
    half = MLA_ROPE // 2
    inv = ROPE_BASE ** (-jnp.arange(half, dtype=F32) / half)
    ang = pos.astype(F32)[:, None] * inv[None, :]
    cos, sin = jnp.cos(ang), jnp.sin(ang)
    n = pos.shape[0]
    tail = jnp.zeros((n, HEAD_PAD - MLA_NOPE - MLA_ROPE), F32)
    ct = jnp.concatenate([jnp.ones((n, MLA_NOPE), F32), cos, cos, tail], axis=1)
    st = jnp.concatenate([jnp.zeros((n, MLA_NOPE), F32), -sin, sin, tail], axis=1)
    return ct, st


def _proj_call(x2d, meta_tile, g, w1, gcq, gckv, wuq, wukv, ct, st, S):
    BS, D = x2d.shape
    nt = BS // TQ + 1
    per_b = S // TQ
    row = lambda w: pl.BlockSpec((TQ, w), lambda i: (i, 0))
    full = lambda a: pl.BlockSpec(a.shape, lambda i: (0, 0))
    tab = pl.BlockSpec((TQ, HEAD_PAD), lambda i: (jnp.where(i == nt - 1, per_b, i % per_b), 0))
    widths = [(MLA_HEADS * HEAD_PAD, BF16), (MLA_HEADS * HEAD_PAD, BF16), (None, BF16),
              (512, BF16), (512, BF16), (None, BF16), (512, BF16), (LANES, BF16), (LANES, F32),
              (D, F32), (D, F32)]
    vt_rows = MLA_HEADS * MLA_V
    vt_spec = pl.BlockSpec((1, vt_rows, TQ), lambda i: (i, 0, 0))
    vt_shape = jax.ShapeDtypeStruct((nt, vt_rows, TQ), BF16)
    return pl.pallas_call(
        _proj_kernel,
        grid=(nt,),
        in_specs=[pl.BlockSpec((TQ, D), lambda i: (jnp.minimum(i, nt - 2), 0)), full(meta_tile),
                  full(g), full(w1), full(gcq), full(gckv), full(wuq), full(wukv), tab, tab],
        out_specs=[vt_spec if w is None else row(w) for w, _ in widths],
        out_shape=[vt_shape if w is None else jax.ShapeDtypeStruct((nt * TQ, w), dt) for w, dt in widths],
        compiler_params=_params(("parallel",)),
        name="proj",
    )(x2d, meta_tile, g, w1, gcq, gckv, wuq, wukv, ct, st)


def _colmax(z):
    return jnp.max(z, axis=0, keepdims=True)


def _pv(vt, z, m, l, acc):
    p = jnp.exp2(z - m)
    return l + jnp.sum(p, axis=0, keepdims=True), acc + _dot(vt, p.astype(BF16))


def _tile_ids(nq):
    g = pl.program_id(1)
    is_meta = g == pl.num_programs(1) - 1
    return is_meta, jnp.where(is_meta, 0, g % nq)


def _chunk_mask():
    key = lax.broadcasted_iota(jnp.int32, (TQ, TQ), 0)
    qry = lax.broadcasted_iota(jnp.int32, (TQ, TQ), 1)
    shift = CHUNK.bit_length() - 1
    return (key >> shift) <= (qry >> shift)


_CHUNKS = (4, 2, 1)


def _fori_chunks(lo, hi, body_n, carry):
    for n in _CHUNKS:
        cnt = (hi - lo) // n
        carry = lax.fori_loop(0, cnt, lambda u, c, lo=lo, n=n: body_n(lo + n * u, c, n), carry)
        lo = lo + cnt * n
    return carry


def _fori_each(lo, hi, body, carry):
    def body_n(t, c, n):
        for i in range(n):
            c = body(t + i, c)
        return c
    return _fori_chunks(lo, hi, body_n, carry)


def _loop(lo, hi, fn):
    _fori_each(lo, hi, lambda t, c: (fn(t), c)[1], 0)


def _mla_kernel(q_ref, k_ref, vt_ref, km_ref, vtm_ref, o_ref, s_scr, sm_scr, *, nq):
    is_meta, j = _tile_ids(nq)
    n_full = jnp.where(is_meta, 0, j)
    n_vis = jnp.where(is_meta, 0, j + 1)
    c = (MLA_NOPE + MLA_ROPE) ** -0.5 * LOG2E
    qs = [q_ref[:, hh * HEAD_PAD:(hh + 1) * HEAD_PAD] for hh in range(2)]
    hcols = [slice(hh * HEAD_PAD, (hh + 1) * HEAD_PAD) for hh in range(2)]
    vrows = [slice(hh * MLA_V, (hh + 1) * MLA_V) for hh in range(2)]
    meta_ok = lax.broadcasted_iota(jnp.int32, (LANES, TQ), 0) < N_META
    diag_ok = _chunk_mask()

    ms = []
    for hh in range(2):
        z = jnp.where(meta_ok, _dot_t(km_ref[:, hcols[hh]], qs[hh]) * c, NEG_INF)
        sm_scr[hh] = z
        ms.append(_colmax(z))

    def score_tiles(t, ms, n, masked):
        ks = pl.multiple_of(t * TQ, TQ)
        out = []
        for hh in range(2):
            z = _dot_t(k_ref[pl.ds(ks, n * TQ), hcols[hh]], qs[hh]) * c
            if masked:
                z = jnp.where(diag_ok, z, NEG_INF)
            for i in range(n):
                s_scr[hh, t + i] = z[i * TQ:(i + 1) * TQ]
            out.append(jnp.maximum(ms[hh], _colmax(z)))
        return tuple(out)

    ms = _fori_chunks(0, n_full, lambda t, m, n: score_tiles(t, m, n, False), tuple(ms))
    ms = lax.fori_loop(n_full, n_vis, lambda t, m: score_tiles(t, m, 1, True), ms)

    st = []
    for hh in range(2):
        st += list(_pv(vtm_ref[0, vrows[hh], 0:LANES], sm_scr[hh], ms[hh],
                       jnp.zeros((1, TQ), F32), jnp.zeros((MLA_V, TQ), F32)))

    def pv_tile(t, st):
        out = []
        for hh in range(2):
            out += list(_pv(vt_ref[t, vrows[hh], :], s_scr[hh, t], ms[hh], st[2 * hh], st[2 * hh + 1]))
        return tuple(out)

    st = _fori_each(0, n_vis, pv_tile, tuple(st))
    ot = jnp.concatenate([st[1] / st[0], st[3] / st[2]], axis=0)
    o_ref[...] = ot.T.astype(BF16)


def _mla_call(qm, km, vmt, B, S):
    nq = S // TQ
    ng = B * nq + 1
    pairs = MLA_HEADS // 2
    bidx = lambda g: jnp.minimum(g // nq, B - 1)
    mrow = B * S // LANES
    return pl.pallas_call(
        functools.partial(_mla_kernel, nq=nq),
        grid=(pairs, ng),
        in_specs=[
            pl.BlockSpec((TQ, 2 * HEAD_PAD), lambda p, g: (g, p)),
            pl.BlockSpec((S, 2 * HEAD_PAD), lambda p, g: (bidx(g), p)),
            pl.BlockSpec((nq, 2 * MLA_V, TQ), lambda p, g: (bidx(g), p, 0)),
            pl.BlockSpec((LANES, 2 * HEAD_PAD), lambda p, g: (mrow, p)),
            pl.BlockSpec((1, 2 * MLA_V, TQ), lambda p, g: (ng - 1, p, 0)),
        ],
        out_specs=pl.BlockSpec((TQ, 2 * MLA_V), lambda p, g: (g, p)),
        out_shape=jax.ShapeDtypeStruct((ng * TQ, MLA_HEADS * MLA_V), BF16),
        scratch_shapes=[pltpu.VMEM((2, nq, TQ, TQ), F32),
                        pltpu.VMEM((2, LANES, TQ), F32)],
        compiler_params=_params(("parallel", "arbitrary")),
        name="mla_attn",
    )(qm, km, vmt, km, vmt)


_IDX_BITS = 13
_B_DIAG, _B_PREV, _B_META_FIRST, _B_META_META = 0, 1, 2, 3


def _dsa_kernel(qi_ref, wi_ref, qb_ref, ki_ref, kb_ref, vbt_ref, kim_ref, kbm_ref, vbtm_ref, bias_ref, o_ref,
                keys_scr, keym_scr, sel_scr, selm_scr, s_scr, sm_scr, qs_scr, mstar_scr, *, nq, k_sel):
    is_meta, j = _tile_ids(nq)
    n_full = jnp.where(is_meta, 0, j)
    n_vis = jnp.where(is_meta, 0, j + 1)
    lane = lax.broadcasted_iota(jnp.int32, (TQ, LANES), 1)
    low = lane < DSA_HEAD_DIM
    krow_m = lax.broadcasted_iota(jnp.int32, (LANES, TQ), 0)
    krow = lax.broadcasted_iota(jnp.int32, (TQ, TQ), 0)
    meta_ok = krow_m < N_META
    diag_ok = _chunk_mask()

    for p in range(IDX_HEADS // 2):
        qp = qi_ref[:, p * LANES:(p + 1) * LANES]
        qs_scr[2 * p] = jnp.where(low, qp, jnp.zeros_like(qp))
        qs_scr[2 * p + 1] = jnp.where(low, jnp.zeros_like(qp), qp)
    wt = (wi_ref[...] * (IDX_HEADS ** -0.5)).T

    def sort_key(score):
        bits = pltpu.bitcast(score + 0.0, jnp.int32)
        return bits ^ ((bits >> 31) & 0x7FFFFFFF)

    def score_of(kt):
        score = jnp.zeros((kt.shape[0], TQ), F32)
        for h in range(IDX_HEADS):
            score = score + wt[h:h + 1, :] * jnp.maximum(_dot_t(kt, qs_scr[h]), 0.0)
        return sort_key(score)

    keym_scr[...] = jnp.where(meta_ok, score_of(kim_ref[...]), INT_MIN)

    def score_tiles(t, n, masked):
        key = score_of(ki_ref[pl.ds(pl.multiple_of(t * TQ, TQ), n * TQ), :])
        if masked:
            key = jnp.where(diag_ok, key, INT_MIN)
        for i in range(n):
            keys_scr[t + i] = key[i * TQ:(i + 1) * TQ]

    _fori_chunks(0, n_full, lambda t, c, n: (score_tiles(t, n, False), c)[1], 0)
    lax.fori_loop(n_full, n_vis, lambda t, c: (score_tiles(t, 1, True), c)[1], 0)

    one = lambda m: jnp.where(m, 1, 0)
    fold = lambda a: jnp.sum(a.reshape(-1, 8, TQ), axis=0)

    def count(pred):
        def cbody(t, acc):
            return acc + fold(pred(keys_scr[t], LANES + t * TQ + krow))
        acc = _fori_each(0, n_vis, cbody, fold(pred(keym_scr[...], krow_m)))
        return jnp.sum(acc, axis=0, keepdims=True)

    t0 = jnp.where(count(lambda b, _: one(b >= 0)) >= k_sel, 0, INT_MIN)

    def bs_body(i, t):
        cand = t | lax.shift_left(jnp.int32(1), 30 - i)
        return jnp.where(count(lambda b, _: one(b >= cand)) >= k_sel, cand, t)

    t = lax.fori_loop(0, 31, bs_body, t0)
    cnt_gt = count(lambda b, _: one(b > t))
    cnt_ge = count(lambda b, _: one(b >= t))
    need = k_sel - cnt_gt

    mstar_scr[...] = jnp.full((1, TQ), 2 ** _IDX_BITS, jnp.int32)
    tied = jnp.max(jnp.where((cnt_ge > k_sel) & (t != INT_MIN), 1, 0))

    @pl.when(tied > 0)
    def _():
        def tie_body(i, r):
            cand = r | lax.shift_left(jnp.int32(1), _IDX_BITS - 1 - i)
            f = count(lambda b, idx: jnp.where(b == t, one(idx < cand), 0))
            return jnp.where(f < need, cand, r)
        r = lax.fori_loop(0, _IDX_BITS, tie_body, jnp.zeros((1, TQ), jnp.int32))
        mstar_scr[...] = r + 1

    mstar = mstar_scr[...]

    def sel_mask(b, idx):
        tie_ok = jnp.where(b == t, one(idx < mstar), 0)
        sel = jnp.where(b == INT_MIN, 0, jnp.where(b > t, 1, tie_ok))
        return jnp.where(sel > 0, 0.0, NEG_INF)

    selm_scr[...] = sel_mask(keym_scr[...], krow_m)
    _loop(0, n_vis, lambda t: sel_scr.__setitem__(t, sel_mask(keys_scr[t], LANES + t * TQ + krow)))

    c = DSA_HEAD_DIM ** -0.5 * LOG2E
    n_far = jnp.maximum(n_full - 1, 0)
    use_mb = jnp.where(is_meta | (j == 0), 1.0, 0.0)
    mb_idx = jnp.where(is_meta, _B_META_META, _B_META_FIRST)
    for p in range(DSA_HEADS // 2):
        qp = qb_ref[:, p * LANES:(p + 1) * LANES]
        qs = [jnp.where(low, qp, jnp.zeros_like(qp)), jnp.where(low, jnp.zeros_like(qp), qp)]
        cols = slice(p * LANES, (p + 1) * LANES)
        vrows = [slice((2 * p + hh) * DSA_HEAD_DIM, (2 * p + hh + 1) * DSA_HEAD_DIM) for hh in range(2)]

        ms = []
        for hh in range(2):
            z = (_dot_t(kbm_ref[:, cols], qs[hh]) * c + bias_ref[2 * p + hh, mb_idx][:LANES, :] * use_mb
                 + selm_scr[...])
            sm_scr[hh] = z
            ms.append(_colmax(z))

        def score_tiles(t, ms, n, near, qs=qs, cols=cols, p=p):
            k = kb_ref[pl.ds(pl.multiple_of(t * TQ, TQ), n * TQ), cols]
            out = []
            for hh in range(2):
                zz = _dot_t(k, qs[hh]) * c
                m = ms[hh]
                for i in range(n):
                    z = zz[i * TQ:(i + 1) * TQ] + sel_scr[t + i]
                    if near is not None:
                        z = z + bias_ref[2 * p + hh, near]
                    s_scr[hh, t + i] = z
                    m = jnp.maximum(m, _colmax(z))
                out.append(m)
            return tuple(out)

        ms = _fori_chunks(0, n_far, lambda t, m, n: score_tiles(t, m, n, None), tuple(ms))
        ms = lax.fori_loop(n_far, n_full, lambda t, m: score_tiles(t, m, 1, _B_PREV), ms)
        ms = lax.fori_loop(n_full, n_vis, lambda t, m: score_tiles(t, m, 1, _B_DIAG), ms)

        st = []
        for hh in range(2):
            st += list(_pv(vbtm_ref[0, vrows[hh], 0:LANES], sm_scr[hh], ms[hh],
                           jnp.zeros((1, TQ), F32), jnp.zeros((DSA_HEAD_DIM, TQ), F32)))

        def pv_tile(t, st, ms=ms, vrows=vrows):
            out = []
            for hh in range(2):
                out += list(_pv(vbt_ref[t, vrows[hh], :], s_scr[hh, t], ms[hh], st[2 * hh], st[2 * hh + 1]))
            return tuple(out)

        st = _fori_each(0, n_vis, pv_tile, tuple(st))
        ot = jnp.concatenate([st[1] / st[0], st[3] / st[2]], axis=0)
        o_ref[:, cols] = ot.T.astype(BF16)


def _dsa_call(qi, wi, qb, ki, kb, vbt, bias, B, S, k_sel):
    nq = S // TQ
    ng = B * nq + 1
    W = DSA_HEADS * DSA_HEAD_DIM
    bidx = lambda g: jnp.minimum(g // nq, B - 1)
    mrow = B * S // LANES
    qrow = lambda w: pl.BlockSpec((TQ, w), lambda _, g: (g, 0))
    kv = lambda w: pl.BlockSpec((S, w), lambda _, g: (bidx(g), 0))
    kvm = lambda w: pl.BlockSpec((LANES, w), lambda _, g: (mrow, 0))
    return pl.pallas_call(
        functools.partial(_dsa_kernel, nq=nq, k_sel=k_sel),
        grid=(1, ng),
        in_specs=[qrow(W), qrow(LANES), qrow(W), kv(LANES), kv(W),
                  pl.BlockSpec((nq, W, TQ), lambda _, g: (bidx(g), 0, 0)),
                  kvm(LANES), kvm(W),
                  pl.BlockSpec((1, W, TQ), lambda _, g: (ng - 1, 0, 0)),
                  pl.BlockSpec(bias.shape, lambda _, g: (0, 0, 0, 0))],
        out_specs=qrow(W),
        out_shape=jax.ShapeDtypeStruct((ng * TQ, W), BF16),
        scratch_shapes=[
            pltpu.VMEM((nq, TQ, TQ), jnp.int32),
            pltpu.VMEM((LANES, TQ), jnp.int32),
            pltpu.VMEM((nq, TQ, TQ), F32),
            pltpu.VMEM((LANES, TQ), F32),
            pltpu.VMEM((2, nq, TQ, TQ), F32),
            pltpu.VMEM((2, LANES, TQ), F32),
            pltpu.VMEM((IDX_HEADS, TQ, LANES), BF16),
            pltpu.VMEM((1, TQ), jnp.int32),
        ],
        compiler_params=_params(("arbitrary", "arbitrary")),
        name="dsa",
    )(qi, wi, qb, ki, kb, vbt, ki, kb, vbt, bias)


def _t5_bucket(rel):
    nb = REL_BUCKETS // 2
    max_exact = nb // 2
    n = jnp.abs(rel)
    large = max_exact + (jnp.log(jnp.maximum(n, 1).astype(F32) / max_exact)
                         / math.log(REL_MAX_DIST / max_exact) * (nb - max_exact)).astype(jnp.int32)
    large = jnp.minimum(large, nb - 1)
    return jnp.where(rel > 0, nb, 0) + jnp.where(n < max_exact, n, large)


def _bias_kernel(bucket_ref, rb_ref, o_ref, *, far_bucket):
    h = pl.program_id(0)
    far = rb_ref[far_bucket, h]
    for tile in range(bucket_ref.shape[0]):
        bkt = bucket_ref[tile]
        acc = jnp.zeros(bkt.shape, F32)
        for b in range(REL_BUCKETS):
            acc = jnp.where(bkt == b, rb_ref[b, h], acc)
        o_ref[0, tile] = (acc - far) * LOG2E


def _bias_tiles(rel_bias):
    k = jnp.arange(TQ, dtype=jnp.int32)[:, None]
    q = jnp.arange(TQ, dtype=jnp.int32)[None, :]
    rels = jnp.stack([k - q, k - q - TQ, k - (q + N_META), k - q])
    far_bucket = REL_BUCKETS // 2 - 1
    return pl.pallas_call(
        functools.partial(_bias_kernel, far_bucket=far_bucket),
        grid=(DSA_HEADS,),
        in_specs=[pl.BlockSpec((4, TQ, TQ), lambda h: (0, 0, 0)),
                  pl.BlockSpec(memory_space=pltpu.SMEM)],
        out_specs=pl.BlockSpec((1, 4, TQ, TQ), lambda h: (h, 0, 0, 0)),
        out_shape=jax.ShapeDtypeStruct((DSA_HEADS, 4, TQ, TQ), F32),
        compiler_params=_params(("parallel",)),
        name="bias_tiles",
    )(_t5_bucket(rels), rel_bias)


_MERGE_ROWS = 128


def _merge_kernel(x_ref, meta_ref, oa_ref, ob_ref, ga_ref, gb_ref, wa_ref, wb_ref, wo_ref, h1_ref):
    is_meta = pl.program_id(0) == pl.num_programs(0) - 1
    for rows in (slice(s, s + _MERGE_ROWS) for s in range(0, x_ref.shape[0], _MERGE_ROWS)):
        h = jnp.where(is_meta, meta_ref[rows, :], x_ref[rows, :])
        y = (jax.nn.sigmoid(ga_ref[rows, :]) * _dot(oa_ref[rows, :], wa_ref[...])
             + jax.nn.sigmoid(gb_ref[rows, :]) * _dot(ob_ref[rows, :], wb_ref[...]))
        h1_ref[rows, :] = h + _dot(y.astype(BF16), wo_ref[...])


def _merge_call(x2d, meta_tile, oa, ob, ga, gb, wa, wb, wo):
    BS, D = x2d.shape
    nt = BS // TQ + 1
    row = lambda w: pl.BlockSpec((TQ, w), lambda i: (i, 0))
    full = lambda a: pl.BlockSpec(a.shape, lambda i: (0, 0))
    return pl.pallas_call(
        _merge_kernel,
        grid=(nt,),
        in_specs=[pl.BlockSpec((TQ, D), lambda i: (jnp.minimum(i, nt - 2), 0)), full(meta_tile),
                  row(oa.shape[1]), row(ob.shape[1]), row(D), row(D), full(wa), full(wb), full(wo)],
        out_specs=row(D),
        out_shape=jax.ShapeDtypeStruct((nt * TQ, D), F32),
        compiler_params=_params(("parallel",)),
        name="merge",
    )(x2d, meta_tile, oa, ob, ga, gb, wa, wb, wo)


_ROUTER_ROWS = 40
TE = 256
_ISSUE_UNROLL = 8


def _route_rows(xn, wrt, brt):
    nr = _ROUTER_ROWS
    tm = xn.shape[0]
    lt = lax.dot_general(wrt, xn, (((1,), (1,)), ((), ())), preferred_element_type=F32,
                         precision=lax.Precision.HIGHEST) + brt
    row = lax.broadcasted_iota(jnp.int32, (nr, tm), 0)
    ninf = -jnp.inf
    cmax = lambda a: jnp.max(a, axis=0, keepdims=True)
    cmin = lambda a: jnp.min(a, axis=0, keepdims=True)
    gl = jnp.where((row >= N_EXPERTS) & (row < N_EXPERTS + N_GROUPS), lt, ninf)
    gmax = cmax(gl)
    gsel = cmin(jnp.where(gl == gmax, row, nr)) - N_EXPERTS
    p_group = 1.0 / jnp.sum(jnp.exp(gl - gmax), axis=0, keepdims=True)
    lo = gsel * EXPERTS_PER_GROUP
    el = jnp.where((row >= lo) & (row < lo + EXPERTS_PER_GROUP), lt, ninf)
    m1 = cmax(el)
    i1 = cmin(jnp.where(el == m1, row, nr))
    el2 = jnp.where(row == i1, ninf, el)
    m2 = cmax(el2)
    i2 = cmin(jnp.where(el2 == m2, row, nr))
    e2 = jnp.exp(m2 - m1)
    return i1, i2, p_group / (1.0 + e2), p_group * e2 / (1.0 + e2)


def _rank_kernel(h1_ref, gf_ref, wrt_ref, brt_ref, route_ref, gate_ref, cnt_ref, run_scr):
    @pl.when(pl.program_id(0) == 0)
    def _():
        run_scr[...] = jnp.zeros_like(run_scr)

    tm = h1_ref.shape[0]
    xn = _rms(h1_ref[...], gf_ref[...])
    i1, i2, w1, w2 = _route_rows(xn, wrt_ref[...], brt_ref[...])
    row = lax.broadcasted_iota(jnp.int32, (N_EXPERTS, tm), 0)
    o1 = jnp.where(row == i1, 1.0, 0.0)
    o2 = jnp.where(row == i2, 1.0, 0.0)
    a = lax.broadcasted_iota(jnp.int32, (tm, tm), 0)
    b = lax.broadcasted_iota(jnp.int32, (tm, tm), 1)
    before = jnp.where(a < b, 1.0, 0.0).astype(BF16)
    p1 = _dot(o1.astype(BF16), before)
    p2 = _dot(o2.astype(BF16), before)
    run = run_scr[:, 0:1]
    c1 = jnp.sum(o1, axis=1, keepdims=True)
    c2 = jnp.sum(o2, axis=1, keepdims=True)
    r1 = jnp.sum(o1 * (run + p1), axis=0, keepdims=True)
    r2 = jnp.sum(o2 * (run + c1 + p2), axis=0, keepdims=True)
    new_run = run + c1 + c2
    run_scr[...] = jnp.broadcast_to(new_run, run_scr.shape)
    cnt_ref[...] = jnp.broadcast_to(new_run, cnt_ref.shape).astype(jnp.int32)
    z = jnp.zeros((4, tm), jnp.int32)
    route_ref[...] = jnp.concatenate([i1, r1.astype(jnp.int32), i2, r2.astype(jnp.int32), z], axis=0)
    gt = jnp.concatenate([w1, w2, jnp.zeros((LANES - 2, tm), F32)], axis=0)
    gate_ref[...] = gt.T


def _rank_call(h1, gf, wrt, brt):
    R, D = h1.shape
    full = lambda a: pl.BlockSpec(a.shape, lambda i: (0, 0))
    return pl.pallas_call(
        _rank_kernel,
        grid=(R // TQ,),
        in_specs=[pl.BlockSpec((TQ, D), lambda i: (i, 0)), full(gf), full(wrt), full(brt)],
        out_specs=[pl.BlockSpec((8, TQ), lambda i: (0, i)), pl.BlockSpec((TQ, LANES), lambda i: (i, 0)),
                   pl.BlockSpec((N_EXPERTS, LANES), lambda i: (0, 0))],
        out_shape=[jax.ShapeDtypeStruct((8, R), jnp.int32), jax.ShapeDtypeStruct((R, LANES), F32),
                   jax.ShapeDtypeStruct((N_EXPERTS, LANES), jnp.int32)],
        scratch_shapes=[pltpu.VMEM((N_EXPERTS, LANES), F32)],
        compiler_params=_params(("arbitrary",)),
        name="route_rank",
    )(h1, gf, wrt, brt)


def _dispatch_kernel(pos_ref, h1_ref, gf_ref, xs_in, xs_hbm, buf, sem, *, n_rows):
    del xs_in
    i = pl.program_id(0)
    nt = pl.num_programs(0)
    slot = i % 2
    tm = h1_ref.shape[0]

    def wait_slot(s):
        for _ in range(2):
            pltpu.make_async_copy(buf.at[s], xs_hbm.at[pl.ds(0, tm), :], sem.at[s]).wait()

    @pl.when(i >= 2)
    def _():
        wait_slot(slot)

    buf[slot] = _rms(h1_ref[...], gf_ref[...])

    def issue(u, c):
        for v in range(_ISSUE_UNROLL):
            r = u * _ISSUE_UNROLL + v
            src = buf.at[slot, pl.ds(r, 1), :]
            for k in range(2):
                p = pos_ref[k * n_rows + i * tm + r]
                pltpu.make_async_copy(src, xs_hbm.at[pl.ds(p, 1), :], sem.at[slot]).start()
        return c

    lax.fori_loop(0, tm // _ISSUE_UNROLL, issue, 0)

    @pl.when(i == nt - 1)
    def _():
        wait_slot(slot)

        @pl.when(nt >= 2)
        def _():
            wait_slot(1 - slot)


def _dispatch_call(pos, h1, gf, n_slots):
    R, D = h1.shape
    return pl.pallas_call(
        functools.partial(_dispatch_kernel, n_rows=R),
        grid_spec=pltpu.PrefetchScalarGridSpec(
            num_scalar_prefetch=1, grid=(R // TQ,),
            in_specs=[pl.BlockSpec((TQ, D), lambda i, pos: (i, 0)),
                      pl.BlockSpec(gf.shape, lambda i, pos: (0, 0)),
                      pl.BlockSpec(memory_space=pl.ANY)],
            out_specs=pl.BlockSpec(memory_space=pl.ANY),
            scratch_shapes=[pltpu.VMEM((2, TQ, D), F32), pltpu.SemaphoreType.DMA((2,))]),
        out_shape=jax.ShapeDtypeStruct((n_slots, D), F32),
        input_output_aliases={3: 0},
        compiler_params=_params(("arbitrary",)),
        name="moe_dispatch",
    )(pos, h1, gf, jnp.zeros((n_slots, D), F32))


def _ffn_kernel(be_ref, nu_ref, x_ref, wg_ref, wu_ref, wd_ref, y_ref):
    i = pl.program_id(0)

    @pl.when(i < nu_ref[0])
    def _():
        x = x_ref[...].astype(BF16)
        a = _dot(x, wg_ref[0].astype(BF16))
        u = _dot(x, wu_ref[0].astype(BF16))
        hmid = (a * jax.nn.sigmoid(a) * u).astype(BF16)
        y_ref[...] = _dot(hmid, wd_ref[0].astype(BF16))

    @pl.when(i >= nu_ref[0])
    def _():
        y_ref[...] = jnp.zeros_like(y_ref)


def _ffn_call(blk_expert, n_used, xs, wg, wu, wd):
    NS, D = xs.shape
    wspec = lambda shp: pl.BlockSpec((1,) + shp, lambda i, be, nu: (be[i], 0, 0))
    return pl.pallas_call(
        _ffn_kernel,
        grid_spec=pltpu.PrefetchScalarGridSpec(
            num_scalar_prefetch=2, grid=(NS // TE,),
            in_specs=[pl.BlockSpec((TE, D), lambda i, be, nu: (i, 0)),
                      wspec((D, D_EXPERT)), wspec((D, D_EXPERT)), wspec((D_EXPERT, D))],
            out_specs=pl.BlockSpec((TE, D), lambda i, be, nu: (i, 0))),
        out_shape=jax.ShapeDtypeStruct((NS, D), F32),
        compiler_params=_params(("arbitrary",)),
        name="moe_ffn",
    )(blk_expert, n_used, xs, wg, wu, wd)


def _combine_kernel(pos_ref, h1_ref, gate_ref, gfin_ref, ys_hbm, o_ref, buf, sem, *, n_rows):
    i = pl.program_id(0)
    nt = pl.num_programs(0)
    slot = i % 2
    tm = h1_ref.shape[0]

    def fetch(tile, s):
        def issue(u, c):
            for v in range(_ISSUE_UNROLL):
                r = u * _ISSUE_UNROLL + v
                for k in range(2):
                    p = pos_ref[k * n_rows + tile * tm + r]
                    pltpu.make_async_copy(ys_hbm.at[pl.ds(p, 1), :], buf.at[s, k, pl.ds(r, 1), :],
                                          sem.at[s]).start()
            return c
        lax.fori_loop(0, tm // _ISSUE_UNROLL, issue, 0)

    @pl.when(i == 0)
    def _():
        fetch(0, 0)

    @pl.when(i + 1 < nt)
    def _():
        fetch(i + 1, 1 - slot)

    for k in range(2):
        pltpu.make_async_copy(ys_hbm.at[pl.ds(0, tm), :], buf.at[slot, k], sem.at[slot]).wait()

    g = gate_ref[...]
    ffn = g[:, 0:1] * buf[slot, 0] + g[:, 1:2] * buf[slot, 1]

    @pl.when(i < nt - 1)
    def _():
        o_ref[...] = _rms(h1_ref[...] + ffn, gfin_ref[...])


def _combine_call(pos, h1, gates, gfin, ys, n_out):
    R, D = h1.shape
    nt = R // TQ
    return pl.pallas_call(
        functools.partial(_combine_kernel, n_rows=R),
        grid_spec=pltpu.PrefetchScalarGridSpec(
            num_scalar_prefetch=1, grid=(nt,),
            in_specs=[pl.BlockSpec((TQ, D), lambda i, pos: (i, 0)),
                      pl.BlockSpec((TQ, LANES), lambda i, pos: (i, 0)),
                      pl.BlockSpec(gfin.shape, lambda i, pos: (0, 0)),
                      pl.BlockSpec(memory_space=pl.ANY)],
            out_specs=pl.BlockSpec((TQ, D), lambda i, pos: (jnp.minimum(i, nt - 2), 0)),
            scratch_shapes=[pltpu.VMEM((2, 2, TQ, D), F32), pltpu.SemaphoreType.DMA((2,))]),
        out_shape=jax.ShapeDtypeStruct((n_out, D), F32),
        compiler_params=_params(("arbitrary",)),
        name="moe_combine",
    )(pos, h1, gates, gfin, ys)


def _sparse_moe(h1, gf, wrt, brt, wg, wu, wd, gfin, n_out):
    R, D = h1.shape
    route, gates, cnt = _rank_call(h1, gf, wrt, brt)
    counts = cnt[:, 0]
    padded = (counts + TE - 1) // TE * TE
    pad_end = jnp.cumsum(padded)
    offs = pad_end - padded
    nb = -(-2 * R // TE) + N_EXPERTS
    eids = jnp.arange(N_EXPERTS, dtype=jnp.int32)
    slot_of = lambda e, r: r + jnp.sum(jnp.where(e[:, None] == eids[None, :], offs[None, :], 0), axis=1)
    pos = jnp.concatenate([slot_of(route[0], route[1]), slot_of(route[2], route[3])]).astype(jnp.int32)
    blk_start = jnp.arange(nb, dtype=jnp.int32) * TE
    blk_expert = jnp.minimum(jnp.sum((blk_start[:, None] >= pad_end[None, :]).astype(jnp.int32), axis=1),
                             N_EXPERTS - 1).astype(jnp.int32)
    n_used = (pad_end[-1] // TE).astype(jnp.int32).reshape(1)
    xs = _dispatch_call(pos, h1, gf, nb * TE)
    ys = _ffn_call(blk_expert, n_used, xs, wg, wu, wd)
    return _combine_call(pos, h1, gates, gfin, ys, n_out)


def kernel(x, meta_tokens, norm_mix_g, w_in, mla_cq_norm_g, mla_ckv_norm_g, w_mla_uq, w_mla_ukv,
           w_branch_a, w_branch_b, w_out, rel_bias, norm_ffn_g, w_router_group, b_router_group,
           w_router_expert, b_router_expert, w_exp_gate, w_exp_up, w_exp_down, norm_final_g):
    B, S, D = x.shape
    assert S % TQ == 0 and norm_mix_g.shape[0] == 1
    k_sel = min(K_SEL_MAX, S // 4)
    x2d = x.reshape(B * S, D)
    meta_tile = jnp.concatenate([meta_tokens.astype(x.dtype), jnp.zeros((TQ - N_META, D), x.dtype)], axis=0)

    pos = np.concatenate([N_META + np.arange(S), np.minimum(np.arange(TQ), N_META)]).astype(np.int32)
    ct, st = _rope_tables(jnp.asarray(pos))

    qm, km, vmt, qb, kb, vbt, qi, ki, wi, ga, gb = _proj_call(
        x2d, meta_tile, norm_mix_g.reshape(1, D), _pack_w1(w_in[0]), mla_cq_norm_g.reshape(1, -1),
        mla_ckv_norm_g.reshape(1, -1), _pack_wuq(w_mla_uq[0]), _pack_wukv(w_mla_ukv[0]), ct, st, S)

    o_a = _mla_call(qm, km, vmt, B, S)
    o_b = _dsa_call(qi, wi, qb, ki, kb, vbt, _bias_tiles(rel_bias), B, S, k_sel)

    pad_r = _ROUTER_ROWS - N_EXPERTS - N_GROUPS
    w_r = jnp.concatenate([w_router_expert[0].T, w_router_group[0].T, jnp.zeros((pad_r, D), F32)], axis=0)
    b_r = jnp.concatenate([b_router_expert[0], b_router_group[0],
                           jnp.zeros((pad_r,), F32)]).reshape(_ROUTER_ROWS, 1)
    h1 = _merge_call(x2d, meta_tile, o_a, o_b, ga, gb, w_branch_a[0].astype(BF16),
                     w_branch_b[0].astype(BF16), w_out[0].astype(BF16))
    out = _sparse_moe(h1, norm_ffn_g.reshape(1, D), w_r, b_r, w_exp_gate[0], w_exp_up[0], w_exp_down[0],
                      norm_final_g.reshape(1, D), B * S)
    return out.reshape(B, S, D)
```

```python
import functools
import math

import numpy as np
import jax
import jax.numpy as jnp
from jax import lax
from jax.experimental import pallas as pl
from jax.experimental.pallas import tpu as pltpu

CHUNK = 64
N_META = 16
NEG_INF = -1e30
RMS_EPS = 1e-6
ROPE_BASE = 10000.0
MLA_HEADS = 8
MLA_Q_LORA = 256
MLA_KV_LORA = 128
MLA_NOPE = 64
MLA_ROPE = 32
MLA_V = 64
DSA_HEADS = 8
DSA_HEAD_DIM = 64
IDX_HEADS = 8
IDX_DIM = 64
K_SEL_MAX = 256
REL_BUCKETS = 32
REL_MAX_DIST = 128
N_GROUPS = 4
EXPERTS_PER_GROUP = 8
N_EXPERTS = N_GROUPS * EXPERTS_PER_GROUP
D_EXPERT = 256

LANES = 128
TQ = 256
HEAD_PAD = 128
VMEM_LIMIT = 56 * 1024 * 1024
INT_MIN = -2 ** 31
LOG2E = math.log2(math.e)

F32 = jnp.float32
BF16 = jnp.bfloat16


def _params(sem):
    return pltpu.CompilerParams(dimension_semantics=sem, vmem_limit_bytes=VMEM_LIMIT)


def _rms(x, g):
    return x * lax.rsqrt(jnp.mean(x * x, axis=-1, keepdims=True) + RMS_EPS) * g


def _dot(a, b):
    return jnp.dot(a, b, preferred_element_type=F32)


def _dot_t(a, b):
    return lax.dot_general(a, b, (((1,), (1,)), ((), ())), preferred_element_type=F32)


_C_CQ = (0, 256)
_C_CKV = (256, 384)
_C_KRX = (384, 512)
_C_KRY = (512, 640)
_C_QB = (640, 1152)
_C_KB = (1152, 1664)
_C_VB = (1664, 2176)
_C_QI = (2176, 2688)
_C_KI = (2688, 2816)
_C_WI = (2816, 2944)
_C_GA = (2944, 3968)
_C_GB = (3968, 4992)
_W1_COLS = 4992


def _proj_kernel(x_ref, meta_ref, g_ref, w1_ref, gcq_ref, gckv_ref, wuq_ref, wukv_ref, ct_ref, st_ref,
                 qm_ref, km_ref, vm_ref, qb_ref, kb_ref, vb_ref, qi_ref, ki_ref, wi_ref,
                 ga_ref, gb_ref):
    is_meta = pl.program_id(0) == pl.num_programs(0) - 1
    h = jnp.where(is_meta, meta_ref[...], x_ref[...])
    xb = _rms(h, g_ref[...]).astype(BF16)

    def seg(c):
        return _dot(xb, w1_ref[:, c[0]:c[1]])

    ct = ct_ref[...]
    st = st_ref[...]
    nq = _rms(seg(_C_CQ), gcq_ref[...]).astype(BF16)
    qa = _dot(nq, wuq_ref[...])
    half = MLA_HEADS * HEAD_PAD
    for hd in range(MLA_HEADS):
        lo, hi = hd * HEAD_PAD, (hd + 1) * HEAD_PAD
        qm_ref[:, lo:hi] = (qa[:, lo:hi] * ct + qa[:, half + lo:half + hi] * st).astype(BF16)
    nkv = _rms(seg(_C_CKV), gckv_ref[...]).astype(BF16)
    kva = _dot(nkv, wukv_ref[...])
    kr = seg(_C_KRX) * ct + seg(_C_KRY) * st
    for hd in range(MLA_HEADS):
        lo, hi = hd * HEAD_PAD, (hd + 1) * HEAD_PAD
        km_ref[:, lo:hi] = (kva[:, lo:hi] + kr).astype(BF16)
    vm_ref[0] = kva[:, half:].T.astype(BF16)
    qb_ref[...] = seg(_C_QB).astype(BF16)
    kb_ref[...] = seg(_C_KB).astype(BF16)
    vb_ref[0] = seg(_C_VB).T.astype(BF16)
    qi_ref[...] = seg(_C_QI).astype(BF16)
    ki_ref[...] = seg(_C_KI).astype(BF16)
    wi_ref[...] = seg(_C_WI)
    ga_ref[...] = seg(_C_GA)
    gb_ref[...] = seg(_C_GB)


def _pack_w1(w_in):
    D = w_in.shape[0]
    offs = np.cumsum([0, MLA_Q_LORA, MLA_KV_LORA, MLA_ROPE, 512, 512, 512, 512, IDX_DIM, IDX_HEADS, D, D])
    c = lambda i: w_in[:, offs[i]:offs[i + 1]]
    z = lambda n: jnp.zeros((D, n), w_in.dtype)
    k_r = c(2)
    hr = MLA_ROPE // 2
    k_r_sw = jnp.concatenate([k_r[:, hr:], k_r[:, :hr]], axis=1)
    cols = [
        c(0), c(1),
        z(MLA_NOPE), k_r, z(LANES - MLA_NOPE - MLA_ROPE),
        z(MLA_NOPE), k_r_sw, z(LANES - MLA_NOPE - MLA_ROPE),
        c(3), c(4), c(5),
        c(6) * (IDX_DIM ** -0.5),
        c(7), c(7),
        c(8), z(LANES - IDX_HEADS),
        c(9), c(10),
    ]
    w1 = jnp.concatenate(cols, axis=1)
    assert w1.shape[1] == _W1_COLS
    return w1.astype(BF16)


def _pack_wuq(w_uq):
    R = w_uq.shape[0]
    w = w_uq.reshape(R, MLA_HEADS, MLA_NOPE + MLA_ROPE)
    nope, rope = w[..., :MLA_NOPE], w[..., MLA_NOPE:]
    hr = MLA_ROPE // 2
    rope_sw = jnp.concatenate([rope[..., hr:], rope[..., :hr]], axis=-1)
    zt = jnp.zeros((R, MLA_HEADS, HEAD_PAD - MLA_NOPE - MLA_ROPE), w.dtype)
    a = jnp.concatenate([nope, rope, zt], axis=-1).reshape(R, MLA_HEADS * HEAD_PAD)
    b = jnp.concatenate([jnp.zeros_like(nope), rope_sw, zt], axis=-1).reshape(R, MLA_HEADS * HEAD_PAD)
    return jnp.concatenate([a, b], axis=1).astype(BF16)


def _pack_wukv(w_ukv):
    R = w_ukv.shape[0]
    w = w_ukv.reshape(R, MLA_HEADS, MLA_NOPE + MLA_V)
    k = jnp.concatenate([w[..., :MLA_NOPE], jnp.zeros((R, MLA_HEADS, HEAD_PAD - MLA_NOPE), w.dtype)], axis=-1)
    v = w[..., MLA_NOPE:]
    return jnp.concatenate([k.reshape(R, -1), v.reshape(R, -1)], axis=1).astype(BF16)


def _rope_tables(pos):
    half = MLA_ROPE // 2
    inv = ROPE_BASE ** (-jnp.arange(half, dtype=F32) / half)
    ang = pos.astype(F32)[:, None] * inv[None, :]
    cos, sin = jnp.cos(ang), jnp.sin(ang)
    n = pos.shape[0]
    tail = jnp.zeros((n, HEAD_PAD - MLA_NOPE - MLA_ROPE), F32)
    ct = jnp.concatenate([jnp.ones((n, MLA_NOPE), F32), cos, cos, tail], axis=1)
    st = jnp.concatenate([jnp.zeros((n, MLA_NOPE), F32), -sin, sin, tail], axis=1)
    return ct, st


def _proj_call(x2d, meta_tile, g, w1, gcq, gckv, wuq, wukv, ct, st, S):
    BS, D = x2d.shape
    nt = BS // TQ + 1
    per_b = S // TQ
    row = lambda w: pl.BlockSpec((TQ, w), lambda i: (i, 0))
    full = lambda a: pl.BlockSpec(a.shape, lambda i: (0, 0))
    tab = pl.BlockSpec((TQ, HEAD_PAD), lambda i: (jnp.where(i == nt - 1, per_b, i % per_b), 0))
    widths = [(MLA_HEADS * HEAD_PAD, BF16), (MLA_HEADS * HEAD_PAD, BF16), (None, BF16),
              (512, BF16), (512, BF16), (None, BF16), (512, BF16), (LANES, BF16), (LANES, F32),
              (D, F32), (D, F32)]
    vt_rows = MLA_HEADS * MLA_V
    vt_spec = pl.BlockSpec((1, vt_rows, TQ), lambda i: (i, 0, 0))
    vt_shape = jax.ShapeDtypeStruct((nt, vt_rows, TQ), BF16)
    return pl.pallas_call(
        _proj_kernel,
        grid=(nt,),
        in_specs=[pl.BlockSpec((TQ, D), lambda i: (jnp.minimum(i, nt - 2), 0)), full(meta_tile),
                  full(g), full(w1), full(gcq), full(gckv), full(wuq), full(wukv), tab, tab],
        out_specs=[vt_spec if w is None else row(w) for w, _ in widths],
        out_shape=[vt_shape if w is None else jax.ShapeDtypeStruct((nt * TQ, w), dt) for w, dt in widths],
        compiler_params=_params(("parallel",)),
        name="proj",
    )(x2d, meta_tile, g, w1, gcq, gckv, wuq, wukv, ct, st)


def _colmax(z):
    return jnp.max(z, axis=0, keepdims=True)


def _pv(vt, z, m, l, acc):
    p = jnp.exp2(z - m)
    return l + jnp.sum(p, axis=0, keepdims=True), acc + _dot(vt, p.astype(BF16))


def _tile_ids(nq):
    g = pl.program_id(1)
    is_meta = g == pl.num_programs(1) - 1
    return is_meta, jnp.where(is_meta, 0, g % nq)


def _chunk_mask():
    key = lax.broadcasted_iota(jnp.int32, (TQ, TQ), 0)
    qry = lax.broadcasted_iota(jnp.int32, (TQ, TQ), 1)
    shift = CHUNK.bit_length() - 1
    return (key >> shift) <= (qry >> shift)


_CHUNKS = (4, 2, 1)


def _fori_chunks(lo, hi, body_n, carry):
    for n in _CHUNKS:
        cnt = (hi - lo) // n
        carry = lax.fori_loop(0, cnt, lambda u, c, lo=lo, n=n: body_n(lo + n * u, c, n), carry)
        lo = lo + cnt * n
    return carry


def _fori_each(lo, hi, body, carry):
    def body_n(t, c, n):
        for i in range(n):
            c = body(t + i, c)
        return c
    return _fori_chunks(lo, hi, body_n, carry)


def _loop(lo, hi, fn):
    _fori_each(lo, hi, lambda t, c: (fn(t), c)[1], 0)


def _mla_kernel(q_ref, k_ref, vt_ref, km_ref, vtm_ref, o_ref, s_scr, sm_scr, *, nq):
    is_meta, j = _tile_ids(nq)
    n_full = jnp.where(is_meta, 0, j)
    n_vis = jnp.where(is_meta, 0, j + 1)
    c = (MLA_NOPE + MLA_ROPE) ** -0.5 * LOG2E
    qs = [q_ref[:, hh * HEAD_PAD:(hh + 1) * HEAD_PAD] for hh in range(2)]
    hcols = [slice(hh * HEAD_PAD, (hh + 1) * HEAD_PAD) for hh in range(2)]
    vrows = [slice(hh * MLA_V, (hh + 1) * MLA_V) for hh in range(2)]
    meta_ok = lax.broadcasted_iota(jnp.int32, (LANES, TQ), 0) < N_META
    diag_ok = _chunk_mask()

    ms = []
    for hh in range(2):
        z = jnp.where(meta_ok, _dot_t(km_ref[:, hcols[hh]], qs[hh]) * c, NEG_INF)
        sm_scr[hh] = z
        ms.append(_colmax(z))

    def score_tiles(t, ms, n, masked):
        ks = pl.multiple_of(t * TQ, TQ)
        out = []
        for hh in range(2):
            z = _dot_t(k_ref[pl.ds(ks, n * TQ), hcols[hh]], qs[hh]) * c
            if masked:
                z = jnp.where(diag_ok, z, NEG_INF)
            for i in range(n):
                s_scr[hh, t + i] = z[i * TQ:(i + 1) * TQ]
            out.append(jnp.maximum(ms[hh], _colmax(z)))
        return tuple(out)

    ms = _fori_chunks(0, n_full, lambda t, m, n: score_tiles(t, m, n, False), tuple(ms))
    ms = lax.fori_loop(n_full, n_vis, lambda t, m: score_tiles(t, m, 1, True), ms)

    st = []
    for hh in range(2):
        st += list(_pv(vtm_ref[0, vrows[hh], 0:LANES], sm_scr[hh], ms[hh],
                       jnp.zeros((1, TQ), F32), jnp.zeros((MLA_V, TQ), F32)))

    def pv_tile(t, st):
        out = []
        for hh in range(2):
            out += list(_pv(vt_ref[t, vrows[hh], :], s_scr[hh, t], ms[hh], st[2 * hh], st[2 * hh + 1]))
        return tuple(out)

    st = _fori_each(0, n_vis, pv_tile, tuple(st))
    ot = jnp.concatenate([st[1] / st[0], st[3] / st[2]], axis=0)
    o_ref[...] = ot.T.astype(BF16)


def _mla_call(qm, km, vmt, B, S):
    nq = S // TQ
    ng = B * nq + 1
    pairs = MLA_HEADS // 2
    bidx = lambda g: jnp.minimum(g // nq, B - 1)
    mrow = B * S // LANES
    return pl.pallas_call(
        functools.partial(_mla_kernel, nq=nq),
        grid=(pairs, ng),
        in_specs=[
            pl.BlockSpec((TQ, 2 * HEAD_PAD), lambda p, g: (g, p)),
            pl.BlockSpec((S, 2 * HEAD_PAD), lambda p, g: (bidx(g), p)),
            pl.BlockSpec((nq, 2 * MLA_V, TQ), lambda p, g: (bidx(g), p, 0)),
            pl.BlockSpec((LANES, 2 * HEAD_PAD), lambda p, g: (mrow, p)),
            pl.BlockSpec((1, 2 * MLA_V, TQ), lambda p, g: (ng - 1, p, 0)),
        ],
        out_specs=pl.BlockSpec((TQ, 2 * MLA_V), lambda p, g: (g, p)),
        out_shape=jax.ShapeDtypeStruct((ng * TQ, MLA_HEADS * MLA_V), BF16),
        scratch_shapes=[pltpu.VMEM((2, nq, TQ, TQ), F32),
                        pltpu.VMEM((2, LANES, TQ), F32)],
        compiler_params=_params(("parallel", "arbitrary")),
        name="mla_attn",
    )(qm, km, vmt, km, vmt)


_IDX_BITS = 13
_B_DIAG, _B_PREV, _B_META_FIRST, _B_META_META = 0, 1, 2, 3


def _dsa_kernel(qi_ref, wi_ref, qb_ref, ki_ref, kb_ref, vbt_ref, kim_ref, kbm_ref, vbtm_ref, bias_ref, o_ref,
                keys_scr, keym_scr, sel_scr, selm_scr, s_scr, sm_scr, qs_scr, mstar_scr,
                dig_scr, dm_scr, *, nq, k_sel):
    is_meta, j = _tile_ids(nq)
    n_full = jnp.where(is_meta, 0, j)
    n_vis = jnp.where(is_meta, 0, j + 1)
    lane = lax.broadcasted_iota(jnp.int32, (TQ, LANES), 1)
    low = lane < DSA_HEAD_DIM
    krow_m = lax.broadcasted_iota(jnp.int32, (LANES, TQ), 0)
    krow = lax.broadcasted_iota(jnp.int32, (TQ, TQ), 0)
    meta_ok = krow_m < N_META
    diag_ok = _chunk_mask()

    for p in range(IDX_HEADS // 2):
        qp = qi_ref[:, p * LANES:(p + 1) * LANES]
        qs_scr[2 * p] = jnp.where(low, qp, jnp.zeros_like(qp))
        qs_scr[2 * p + 1] = jnp.where(low, jnp.zeros_like(qp), qp)
    wt = (wi_ref[...] * (IDX_HEADS ** -0.5)).T

    def sort_key(score):
        bits = pltpu.bitcast(score + 0.0, jnp.int32)
        return bits ^ ((bits >> 31) & 0x7FFFFFFF)

    def score_of(kt):
        score = jnp.zeros((kt.shape[0], TQ), F32)
        for h in range(IDX_HEADS):
            score = score + wt[h:h + 1, :] * jnp.maximum(_dot_t(kt, qs_scr[h]), 0.0)
        return sort_key(score)

    keym_scr[...] = jnp.where(meta_ok, score_of(kim_ref[...]), INT_MIN)

    def score_tiles(t, n, masked):
        key = score_of(ki_ref[pl.ds(pl.multiple_of(t * TQ, TQ), n * TQ), :])
        if masked:
            key = jnp.where(diag_ok, key, INT_MIN)
        for i in range(n):
            keys_scr[t + i] = key[i * TQ:(i + 1) * TQ]

    _fori_chunks(0, n_full, lambda t, c, n: (score_tiles(t, n, False), c)[1], 0)
    lax.fori_loop(n_full, n_vis, lambda t, c: (score_tiles(t, 1, True), c)[1], 0)

    one = lambda m: jnp.where(m, 1, 0)
    fold = lambda a: jnp.sum(a.reshape(-1, 8, TQ), axis=0)

    def count(pred):
        def cbody(t, acc):
            return acc + fold(pred(keys_scr[t], LANES + t * TQ + krow))
        acc = _fori_each(0, n_vis, cbody, fold(pred(keym_scr[...], krow_m)))
        return jnp.sum(acc, axis=0, keepdims=True)

    def fold16(m):
        parts = [m[r:r + 16] for r in range(0, m.shape[0], 16)]
        while len(parts) > 1:
            parts = [a + b for a, b in zip(parts[::2], parts[1::2])]
        return parts[0].astype(F32)

    one16 = lambda m: jnp.where(m, jnp.ones((), BF16), jnp.zeros((), BF16))

    def count16(meta_scr, tile_scr, pred):
        acc = _fori_each(0, n_vis, lambda t, a: a + fold16(one16(pred(tile_scr[t]))),
                         fold16(one16(pred(meta_scr[...]))))
        return jnp.sum(acc, axis=0, keepdims=True)

    def digit_search(prefix, shift, need):
        top, last = shift == 24, shift == 0

        def digits(key):
            d = ((key >> shift) + 128 if top else (key >> shift) & 0xFF).astype(F32)
            if not top:
                d = jnp.where((key >> (shift + 8)) == prefix, d, -1.0)
            return d.astype(BF16)

        dm_scr[...] = digits(keym_scr[...])
        _loop(0, n_vis, lambda t: dig_scr.__setitem__(t, digits(keys_scr[t])))

        def d_body(i, c):
            d, above, at_least = c
            cand = d + lax.shift_left(jnp.int32(1), 7 - i)
            cb = cand.astype(F32).astype(BF16)
            cnt = count16(dm_scr, dig_scr, lambda v: v >= cb)
            ok = cnt >= need
            return jnp.where(ok, cand, d), jnp.where(ok, above, cnt), jnp.where(ok, cnt, at_least)

        zero = jnp.zeros((1, TQ), F32)
        all_cands = count16(dm_scr, dig_scr, lambda v: v >= jnp.zeros((), BF16)) if last else zero
        return lax.fori_loop(0, 8, d_body, (jnp.zeros((1, TQ), jnp.int32), zero, all_cands))

    t = jnp.zeros((1, TQ), jnp.int32)
    need_d = jnp.full((1, TQ), float(k_sel), F32)
    for shift in (24, 16, 8, 0):
        d, above, at_least = digit_search(t, shift, need_d)
        t = (d - 128) if shift == 24 else (t << 8) | d
        need_d = need_d - above
    need = need_d.astype(jnp.int32)
    cnt_ge = k_sel - need + (at_least - above).astype(jnp.int32)

    mstar_scr[...] = jnp.full((1, TQ), 2 ** _IDX_BITS, jnp.int32)
    tied = jnp.max(jnp.where((cnt_ge > k_sel) & (t != INT_MIN), 1, 0))

    @pl.when(tied > 0)
    def _():
        def tie_body(i, r):
            cand = r | lax.shift_left(jnp.int32(1), _IDX_BITS - 1 - i)
            f = count(lambda b, idx: jnp.where(b == t, one(idx < cand), 0))
            return jnp.where(f < need, cand, r)
        r = lax.fori_loop(0, _IDX_BITS, tie_body, jnp.zeros((1, TQ), jnp.int32))
        mstar_scr[...] = r + 1

    mstar = mstar_scr[...]

    def sel_mask(b, idx):
        tie_ok = jnp.where(b == t, one(idx < mstar), 0)
        sel = jnp.where(b == INT_MIN, 0, jnp.where(b > t, 1, tie_ok))
        return jnp.where(sel > 0, 0.0, NEG_INF)

    selm_scr[...] = sel_mask(keym_scr[...], krow_m)
    _loop(0, n_vis, lambda t: sel_scr.__setitem__(t, sel_mask(keys_scr[t], LANES + t * TQ + krow)))

    c = DSA_HEAD_DIM ** -0.5 * LOG2E
    n_far = jnp.maximum(n_full - 1, 0)
    use_mb = jnp.where(is_meta | (j == 0), 1.0, 0.0)
    mb_idx = jnp.where(is_meta, _B_META_META, _B_META_FIRST)
    for p in range(DSA_HEADS // 2):
        qp = qb_ref[:, p * LANES:(p + 1) * LANES]
        qs = [jnp.where(low, qp, jnp.zeros_like(qp)), jnp.where(low, jnp.zeros_like(qp), qp)]
        cols = slice(p * LANES, (p + 1) * LANES)
        vrows = [slice((2 * p + hh) * DSA_HEAD_DIM, (2 * p + hh + 1) * DSA_HEAD_DIM) for hh in range(2)]

        ms = []
        for hh in range(2):
            z = (_dot_t(kbm_ref[:, cols], qs[hh]) * c + bias_ref[2 * p + hh, mb_idx][:LANES, :] * use_mb
                 + selm_scr[...])
            sm_scr[hh] = z
            ms.append(_colmax(z))

        def score_tiles(t, ms, n, near, qs=qs, cols=cols, p=p):
            k = kb_ref[pl.ds(pl.multiple_of(t * TQ, TQ), n * TQ), cols]
            out = []
            for hh in range(2):
                zz = _dot_t(k, qs[hh]) * c
                m = ms[hh]
                for i in range(n):
                    z = zz[i * TQ:(i + 1) * TQ] + sel_scr[t + i]
                    if near is not None:
                        z = z + bias_ref[2 * p + hh, near]
                    s_scr[hh, t + i] = z
                    m = jnp.maximum(m, _colmax(z))
                out.append(m)
            return tuple(out)

        ms = _fori_chunks(0, n_far, lambda t, m, n: score_tiles(t, m, n, None), tuple(ms))
        ms = lax.fori_loop(n_far, n_full, lambda t, m: score_tiles(t, m, 1, _B_PREV), ms)
        ms = lax.fori_loop(n_full, n_vis, lambda t, m: score_tiles(t, m, 1, _B_DIAG), ms)

        st = []
        for hh in range(2):
            st += list(_pv(vbtm_ref[0, vrows[hh], 0:LANES], sm_scr[hh], ms[hh],
                           jnp.zeros((1, TQ), F32), jnp.zeros((DSA_HEAD_DIM, TQ), F32)))

        def pv_tile(t, st, ms=ms, vrows=vrows):
            out = []
            for hh in range(2):
                out += list(_pv(vbt_ref[t, vrows[hh], :], s_scr[hh, t], ms[hh], st[2 * hh], st[2 * hh + 1]))
            return tuple(out)

        st = _fori_each(0, n_vis, pv_tile, tuple(st))
        ot = jnp.concatenate([st[1] / st[0], st[3] / st[2]], axis=0)
        o_ref[:, cols] = ot.T.astype(BF16)


def _dsa_call(qi, wi, qb, ki, kb, vbt, bias, B, S, k_sel):
    nq = S // TQ
    ng = B * nq + 1
    W = DSA_HEADS * DSA_HEAD_DIM
    bidx = lambda g: jnp.minimum(g // nq, B - 1)
    mrow = B * S // LANES
    qrow = lambda w: pl.BlockSpec((TQ, w), lambda _, g: (g, 0))
    kv = lambda w: pl.BlockSpec((S, w), lambda _, g: (bidx(g), 0))
    kvm = lambda w: pl.BlockSpec((LANES, w), lambda _, g: (mrow, 0))
    return pl.pallas_call(
        functools.partial(_dsa_kernel, nq=nq, k_sel=k_sel),
        grid=(1, ng),
        in_specs=[qrow(W), qrow(LANES), qrow(W), kv(LANES), kv(W),
                  pl.BlockSpec((nq, W, TQ), lambda _, g: (bidx(g), 0, 0)),
                  kvm(LANES), kvm(W),
                  pl.BlockSpec((1, W, TQ), lambda _, g: (ng - 1, 0, 0)),
                  pl.BlockSpec(bias.shape, lambda _, g: (0, 0, 0, 0))],
        out_specs=qrow(W),
        out_shape=jax.ShapeDtypeStruct((ng * TQ, W), BF16),
        scratch_shapes=[
            pltpu.VMEM((nq, TQ, TQ), jnp.int32),
            pltpu.VMEM((LANES, TQ), jnp.int32),
            pltpu.VMEM((nq, TQ, TQ), F32),
            pltpu.VMEM((LANES, TQ), F32),
            pltpu.VMEM((2, nq, TQ, TQ), F32),
            pltpu.VMEM((2, LANES, TQ), F32),
            pltpu.VMEM((IDX_HEADS, TQ, LANES), BF16),
            pltpu.VMEM((1, TQ), jnp.int32),
            pltpu.VMEM((nq, TQ, TQ), BF16),
            pltpu.VMEM((LANES, TQ), BF16),
        ],
        compiler_params=_params(("arbitrary", "arbitrary")),
        name="dsa",
    )(qi, wi, qb, ki, kb, vbt, ki, kb, vbt, bias)


def _t5_bucket(rel):
    nb = REL_BUCKETS // 2
    max_exact = nb // 2
    n = jnp.abs(rel)
    large = max_exact + (jnp.log(jnp.maximum(n, 1).astype(F32) / max_exact)
                         / math.log(REL_MAX_DIST / max_exact) * (nb - max_exact)).astype(jnp.int32)
    large = jnp.minimum(large, nb - 1)
    return jnp.where(rel > 0, nb, 0) + jnp.where(n < max_exact, n, large)


def _bias_kernel(bucket_ref, rb_ref, o_ref, *, far_bucket):
    h = pl.program_id(0)
    far = rb_ref[far_bucket, h]
    for tile in range(bucket_ref.shape[0]):
        bkt = bucket_ref[tile]
        acc = jnp.zeros(bkt.shape, F32)
        for b in range(REL_BUCKETS):
            acc = jnp.where(bkt == b, rb_ref[b, h], acc)
        o_ref[0, tile] = (acc - far) * LOG2E


def _bias_tiles(rel_bias):
    k = jnp.arange(TQ, dtype=jnp.int32)[:, None]
    q = jnp.arange(TQ, dtype=jnp.int32)[None, :]
    rels = jnp.stack([k - q, k - q - TQ, k - (q + N_META), k - q])
    far_bucket = REL_BUCKETS // 2 - 1
    return pl.pallas_call(
        functools.partial(_bias_kernel, far_bucket=far_bucket),
        grid=(DSA_HEADS,),
        in_specs=[pl.BlockSpec((4, TQ, TQ), lambda h: (0, 0, 0)),
                  pl.BlockSpec(memory_space=pltpu.SMEM)],
        out_specs=pl.BlockSpec((1, 4, TQ, TQ), lambda h: (h, 0, 0, 0)),
        out_shape=jax.ShapeDtypeStruct((DSA_HEADS, 4, TQ, TQ), F32),
        compiler_params=_params(("parallel",)),
        name="bias_tiles",
    )(_t5_bucket(rels), rel_bias)


_MERGE_ROWS = 128


def _merge_kernel(x_ref, meta_ref, oa_ref, ob_ref, ga_ref, gb_ref, wa_ref, wb_ref, wo_ref, h1_ref):
    is_meta = pl.program_id(0) == pl.num_programs(0) - 1
    for rows in (slice(s, s + _MERGE_ROWS) for s in range(0, x_ref.shape[0], _MERGE_ROWS)):
        h = jnp.where(is_meta, meta_ref[rows, :], x_ref[rows, :])
        y = (jax.nn.sigmoid(ga_ref[rows, :]) * _dot(oa_ref[rows, :], wa_ref[...])
             + jax.nn.sigmoid(gb_ref[rows, :]) * _dot(ob_ref[rows, :], wb_ref[...]))
        h1_ref[rows, :] = h + _dot(y.astype(BF16), wo_ref[...])


def _merge_call(x2d, meta_tile, oa, ob, ga, gb, wa, wb, wo):
    BS, D = x2d.shape
    nt = BS // TQ + 1
    row = lambda w: pl.BlockSpec((TQ, w), lambda i: (i, 0))
    full = lambda a: pl.BlockSpec(a.shape, lambda i: (0, 0))
    return pl.pallas_call(
        _merge_kernel,
        grid=(nt,),
        in_specs=[pl.BlockSpec((TQ, D), lambda i: (jnp.minimum(i, nt - 2), 0)), full(meta_tile),
                  row(oa.shape[1]), row(ob.shape[1]), row(D), row(D), full(wa), full(wb), full(wo)],
        out_specs=row(D),
        out_shape=jax.ShapeDtypeStruct((nt * TQ, D), F32),
        compiler_params=_params(("parallel",)),
        name="merge",
    )(x2d, meta_tile, oa, ob, ga, gb, wa, wb, wo)


_ROUTER_ROWS = 40
TE = 256
_ISSUE_UNROLL = 8


def _route_rows(xn, wrt, brt):
    nr = _ROUTER_ROWS
    tm = xn.shape[0]
    lt = lax.dot_general(wrt, xn, (((1,), (1,)), ((), ())), preferred_element_type=F32,
                         precision=lax.Precision.HIGHEST) + brt
    row = lax.broadcasted_iota(jnp.int32, (nr, tm), 0)
    ninf = -jnp.inf
    cmax = lambda a: jnp.max(a, axis=0, keepdims=True)
    cmin = lambda a: jnp.min(a, axis=0, keepdims=True)
    gl = jnp.where((row >= N_EXPERTS) & (row < N_EXPERTS + N_GROUPS), lt, ninf)
    gmax = cmax(gl)
    gsel = cmin(jnp.where(gl == gmax, row, nr)) - N_EXPERTS
    p_group = 1.0 / jnp.sum(jnp.exp(gl - gmax), axis=0, keepdims=True)
    lo = gsel * EXPERTS_PER_GROUP
    el = jnp.where((row >= lo) & (row < lo + EXPERTS_PER_GROUP), lt, ninf)
    m1 = cmax(el)
    i1 = cmin(jnp.where(el == m1, row, nr))
    el2 = jnp.where(row == i1, ninf, el)
    m2 = cmax(el2)
    i2 = cmin(jnp.where(el2 == m2, row, nr))
    e2 = jnp.exp(m2 - m1)
    return i1, i2, p_group / (1.0 + e2), p_group * e2 / (1.0 + e2)


def _rank_kernel(h1_ref, gf_ref, wrt_ref, brt_ref, route_ref, gate_ref, cnt_ref, run_scr):
    @pl.when(pl.program_id(0) == 0)
    def _():
        run_scr[...] = jnp.zeros_like(run_scr)

    tm = h1_ref.shape[0]
    xn = _rms(h1_ref[...], gf_ref[...])
    i1, i2, w1, w2 = _route_rows(xn, wrt_ref[...], brt_ref[...])
    row = lax.broadcasted_iota(jnp.int32, (N_EXPERTS, tm), 0)
    o1 = jnp.where(row == i1, 1.0, 0.0)
    o2 = jnp.where(row == i2, 1.0, 0.0)
    a = lax.broadcasted_iota(jnp.int32, (tm, tm), 0)
    b = lax.broadcasted_iota(jnp.int32, (tm, tm), 1)
    before = jnp.where(a < b, 1.0, 0.0).astype(BF16)
    p1 = _dot(o1.astype(BF16), before)
    p2 = _dot(o2.astype(BF16), before)
    run = run_scr[:, 0:1]
    c1 = jnp.sum(o1, axis=1, keepdims=True)
    c2 = jnp.sum(o2, axis=1, keepdims=True)
    r1 = jnp.sum(o1 * (run + p1), axis=0, keepdims=True)
    r2 = jnp.sum(o2 * (run + c1 + p2), axis=0, keepdims=True)
    new_run = run + c1 + c2
    run_scr[...] = jnp.broadcast_to(new_run, run_scr.shape)
    cnt_ref[...] = jnp.broadcast_to(new_run, cnt_ref.shape).astype(jnp.int32)
    z = jnp.zeros((4, tm), jnp.int32)
    route_ref[...] = jnp.concatenate([i1, r1.astype(jnp.int32), i2, r2.astype(jnp.int32), z], axis=0)
    gt = jnp.concatenate([w1, w2, jnp.zeros((LANES - 2, tm), F32)], axis=0)
    gate_ref[...] = gt.T


def _rank_call(h1, gf, wrt, brt):
    R, D = h1.shape
    full = lambda a: pl.BlockSpec(a.shape, lambda i: (0, 0))
    return pl.pallas_call(
        _rank_kernel,
        grid=(R // TQ,),
        in_specs=[pl.BlockSpec((TQ, D), lambda i: (i, 0)), full(gf), full(wrt), full(brt)],
        out_specs=[pl.BlockSpec((8, TQ), lambda i: (0, i)), pl.BlockSpec((TQ, LANES), lambda i: (i, 0)),
                   pl.BlockSpec((N_EXPERTS, LANES), lambda i: (0, 0))],
        out_shape=[jax.ShapeDtypeStruct((8, R), jnp.int32), jax.ShapeDtypeStruct((R, LANES), F32),
                   jax.ShapeDtypeStruct((N_EXPERTS, LANES), jnp.int32)],
        scratch_shapes=[pltpu.VMEM((N_EXPERTS, LANES), F32)],
        compiler_params=_params(("arbitrary",)),
        name="route_rank",
    )(h1, gf, wrt, brt)


def _dispatch_kernel(pos_ref, h1_ref, gf_ref, xs_in, xs_hbm, buf, sem, *, n_rows):
    del xs_in
    i = pl.program_id(0)
    nt = pl.num_programs(0)
    slot = i % 2
    tm = h1_ref.shape[0]

    def wait_slot(s):
        for _ in range(2):
            pltpu.make_async_copy(buf.at[s], xs_hbm.at[pl.ds(0, tm), :], sem.at[s]).wait()

    @pl.when(i >= 2)
    def _():
        wait_slot(slot)

    buf[slot] = _rms(h1_ref[...], gf_ref[...])

    def issue(u, c):
        for v in range(_ISSUE_UNROLL):
            r = u * _ISSUE_UNROLL + v
            src = buf.at[slot, pl.ds(r, 1), :]
            for k in range(2):
                p = pos_ref[k * n_rows + i * tm + r]
                pltpu.make_async_copy(src, xs_hbm.at[pl.ds(p, 1), :], sem.at[slot]).start()
        return c

    lax.fori_loop(0, tm // _ISSUE_UNROLL, issue, 0)

    @pl.when(i == nt - 1)
    def _():
        wait_slot(slot)

        @pl.when(nt >= 2)
        def _():
            wait_slot(1 - slot)


def _dispatch_call(pos, h1, gf, n_slots):
    R, D = h1.shape
    return pl.pallas_call(
        functools.partial(_dispatch_kernel, n_rows=R),
        grid_spec=pltpu.PrefetchScalarGridSpec(
            num_scalar_prefetch=1, grid=(R // TQ,),
            in_specs=[pl.BlockSpec((TQ, D), lambda i, pos: (i, 0)),
                      pl.BlockSpec(gf.shape, lambda i, pos: (0, 0)),
                      pl.BlockSpec(memory_space=pl.ANY)],
            out_specs=pl.BlockSpec(memory_space=pl.ANY),
            scratch_shapes=[pltpu.VMEM((2, TQ, D), F32), pltpu.SemaphoreType.DMA((2,))]),
        out_shape=jax.ShapeDtypeStruct((n_slots, D), F32),
        input_output_aliases={3: 0},
        compiler_params=_params(("arbitrary",)),
        name="moe_dispatch",
    )(pos, h1, gf, jnp.zeros((n_slots, D), F32))


def _ffn_kernel(be_ref, nu_ref, x_ref, wg_ref, wu_ref, wd_ref, y_ref):
    i = pl.program_id(0)

    @pl.when(i < nu_ref[0])
    def _():
        x = x_ref[...].astype(BF16)
        a = _dot(x, wg_ref[0].astype(BF16))
        u = _dot(x, wu_ref[0].astype(BF16))
        hmid = (a * jax.nn.sigmoid(a) * u).astype(BF16)
        y_ref[...] = _dot(hmid, wd_ref[0].astype(BF16))

    @pl.when(i >= nu_ref[0])
    def _():
        y_ref[...] = jnp.zeros_like(y_ref)


def _ffn_call(blk_expert, n_used, xs, wg, wu, wd):
    NS, D = xs.shape
    wspec = lambda shp: pl.BlockSpec((1,) + shp, lambda i, be, nu: (be[i], 0, 0))
    return pl.pallas_call(
        _ffn_kernel,
        grid_spec=pltpu.PrefetchScalarGridSpec(
            num_scalar_prefetch=2, grid=(NS // TE,),
            in_specs=[pl.BlockSpec((TE, D), lambda i, be, nu: (i, 0)),
                      wspec((D, D_EXPERT)), wspec((D, D_EXPERT)), wspec((D_EXPERT, D))],
            out_specs=pl.BlockSpec((TE, D), lambda i, be, nu: (i, 0))),
        out_shape=jax.ShapeDtypeStruct((NS, D), F32),
        compiler_params=_params(("arbitrary",)),
        name="moe_ffn",
    )(blk_expert, n_used, xs, wg, wu, wd)


def _combine_kernel(pos_ref, h1_ref, gate_ref, gfin_ref, ys_hbm, o_ref, buf, sem, *, n_rows):
    i = pl.program_id(0)
    nt = pl.num_programs(0)
    slot = i % 2
    tm = h1_ref.shape[0]

    def fetch(tile, s):
        def issue(u, c):
            for v in range(_ISSUE_UNROLL):
                r = u * _ISSUE_UNROLL + v
                for k in range(2):
                    p = pos_ref[k * n_rows + tile * tm + r]
                    pltpu.make_async_copy(ys_hbm.at[pl.ds(p, 1), :], buf.at[s, k, pl.ds(r, 1), :],
                                          sem.at[s]).start()
            return c
        lax.fori_loop(0, tm // _ISSUE_UNROLL, issue, 0)

    @pl.when(i == 0)
    def _():
        fetch(0, 0)

    @pl.when(i + 1 < nt)
    def _():
        fetch(i + 1, 1 - slot)

    for k in range(2):
        pltpu.make_async_copy(ys_hbm.at[pl.ds(0, tm), :], buf.at[slot, k], sem.at[slot]).wait()

    g = gate_ref[...]
    ffn = g[:, 0:1] * buf[slot, 0] + g[:, 1:2] * buf[slot, 1]

    @pl.when(i < nt - 1)
    def _():
        o_ref[...] = _rms(h1_ref[...] + ffn, gfin_ref[...])


def _combine_call(pos, h1, gates, gfin, ys, n_out):
    R, D = h1.shape
    nt = R // TQ
    return pl.pallas_call(
        functools.partial(_combine_kernel, n_rows=R),
        grid_spec=pltpu.PrefetchScalarGridSpec(
            num_scalar_prefetch=1, grid=(nt,),
            in_specs=[pl.BlockSpec((TQ, D), lambda i, pos: (i, 0)),
                      pl.BlockSpec((TQ, LANES), lambda i, pos: (i, 0)),
                      pl.BlockSpec(gfin.shape, lambda i, pos: (0, 0)),
                      pl.BlockSpec(memory_space=pl.ANY)],
            out_specs=pl.BlockSpec((TQ, D), lambda i, pos: (jnp.minimum(i, nt - 2), 0)),
            scratch_shapes=[pltpu.VMEM((2, 2, TQ, D), F32), pltpu.SemaphoreType.DMA((2,))]),
        out_shape=jax.ShapeDtypeStruct((n_out, D), F32),
        compiler_params=_params(("arbitrary",)),
        name="moe_combine",
    )(pos, h1, gates, gfin, ys)


def _sparse_moe(h1, gf, wrt, brt, wg, wu, wd, gfin, n_out):
    R, D = h1.shape
    route, gates, cnt = _rank_call(h1, gf, wrt, brt)
    counts = cnt[:, 0]
    padded = (counts + TE - 1) // TE * TE
    pad_end = jnp.cumsum(padded)
    offs = pad_end - padded
    nb = -(-2 * R // TE) + N_EXPERTS
    eids = jnp.arange(N_EXPERTS, dtype=jnp.int32)
    slot_of = lambda e, r: r + jnp.sum(jnp.where(e[:, None] == eids[None, :], offs[None, :], 0), axis=1)
    pos = jnp.concatenate([slot_of(route[0], route[1]), slot_of(route[2], route[3])]).astype(jnp.int32)
    blk_start = jnp.arange(nb, dtype=jnp.int32) * TE
    blk_expert = jnp.minimum(jnp.sum((blk_start[:, None] >= pad_end[None, :]).astype(jnp.int32), axis=1),
                             N_EXPERTS - 1).astype(jnp.int32)
    n_used = (pad_end[-1] // TE).astype(jnp.int32).reshape(1)
    xs = _dispatch_call(pos, h1, gf, nb * TE)
    ys = _ffn_call(blk_expert, n_used, xs, wg, wu, wd)
    return _combine_call(pos, h1, gates, gfin, ys, n_out)


def kernel(x, meta_tokens, norm_mix_g, w_in, mla_cq_norm_g, mla_ckv_norm_g, w_mla_uq, w_mla_ukv,
           w_branch_a, w_branch_b, w_out, rel_bias, norm_ffn_g, w_router_group, b_router_group,
           w_router_expert, b_router_expert, w_exp_gate, w_exp_up, w_exp_down, norm_final_g):
    B, S, D = x.shape
    assert S % TQ == 0 and norm_mix_g.shape[0] == 1
    k_sel = min(K_SEL_MAX, S // 4)
    x2d = x.reshape(B * S, D)
    meta_tile = jnp.concatenate([meta_tokens.astype(x.dtype), jnp.zeros((TQ - N_META, D), x.dtype)], axis=0)

    pos = np.concatenate([N_META + np.arange(S), np.minimum(np.arange(TQ), N_META)]).astype(np.int32)
    ct, st = _rope_tables(jnp.asarray(pos))

    qm, km, vmt, qb, kb, vbt, qi, ki, wi, ga, gb = _proj_call(
        x2d, meta_tile, norm_mix_g.reshape(1, D), _pack_w1(w_in[0]), mla_cq_norm_g.reshape(1, -1),
        mla_ckv_norm_g.reshape(1, -1), _pack_wuq(w_mla_uq[0]), _pack_wukv(w_mla_ukv[0]), ct, st, S)

    o_a = _mla_call(qm, km, vmt, B, S)
    o_b = _dsa_call(qi, wi, qb, ki, kb, vbt, _bias_tiles(rel_bias), B, S, k_sel)

    pad_r = _ROUTER_ROWS - N_EXPERTS - N_GROUPS
    w_r = jnp.concatenate([w_router_expert[0].T, w_router_group[0].T, jnp.zeros((pad_r, D), F32)], axis=0)
    b_r = jnp.concatenate([b_router_expert[0], b_router_group[0],
                           jnp.zeros((pad_r,), F32)]).reshape(_ROUTER_ROWS, 1)
    h1 = _merge_call(x2d, meta_tile, o_a, o_b, ga, gb, w_branch_a[0].astype(BF16),
                     w_branch_b[0].astype(BF16), w_out[0].astype(BF16))
    out = _sparse_moe(h1, norm_ffn_g.reshape(1, D), w_r, b_r, w_exp_gate[0], w_exp_up[0], w_exp_down[0],
                      norm_final_g.reshape(1, D), B * S)
    return out.reshape(B, S, D)
```

```python
import functools
import math

import numpy as np
import jax
import jax.numpy as jnp
from jax import lax
from jax.experimental import pallas as pl
from jax.experimental.pallas import tpu as pltpu

CHUNK = 64
N_META = 16
NEG_INF = -1e30
RMS_EPS = 1e-6
ROPE_BASE = 10000.0
MLA_HEADS = 8
MLA_Q_LORA = 256
MLA_KV_LORA = 128
MLA_NOPE = 64
MLA_ROPE = 32
MLA_V = 64
DSA_HEADS = 8
DSA_HEAD_DIM = 64
IDX_HEADS = 8
IDX_DIM = 64
K_SEL_MAX = 256
REL_BUCKETS = 32
REL_MAX_DIST = 128
N_GROUPS = 4
EXPERTS_PER_GROUP = 8
N_EXPERTS = N_GROUPS * EXPERTS_PER_GROUP
D_EXPERT = 256

LANES = 128
TQ = 256
HEAD_PAD = 128
VMEM_LIMIT = 56 * 1024 * 1024
INT_MIN = -2 ** 31
LOG2E = math.log2(math.e)

F32 = jnp.float32
BF16 = jnp.bfloat16


def _params(sem):
    return pltpu.CompilerParams(dimension_semantics=sem, vmem_limit_bytes=VMEM_LIMIT)


def _rms(x, g):
    return x * lax.rsqrt(jnp.mean(x * x, axis=-1, keepdims=True) + RMS_EPS) * g


def _dot(a, b):
    return jnp.dot(a, b, preferred_element_type=F32)


def _dot_t(a, b):
    return lax.dot_general(a, b, (((1,), (1,)), ((), ())), preferred_element_type=F32)


_C_CQ = (0, 256)
_C_CKV = (256, 384)
_C_KRX = (384, 512)
_C_KRY = (512, 640)
_C_QB = (640, 1152)
_C_KB = (1152, 1664)
_C_VB = (1664, 2176)
_C_QI = (2176, 2688)
_C_KI = (2688, 2816)
_C_WI = (2816, 2944)
_C_GA = (2944, 3968)
_C_GB = (3968, 4992)
_W1_COLS = 4992


def _proj_kernel(x_ref, meta_ref, g_ref, w1_ref, gcq_ref, gckv_ref, wuq_ref, wukv_ref, ct_ref, st_ref,
                 qm_ref, km_ref, vm_ref, qb_ref, kb_ref, vb_ref, qi_ref, ki_ref, wi_ref,
                 ga_ref, gb_ref):
    is_meta = pl.program_id(0) == pl.num_programs(0) - 1
    h = jnp.where(is_meta, meta_ref[...], x_ref[...])
    xb = _rms(h, g_ref[...]).astype(BF16)

    def seg(c):
        return _dot(xb, w1_ref[:, c[0]:c[1]])

    ct = ct_ref[...]
    st = st_ref[...]
    nq = _rms(seg(_C_CQ), gcq_ref[...]).astype(BF16)
    qa = _dot(nq, wuq_ref[...])
    half = MLA_HEADS * HEAD_PAD
    for hd in range(MLA_HEADS):
        lo, hi = hd * HEAD_PAD, (hd + 1) * HEAD_PAD
        qm_ref[:, lo:hi] = (qa[:, lo:hi] * ct + qa[:, half + lo:half + hi] * st).astype(BF16)
    nkv = _rms(seg(_C_CKV), gckv_ref[...]).astype(BF16)
    kva = _dot(nkv, wukv_ref[...])
    kr = seg(_C_KRX) * ct + seg(_C_KRY) * st
    for hd in range(MLA_HEADS):
        lo, hi = hd * HEAD_PAD, (hd + 1) * HEAD_PAD
        km_ref[:, lo:hi] = (kva[:, lo:hi] + kr).astype(BF16)
    vm_ref[0] = kva[:, half:].T.astype(BF16)
    qb_ref[...] = seg(_C_QB).astype(BF16)
    kb_ref[...] = seg(_C_KB).astype(BF16)
    vb_ref[0] = seg(_C_VB).T.astype(BF16)
    qi_ref[...] = seg(_C_QI).astype(BF16)
    ki_ref[...] = seg(_C_KI).astype(BF16)
    wi_ref[...] = seg(_C_WI)
    ga_ref[...] = seg(_C_GA)
    gb_ref[...] = seg(_C_GB)


def _pack_w1(w_in):
    D = w_in.shape[0]
    offs = np.cumsum([0, MLA_Q_LORA, MLA_KV_LORA, MLA_ROPE, 512, 512, 512, 512, IDX_DIM, IDX_HEADS, D, D])
    c = lambda i: w_in[:, offs[i]:offs[i + 1]]
    z = lambda n: jnp.zeros((D, n), w_in.dtype)
    k_r = c(2)
    hr = MLA_ROPE // 2
    k_r_sw = jnp.concatenate([k_r[:, hr:], k_r[:, :hr]], axis=1)
    cols = [
        c(0), c(1),
        z(MLA_NOPE), k_r, z(LANES - MLA_NOPE - MLA_ROPE),
        z(MLA_NOPE), k_r_sw, z(LANES - MLA_NOPE - MLA_ROPE),
        c(3), c(4), c(5),
        c(6) * (IDX_DIM ** -0.5),
        c(7), c(7),
        c(8), z(LANES - IDX_HEADS),
        c(9), c(10),
    ]
    w1 = jnp.concatenate(cols, axis=1)
    assert w1.shape[1] == _W1_COLS
    return w1.astype(BF16)


def _pack_wuq(w_uq):
    R = w_uq.shape[0]
    w = w_uq.reshape(R, MLA_HEADS, MLA_NOPE + MLA_ROPE)
    nope, rope = w[..., :MLA_NOPE], w[..., MLA_NOPE:]
    hr = MLA_ROPE // 2
    rope_sw = jnp.concatenate([rope[..., hr:], rope[..., :hr]], axis=-1)
    zt = jnp.zeros((R, MLA_HEADS, HEAD_PAD - MLA_NOPE - MLA_ROPE), w.dtype)
    a = jnp.concatenate([nope, rope, zt], axis=-1).reshape(R, MLA_HEADS * HEAD_PAD)
    b = jnp.concatenate([jnp.zeros_like(nope), rope_sw, zt], axis=-1).reshape(R, MLA_HEADS * HEAD_PAD)
    return jnp.concatenate([a, b], axis=1).astype(BF16)


def _pack_wukv(w_ukv):
    R = w_ukv.shape[0]
    w = w_ukv.reshape(R, MLA_HEADS, MLA_NOPE + MLA_V)
    k = jnp.concatenate([w[..., :MLA_NOPE], jnp.zeros((R, MLA_HEADS, HEAD_PAD - MLA_NOPE), w.dtype)], axis=-1)
    v = w[..., MLA_NOPE:]
    return jnp.concatenate([k.reshape(R, -1), v.reshape(R, -1)], axis=1).astype(BF16)


def _rope_tables(pos):
    half = MLA_ROPE // 2
    inv = ROPE_BASE ** (-jnp.arange(half, dtype=F32) / half)
    ang = pos.astype(F32)[:, None] * inv[None, :]
    cos, sin = jnp.cos(ang), jnp.sin(ang)
    n = pos.shape[0]
    tail = jnp.zeros((n, HEAD_PAD - MLA_NOPE - MLA_ROPE), F32)
    ct = jnp.concatenate([jnp.ones((n, MLA_NOPE), F32), cos, cos, tail], axis=1)
    st = jnp.concatenate([jnp.zeros((n, MLA_NOPE), F32), -sin, sin, tail], axis=1)
    return ct, st


def _proj_call(x2d, meta_tile, g, w1, gcq, gckv, wuq, wukv, ct, st, S):
    BS, D = x2d.shape
    nt = BS // TQ + 1
    per_b = S // TQ
    row = lambda w: pl.BlockSpec((TQ, w), lambda i: (i, 0))
    full = lambda a: pl.BlockSpec(a.shape, lambda i: (0, 0))
    tab = pl.BlockSpec((TQ, HEAD_PAD), lambda i: (jnp.where(i == nt - 1, per_b, i % per_b), 0))
    widths = [(MLA_HEADS * HEAD_PAD, BF16), (MLA_HEADS * HEAD_PAD, BF16), (None, BF16),
              (512, BF16), (512, BF16), (None, BF16), (512, BF16), (LANES, BF16), (LANES, F32),
              (D, F32), (D, F32)]
    vt_rows = MLA_HEADS * MLA_V
    vt_spec = pl.BlockSpec((1, vt_rows, TQ), lambda i: (i, 0, 0))
    vt_shape = jax.ShapeDtypeStruct((nt, vt_rows, TQ), BF16)
    return pl.pallas_call(
        _proj_kernel,
        grid=(nt,),
        in_specs=[pl.BlockSpec((TQ, D), lambda i: (jnp.minimum(i, nt - 2), 0)), full(meta_tile),
                  full(g), full(w1), full(gcq), full(gckv), full(wuq), full(wukv), tab, tab],
        out_specs=[vt_spec if w is None else row(w) for w, _ in widths],
        out_shape=[vt_shape if w is None else jax.ShapeDtypeStruct((nt * TQ, w), dt) for w, dt in widths],
        compiler_params=_params(("parallel",)),
        name="proj",
    )(x2d, meta_tile, g, w1, gcq, gckv, wuq, wukv, ct, st)


def _colmax(z):
    return jnp.max(z, axis=0, keepdims=True)


def _pv(vt, z, m, l, acc):
    p = jnp.exp2(z - m)
    return l + jnp.sum(p, axis=0, keepdims=True), acc + _dot(vt, p.astype(BF16))


def _tile_ids(nq):
    g = pl.program_id(1)
    is_meta = g == pl.num_programs(1) - 1
    return is_meta, jnp.where(is_meta, 0, g % nq)


def _chunk_mask():
    key = lax.broadcasted_iota(jnp.int32, (TQ, TQ), 0)
    qry = lax.broadcasted_iota(jnp.int32, (TQ, TQ), 1)
    shift = CHUNK.bit_length() - 1
    return (key >> shift) <= (qry >> shift)


_CHUNKS = (4, 2, 1)


def _fori_chunks(lo, hi, body_n, carry):
    for n in _CHUNKS:
        cnt = (hi - lo) // n
        carry = lax.fori_loop(0, cnt, lambda u, c, lo=lo, n=n: body_n(lo + n * u, c, n), carry)
        lo = lo + cnt * n
    return carry


def _fori_each(lo, hi, body, carry):
    def body_n(t, c, n):
        for i in range(n):
            c = body(t + i, c)
        return c
    return _fori_chunks(lo, hi, body_n, carry)


def _loop(lo, hi, fn):
    _fori_each(lo, hi, lambda t, c: (fn(t), c)[1], 0)


def _mla_kernel(q_ref, k_ref, vt_ref, km_ref, vtm_ref, o_ref, s_scr, sm_scr, *, nq):
    is_meta, j = _tile_ids(nq)
    n_full = jnp.where(is_meta, 0, j)
    n_vis = jnp.where(is_meta, 0, j + 1)
    c = (MLA_NOPE + MLA_ROPE) ** -0.5 * LOG2E
    meta_ok = lax.broadcasted_iota(jnp.int32, (LANES, TQ), 0) < N_META
    diag_ok = _chunk_mask()
    heads = lambda p: (2 * p, 2 * p + 1)
    hcol = lambda h: slice(h * HEAD_PAD, (h + 1) * HEAD_PAD)
    vrow = lambda h: slice(h * MLA_V, (h + 1) * MLA_V)

    def score_meta(p):
        ms = []
        for hh, h in enumerate(heads(p)):
            z = jnp.where(meta_ok, _dot_t(km_ref[:, hcol(h)], q_ref[:, hcol(h)]) * c, NEG_INF)
            sm_scr[p % 2, hh] = z
            ms.append(_colmax(z))
        return tuple(ms)

    def score_tiles(p, t, ms, n, masked):
        ks = pl.multiple_of(t * TQ, TQ)
        out = []
        for hh, h in enumerate(heads(p)):
            z = _dot_t(k_ref[pl.ds(ks, n * TQ), hcol(h)], q_ref[:, hcol(h)]) * c
            if masked:
                z = jnp.where(diag_ok, z, NEG_INF)
            for i in range(n):
                s_scr[p % 2, hh, t + i] = z[i * TQ:(i + 1) * TQ]
            out.append(jnp.maximum(ms[hh], _colmax(z)))
        return tuple(out)

    def pv_meta(p, ms):
        st = []
        for hh, h in enumerate(heads(p)):
            st += list(_pv(vtm_ref[0, vrow(h), 0:LANES], sm_scr[p % 2, hh], ms[hh],
                           jnp.zeros((1, TQ), F32), jnp.zeros((MLA_V, TQ), F32)))
        return tuple(st)

    def pv_tiles(p, t, st, ms, n):
        for i in range(n):
            out = []
            for hh, h in enumerate(heads(p)):
                out += list(_pv(vt_ref[t + i, vrow(h), :], s_scr[p % 2, hh, t + i], ms[hh],
                                st[2 * hh], st[2 * hh + 1]))
            st = tuple(out)
        return st

    pairs = MLA_HEADS // 2
    ms = score_meta(0)
    ms = _fori_chunks(0, n_full, lambda t, m, n: score_tiles(0, t, m, n, False), ms)
    ms = lax.fori_loop(n_full, n_vis, lambda t, m: score_tiles(0, t, m, 1, True), ms)
    for p in range(pairs):
        st = pv_meta(p, ms)
        if p + 1 < pairs:
            ms_next = score_meta(p + 1)
            st, ms_next = _fori_chunks(
                0, n_full, lambda t, c, n, p=p, ms=ms: (pv_tiles(p, t, c[0], ms, n),
                                                        score_tiles(p + 1, t, c[1], n, False)), (st, ms_next))
            st, ms_next = lax.fori_loop(
                n_full, n_vis, lambda t, c, p=p, ms=ms: (pv_tiles(p, t, c[0], ms, 1),
                                                         score_tiles(p + 1, t, c[1], 1, True)), (st, ms_next))
        else:
            st = _fori_chunks(0, n_vis, lambda t, s, n, p=p, ms=ms: pv_tiles(p, t, s, ms, n), st)
            ms_next = None
        ot = jnp.concatenate([st[1] / st[0], st[3] / st[2]], axis=0)
        o_ref[:, p * 2 * MLA_V:(p + 1) * 2 * MLA_V] = ot.T.astype(BF16)
        ms = ms_next


def _mla_call(qm, km, vmt, B, S):
    nq = S // TQ
    ng = B * nq + 1
    bidx = lambda g: jnp.minimum(g // nq, B - 1)
    mrow = B * S // LANES
    wk, wv = MLA_HEADS * HEAD_PAD, MLA_HEADS * MLA_V
    return pl.pallas_call(
        functools.partial(_mla_kernel, nq=nq),
        grid=(1, ng),
        in_specs=[
            pl.BlockSpec((TQ, wk), lambda _, g: (g, 0)),
            pl.BlockSpec((S, wk), lambda _, g: (bidx(g), 0)),
            pl.BlockSpec((nq, wv, TQ), lambda _, g: (bidx(g), 0, 0)),
            pl.BlockSpec((LANES, wk), lambda _, g: (mrow, 0)),
            pl.BlockSpec((1, wv, TQ), lambda _, g: (ng - 1, 0, 0)),
        ],
        out_specs=pl.BlockSpec((TQ, wv), lambda _, g: (g, 0)),
        out_shape=jax.ShapeDtypeStruct((ng * TQ, wv), BF16),
        scratch_shapes=[pltpu.VMEM((2, 2, nq, TQ, TQ), F32),
                        pltpu.VMEM((2, 2, LANES, TQ), F32)],
        compiler_params=_params(("arbitrary", "arbitrary")),
        name="mla_attn",
    )(qm, km, vmt, km, vmt)


_IDX_BITS = 13
_B_DIAG, _B_PREV, _B_META_FIRST, _B_META_META = 0, 1, 2, 3


def _dsa_kernel(qi_ref, wi_ref, qb_ref, ki_ref, kb_ref, vbt_ref, kim_ref, kbm_ref, vbtm_ref, bias_ref, o_ref,
                keys_scr, keym_scr, selm_scr, s_scr, sm_scr, qs_scr, mstar_scr,
                dig_scr, dm_scr, *, nq, k_sel):
    is_meta, j = _tile_ids(nq)
    n_full = jnp.where(is_meta, 0, j)
    n_vis = jnp.where(is_meta, 0, j + 1)
    lane = lax.broadcasted_iota(jnp.int32, (TQ, LANES), 1)
    low = lane < DSA_HEAD_DIM
    krow_m = lax.broadcasted_iota(jnp.int32, (LANES, TQ), 0)
    krow = lax.broadcasted_iota(jnp.int32, (TQ, TQ), 0)
    meta_ok = krow_m < N_META
    diag_ok = _chunk_mask()

    for p in range(IDX_HEADS // 2):
        qp = qi_ref[:, p * LANES:(p + 1) * LANES]
        qs_scr[2 * p] = jnp.where(low, qp, jnp.zeros_like(qp))
        qs_scr[2 * p + 1] = jnp.where(low, jnp.zeros_like(qp), qp)
    wt = (wi_ref[...] * (IDX_HEADS ** -0.5)).T

    def sort_key(score):
        bits = pltpu.bitcast(score + 0.0, jnp.int32)
        return bits ^ ((bits >> 31) & 0x7FFFFFFF)

    def score_of(kt):
        score = jnp.zeros((kt.shape[0], TQ), F32)
        for h in range(IDX_HEADS):
            score = score + wt[h:h + 1, :] * jnp.maximum(_dot_t(kt, qs_scr[h]), 0.0)
        return sort_key(score)

    keym_scr[...] = jnp.where(meta_ok, score_of(kim_ref[...]), INT_MIN)

    def score_tiles(t, n, masked):
        key = score_of(ki_ref[pl.ds(pl.multiple_of(t * TQ, TQ), n * TQ), :])
        if masked:
            key = jnp.where(diag_ok, key, INT_MIN)
        for i in range(n):
            keys_scr[t + i] = key[i * TQ:(i + 1) * TQ]

    _fori_chunks(0, n_full, lambda t, c, n: (score_tiles(t, n, False), c)[1], 0)
    lax.fori_loop(n_full, n_vis, lambda t, c: (score_tiles(t, 1, True), c)[1], 0)

    one = lambda m: jnp.where(m, 1, 0)
    fold = lambda a: jnp.sum(a.reshape(-1, 8, TQ), axis=0)

    def count(pred):
        def cbody(t, acc):
            return acc + fold(pred(keys_scr[t], LANES + t * TQ + krow))
        acc = _fori_each(0, n_vis, cbody, fold(pred(keym_scr[...], krow_m)))
        return jnp.sum(acc, axis=0, keepdims=True)

    def fold16(m):
        parts = [m[r:r + 16] for r in range(0, m.shape[0], 16)]
        while len(parts) > 1:
            parts = [a + b for a, b in zip(parts[::2], parts[1::2])]
        return parts[0].astype(F32)

    one16 = lambda m: jnp.where(m, jnp.ones((), BF16), jnp.zeros((), BF16))

    def count16(meta_scr, tile_scr, pred):
        acc = _fori_each(0, n_vis, lambda t, a: a + fold16(one16(pred(tile_scr[t]))),
                         fold16(one16(pred(meta_scr[...]))))
        return jnp.sum(acc, axis=0, keepdims=True)

    def digit_search(prefix, shift, need):
        top, last = shift == 24, shift == 0

        def digits(key):
            d = ((key >> shift) + 128 if top else (key >> shift) & 0xFF).astype(F32)
            if not top:
                d = jnp.where((key >> (shift + 8)) == prefix, d, -1.0)
            return d.astype(BF16)

        dm_scr[...] = digits(keym_scr[...])
        _loop(0, n_vis, lambda t: dig_scr.__setitem__(t, digits(keys_scr[t])))

        def d_body(i, c):
            d, above, at_least = c
            cand = d + lax.shift_left(jnp.int32(1), 7 - i)
            cb = cand.astype(F32).astype(BF16)
            cnt = count16(dm_scr, dig_scr, lambda v: v >= cb)
            ok = cnt >= need
            return jnp.where(ok, cand, d), jnp.where(ok, above, cnt), jnp.where(ok, cnt, at_least)

        zero = jnp.zeros((1, TQ), F32)
        all_cands = count16(dm_scr, dig_scr, lambda v: v >= jnp.zeros((), BF16)) if last else zero
        return lax.fori_loop(0, 8, d_body, (jnp.zeros((1, TQ), jnp.int32), zero, all_cands))

    t = jnp.zeros((1, TQ), jnp.int32)
    need_d = jnp.full((1, TQ), float(k_sel), F32)
    for shift in (24, 16, 8, 0):
        d, above, at_least = digit_search(t, shift, need_d)
        t = (d - 128) if shift == 24 else (t << 8) | d
        need_d = need_d - above
    need = need_d.astype(jnp.int32)
    cnt_ge = k_sel - need + (at_least - above).astype(jnp.int32)

    mstar_scr[...] = jnp.full((1, TQ), 2 ** _IDX_BITS, jnp.int32)
    tied = jnp.max(jnp.where((cnt_ge > k_sel) & (t != INT_MIN), 1, 0))

    @pl.when(tied > 0)
    def _():
        def tie_body(i, r):
            cand = r | lax.shift_left(jnp.int32(1), _IDX_BITS - 1 - i)
            f = count(lambda b, idx: jnp.where(b == t, one(idx < cand), 0))
            return jnp.where(f < need, cand, r)
        r = lax.fori_loop(0, _IDX_BITS, tie_body, jnp.zeros((1, TQ), jnp.int32))
        mstar_scr[...] = r + 1

    mstar = mstar_scr[...]

    def sel_mask(b, idx):
        tie_ok = jnp.where(b == t, one(idx < mstar), 0)
        sel = jnp.where(b == INT_MIN, 0, jnp.where(b > t, 1, tie_ok))
        return jnp.where(sel > 0, 0.0, NEG_INF)

    selm_scr[...] = sel_mask(keym_scr[...], krow_m)
    _loop(0, n_vis, lambda t: keys_scr.__setitem__(
        t, pltpu.bitcast(sel_mask(keys_scr[t], LANES + t * TQ + krow), jnp.int32)))

    c = DSA_HEAD_DIM ** -0.5 * LOG2E
    n_far = jnp.maximum(n_full - 1, 0)
    use_mb = jnp.where(is_meta | (j == 0), 1.0, 0.0)
    mb_idx = jnp.where(is_meta, _B_META_META, _B_META_FIRST)
    pcols = lambda p: slice(p * LANES, (p + 1) * LANES)
    vrow = lambda h: slice(h * DSA_HEAD_DIM, (h + 1) * DSA_HEAD_DIM)

    def masked_q(p):
        qp = qb_ref[:, pcols(p)]
        return [jnp.where(low, qp, jnp.zeros_like(qp)), jnp.where(low, jnp.zeros_like(qp), qp)]

    def score_meta(p):
        qs = masked_q(p)
        ms = []
        for hh in range(2):
            z = (_dot_t(kbm_ref[:, pcols(p)], qs[hh]) * c + bias_ref[2 * p + hh, mb_idx][:LANES, :] * use_mb
                 + selm_scr[...])
            sm_scr[p % 2, hh] = z
            ms.append(_colmax(z))
        return tuple(ms)

    def score_tiles(p, t, ms, n, near):
        qs = masked_q(p)
        k = kb_ref[pl.ds(pl.multiple_of(t * TQ, TQ), n * TQ), pcols(p)]
        out = []
        for hh in range(2):
            zz = _dot_t(k, qs[hh]) * c
            m = ms[hh]
            for i in range(n):
                z = zz[i * TQ:(i + 1) * TQ] + pltpu.bitcast(keys_scr[t + i], F32)
                if near is not None:
                    z = z + bias_ref[2 * p + hh, near]
                s_scr[p % 2, hh, t + i] = z
                m = jnp.maximum(m, _colmax(z))
            out.append(m)
        return tuple(out)

    def pv_meta(p, ms):
        st = []
        for hh in range(2):
            st += list(_pv(vbtm_ref[0, vrow(2 * p + hh), 0:LANES], sm_scr[p % 2, hh], ms[hh],
                           jnp.zeros((1, TQ), F32), jnp.zeros((DSA_HEAD_DIM, TQ), F32)))
        return tuple(st)

    def pv_tiles(p, t, st, ms, n):
        for i in range(n):
            out = []
            for hh in range(2):
                out += list(_pv(vbt_ref[t + i, vrow(2 * p + hh), :], s_scr[p % 2, hh, t + i], ms[hh],
                                st[2 * hh], st[2 * hh + 1]))
            st = tuple(out)
        return st

    def score_all(p, ms):
        ms = _fori_chunks(0, n_far, lambda t, m, n: score_tiles(p, t, m, n, None), ms)
        ms = lax.fori_loop(n_far, n_full, lambda t, m: score_tiles(p, t, m, 1, _B_PREV), ms)
        return lax.fori_loop(n_full, n_vis, lambda t, m: score_tiles(p, t, m, 1, _B_DIAG), ms)

    pairs = DSA_HEADS // 2
    ms = score_all(0, score_meta(0))
    for p in range(pairs):
        st = pv_meta(p, ms)
        if p + 1 < pairs:
            both = lambda near, n=1, p=p, ms=ms: (
                lambda t, c, n=n: (pv_tiles(p, t, c[0], ms, n), score_tiles(p + 1, t, c[1], n, near)))
            carry = (st, score_meta(p + 1))
            carry = _fori_chunks(0, n_far, lambda t, c, n, p=p, ms=ms: (
                pv_tiles(p, t, c[0], ms, n), score_tiles(p + 1, t, c[1], n, None)), carry)
            carry = lax.fori_loop(n_far, n_full, both(_B_PREV), carry)
            st, ms_next = lax.fori_loop(n_full, n_vis, both(_B_DIAG), carry)
        else:
            st = _fori_chunks(0, n_vis, lambda t, s, n, p=p, ms=ms: pv_tiles(p, t, s, ms, n), st)
            ms_next = None
        ot = jnp.concatenate([st[1] / st[0], st[3] / st[2]], axis=0)
        o_ref[:, pcols(p)] = ot.T.astype(BF16)
        ms = ms_next


def _dsa_call(qi, wi, qb, ki, kb, vbt, bias, B, S, k_sel):
    nq = S // TQ
    ng = B * nq + 1
    W = DSA_HEADS * DSA_HEAD_DIM
    bidx = lambda g: jnp.minimum(g // nq, B - 1)
    mrow = B * S // LANES
    qrow = lambda w: pl.BlockSpec((TQ, w), lambda _, g: (g, 0))
    kv = lambda w: pl.BlockSpec((S, w), lambda _, g: (bidx(g), 0))
    kvm = lambda w: pl.BlockSpec((LANES, w), lambda _, g: (mrow, 0))
    return pl.pallas_call(
        functools.partial(_dsa_kernel, nq=nq, k_sel=k_sel),
        grid=(1, ng),
        in_specs=[qrow(W), qrow(LANES), qrow(W), kv(LANES), kv(W),
                  pl.BlockSpec((nq, W, TQ), lambda _, g: (bidx(g), 0, 0)),
                  kvm(LANES), kvm(W),
                  pl.BlockSpec((1, W, TQ), lambda _, g: (ng - 1, 0, 0)),
                  pl.BlockSpec(bias.shape, lambda _, g: (0, 0, 0, 0), pipeline_mode=pl.Buffered(1))],
        out_specs=qrow(W),
        out_shape=jax.ShapeDtypeStruct((ng * TQ, W), BF16),
        scratch_shapes=[
            pltpu.VMEM((nq, TQ, TQ), jnp.int32),
            pltpu.VMEM((LANES, TQ), jnp.int32),
            pltpu.VMEM((LANES, TQ), F32),
            pltpu.VMEM((2, 2, nq, TQ, TQ), F32),
            pltpu.VMEM((2, 2, LANES, TQ), F32),
            pltpu.VMEM((IDX_HEADS, TQ, LANES), BF16),
            pltpu.VMEM((1, TQ), jnp.int32),
            pltpu.VMEM((nq, TQ, TQ), BF16),
            pltpu.VMEM((LANES, TQ), BF16),
        ],
        compiler_params=_params(("arbitrary", "arbitrary")),
        name="dsa",
    )(qi, wi, qb, ki, kb, vbt, ki, kb, vbt, bias)


def _t5_bucket(rel):
    nb = REL_BUCKETS // 2
    max_exact = nb // 2
    n = jnp.abs(rel)
    large = max_exact + (jnp.log(jnp.maximum(n, 1).astype(F32) / max_exact)
                         / math.log(REL_MAX_DIST / max_exact) * (nb - max_exact)).astype(jnp.int32)
    large = jnp.minimum(large, nb - 1)
    return jnp.where(rel > 0, nb, 0) + jnp.where(n < max_exact, n, large)


def _bias_kernel(bucket_ref, rb_ref, o_ref, *, far_bucket):
    h = pl.program_id(0)
    far = rb_ref[far_bucket, h]
    for tile in range(bucket_ref.shape[0]):
        bkt = bucket_ref[tile]
        acc = jnp.zeros(bkt.shape, F32)
        for b in range(REL_BUCKETS):
            acc = jnp.where(bkt == b, rb_ref[b, h], acc)
        o_ref[0, tile] = (acc - far) * LOG2E


def _bias_tiles(rel_bias):
    k = jnp.arange(TQ, dtype=jnp.int32)[:, None]
    q = jnp.arange(TQ, dtype=jnp.int32)[None, :]
    rels = jnp.stack([k - q, k - q - TQ, k - (q + N_META), k - q])
    far_bucket = REL_BUCKETS // 2 - 1
    return pl.pallas_call(
        functools.partial(_bias_kernel, far_bucket=far_bucket),
        grid=(DSA_HEADS,),
        in_specs=[pl.BlockSpec((4, TQ, TQ), lambda h: (0, 0, 0)),
                  pl.BlockSpec(memory_space=pltpu.SMEM)],
        out_specs=pl.BlockSpec((1, 4, TQ, TQ), lambda h: (h, 0, 0, 0)),
        out_shape=jax.ShapeDtypeStruct((DSA_HEADS, 4, TQ, TQ), F32),
        compiler_params=_params(("parallel",)),
        name="bias_tiles",
    )(_t5_bucket(rels), rel_bias)


_MERGE_ROWS = 128


def _merge_kernel(x_ref, meta_ref, oa_ref, ob_ref, ga_ref, gb_ref, wa_ref, wb_ref, wo_ref, h1_ref):
    is_meta = pl.program_id(0) == pl.num_programs(0) - 1
    for rows in (slice(s, s + _MERGE_ROWS) for s in range(0, x_ref.shape[0], _MERGE_ROWS)):
        h = jnp.where(is_meta, meta_ref[rows, :], x_ref[rows, :])
        y = (jax.nn.sigmoid(ga_ref[rows, :]) * _dot(oa_ref[rows, :], wa_ref[...])
             + jax.nn.sigmoid(gb_ref[rows, :]) * _dot(ob_ref[rows, :], wb_ref[...]))
        h1_ref[rows, :] = h + _dot(y.astype(BF16), wo_ref[...])


def _merge_call(x2d, meta_tile, oa, ob, ga, gb, wa, wb, wo):
    BS, D = x2d.shape
    nt = BS // TQ + 1
    row = lambda w: pl.BlockSpec((TQ, w), lambda i: (i, 0))
    full = lambda a: pl.BlockSpec(a.shape, lambda i: (0, 0))
    return pl.pallas_call(
        _merge_kernel,
        grid=(nt,),
        in_specs=[pl.BlockSpec((TQ, D), lambda i: (jnp.minimum(i, nt - 2), 0)), full(meta_tile),
                  row(oa.shape[1]), row(ob.shape[1]), row(D), row(D), full(wa), full(wb), full(wo)],
        out_specs=row(D),
        out_shape=jax.ShapeDtypeStruct((nt * TQ, D), F32),
        compiler_params=_params(("parallel",)),
        name="merge",
    )(x2d, meta_tile, oa, ob, ga, gb, wa, wb, wo)


_ROUTER_ROWS = 40
TE = 256
_ISSUE_UNROLL = 8


def _route_rows(xn, wrt, brt):
    nr = _ROUTER_ROWS
    tm = xn.shape[0]
    lt = lax.dot_general(wrt, xn, (((1,), (1,)), ((), ())), preferred_element_type=F32,
                         precision=lax.Precision.HIGHEST) + brt
    row = lax.broadcasted_iota(jnp.int32, (nr, tm), 0)
    ninf = -jnp.inf
    cmax = lambda a: jnp.max(a, axis=0, keepdims=True)
    cmin = lambda a: jnp.min(a, axis=0, keepdims=True)
    gl = jnp.where((row >= N_EXPERTS) & (row < N_EXPERTS + N_GROUPS), lt, ninf)
    gmax = cmax(gl)
    gsel = cmin(jnp.where(gl == gmax, row, nr)) - N_EXPERTS
    p_group = 1.0 / jnp.sum(jnp.exp(gl - gmax), axis=0, keepdims=True)
    lo = gsel * EXPERTS_PER_GROUP
    el = jnp.where((row >= lo) & (row < lo + EXPERTS_PER_GROUP), lt, ninf)
    m1 = cmax(el)
    i1 = cmin(jnp.where(el == m1, row, nr))
    el2 = jnp.where(row == i1, ninf, el)
    m2 = cmax(el2)
    i2 = cmin(jnp.where(el2 == m2, row, nr))
    e2 = jnp.exp(m2 - m1)
    return i1, i2, p_group / (1.0 + e2), p_group * e2 / (1.0 + e2)


def _rank_kernel(h1_ref, gf_ref, wrt_ref, brt_ref, route_ref, gate_ref, cnt_ref, run_scr):
    @pl.when(pl.program_id(0) == 0)
    def _():
        run_scr[...] = jnp.zeros_like(run_scr)

    tm = h1_ref.shape[0]
    xn = _rms(h1_ref[...], gf_ref[...])
    i1, i2, w1, w2 = _route_rows(xn, wrt_ref[...], brt_ref[...])
    row = lax.broadcasted_iota(jnp.int32, (N_EXPERTS, tm), 0)
    o1 = jnp.where(row == i1, 1.0, 0.0)
    o2 = jnp.where(row == i2, 1.0, 0.0)
    a = lax.broadcasted_iota(jnp.int32, (tm, tm), 0)
    b = lax.broadcasted_iota(jnp.int32, (tm, tm), 1)
    before = jnp.where(a < b, 1.0, 0.0).astype(BF16)
    p1 = _dot(o1.astype(BF16), before)
    p2 = _dot(o2.astype(BF16), before)
    run = run_scr[:, 0:1]
    c1 = jnp.sum(o1, axis=1, keepdims=True)
    c2 = jnp.sum(o2, axis=1, keepdims=True)
    r1 = jnp.sum(o1 * (run + p1), axis=0, keepdims=True)
    r2 = jnp.sum(o2 * (run + c1 + p2), axis=0, keepdims=True)
    new_run = run + c1 + c2
    run_scr[...] = jnp.broadcast_to(new_run, run_scr.shape)
    cnt_ref[...] = jnp.broadcast_to(new_run, cnt_ref.shape).astype(jnp.int32)
    z = jnp.zeros((4, tm), jnp.int32)
    route_ref[...] = jnp.concatenate([i1, r1.astype(jnp.int32), i2, r2.astype(jnp.int32), z], axis=0)
    gt = jnp.concatenate([w1, w2, jnp.zeros((LANES - 2, tm), F32)], axis=0)
    gate_ref[...] = gt.T


def _rank_call(h1, gf, wrt, brt):
    R, D = h1.shape
    full = lambda a: pl.BlockSpec(a.shape, lambda i: (0, 0))
    return pl.pallas_call(
        _rank_kernel,
        grid=(R // TQ,),
        in_specs=[pl.BlockSpec((TQ, D), lambda i: (i, 0)), full(gf), full(wrt), full(brt)],
        out_specs=[pl.BlockSpec((8, TQ), lambda i: (0, i)), pl.BlockSpec((TQ, LANES), lambda i: (i, 0)),
                   pl.BlockSpec((N_EXPERTS, LANES), lambda i: (0, 0))],
        out_shape=[jax.ShapeDtypeStruct((8, R), jnp.int32), jax.ShapeDtypeStruct((R, LANES), F32),
                   jax.ShapeDtypeStruct((N_EXPERTS, LANES), jnp.int32)],
        scratch_shapes=[pltpu.VMEM((N_EXPERTS, LANES), F32)],
        compiler_params=_params(("arbitrary",)),
        name="route_rank",
    )(h1, gf, wrt, brt)


def _dispatch_kernel(pos_ref, h1_ref, gf_ref, xs_in, xs_hbm, buf, sem, *, n_rows):
    del xs_in
    i = pl.program_id(0)
    nt = pl.num_programs(0)
    slot = i % 2
    tm = h1_ref.shape[0]

    def wait_slot(s):
        for _ in range(2):
            pltpu.make_async_copy(buf.at[s], xs_hbm.at[pl.ds(0, tm), :], sem.at[s]).wait()

    @pl.when(i >= 2)
    def _():
        wait_slot(slot)

    buf[slot] = _rms(h1_ref[...], gf_ref[...])

    def issue(u, c):
        for v in range(_ISSUE_UNROLL):
            r = u * _ISSUE_UNROLL + v
            src = buf.at[slot, pl.ds(r, 1), :]
            for k in range(2):
                p = pos_ref[k * n_rows + i * tm + r]
                pltpu.make_async_copy(src, xs_hbm.at[pl.ds(p, 1), :], sem.at[slot]).start()
        return c

    lax.fori_loop(0, tm // _ISSUE_UNROLL, issue, 0)

    @pl.when(i == nt - 1)
    def _():
        wait_slot(slot)

        @pl.when(nt >= 2)
        def _():
            wait_slot(1 - slot)


def _dispatch_call(pos, h1, gf, n_slots):
    R, D = h1.shape
    return pl.pallas_call(
        functools.partial(_dispatch_kernel, n_rows=R),
        grid_spec=pltpu.PrefetchScalarGridSpec(
            num_scalar_prefetch=1, grid=(R // TQ,),
            in_specs=[pl.BlockSpec((TQ, D), lambda i, pos: (i, 0)),
                      pl.BlockSpec(gf.shape, lambda i, pos: (0, 0)),
                      pl.BlockSpec(memory_space=pl.ANY)],
            out_specs=pl.BlockSpec(memory_space=pl.ANY),
            scratch_shapes=[pltpu.VMEM((2, TQ, D), F32), pltpu.SemaphoreType.DMA((2,))]),
        out_shape=jax.ShapeDtypeStruct((n_slots, D), F32),
        input_output_aliases={3: 0},
        compiler_params=_params(("arbitrary",)),
        name="moe_dispatch",
    )(pos, h1, gf, jnp.zeros((n_slots, D), F32))


def _ffn_kernel(be_ref, nu_ref, x_ref, wg_ref, wu_ref, wd_ref, y_ref):
    i = pl.program_id(0)

    @pl.when(i < nu_ref[0])
    def _():
        x = x_ref[...].astype(BF16)
        a = _dot(x, wg_ref[0].astype(BF16))
        u = _dot(x, wu_ref[0].astype(BF16))
        hmid = (a * jax.nn.sigmoid(a) * u).astype(BF16)
        y_ref[...] = _dot(hmid, wd_ref[0].astype(BF16))

    @pl.when(i >= nu_ref[0])
    def _():
        y_ref[...] = jnp.zeros_like(y_ref)


def _ffn_call(blk_expert, n_used, xs, wg, wu, wd):
    NS, D = xs.shape
    wspec = lambda shp: pl.BlockSpec((1,) + shp, lambda i, be, nu: (be[i], 0, 0))
    return pl.pallas_call(
        _ffn_kernel,
        grid_spec=pltpu.PrefetchScalarGridSpec(
            num_scalar_prefetch=2, grid=(NS // TE,),
            in_specs=[pl.BlockSpec((TE, D), lambda i, be, nu: (i, 0)),
                      wspec((D, D_EXPERT)), wspec((D, D_EXPERT)), wspec((D_EXPERT, D))],
            out_specs=pl.BlockSpec((TE, D), lambda i, be, nu: (i, 0))),
        out_shape=jax.ShapeDtypeStruct((NS, D), F32),
        compiler_params=_params(("arbitrary",)),
        name="moe_ffn",
    )(blk_expert, n_used, xs, wg, wu, wd)


def _combine_kernel(pos_ref, h1_ref, gate_ref, gfin_ref, ys_hbm, o_ref, buf, sem, *, n_rows):
    i = pl.program_id(0)
    nt = pl.num_programs(0)
    slot = i % 2
    tm = h1_ref.shape[0]

    def fetch(tile, s):
        def issue(u, c):
            for v in range(_ISSUE_UNROLL):
                r = u * _ISSUE_UNROLL + v
                for k in range(2):
                    p = pos_ref[k * n_rows + tile * tm + r]
                    pltpu.make_async_copy(ys_hbm.at[pl.ds(p, 1), :], buf.at[s, k, pl.ds(r, 1), :],
                                          sem.at[s]).start()
            return c
        lax.fori_loop(0, tm // _ISSUE_UNROLL, issue, 0)

    @pl.when(i == 0)
    def _():
        fetch(0, 0)

    @pl.when(i + 1 < nt)
    def _():
        fetch(i + 1, 1 - slot)

    for k in range(2):
        pltpu.make_async_copy(ys_hbm.at[pl.ds(0, tm), :], buf.at[slot, k], sem.at[slot]).wait()

    g = gate_ref[...]
    ffn = g[:, 0:1] * buf[slot, 0] + g[:, 1:2] * buf[slot, 1]

    @pl.when(i < nt - 1)
    def _():
        o_ref[...] = _rms(h1_ref[...] + ffn, gfin_ref[...])


def _combine_call(pos, h1, gates, gfin, ys, n_out):
    R, D = h1.shape
    nt = R // TQ
    return pl.pallas_call(
        functools.partial(_combine_kernel, n_rows=R),
        grid_spec=pltpu.PrefetchScalarGridSpec(
            num_scalar_prefetch=1, grid=(nt,),
            in_specs=[pl.BlockSpec((TQ, D), lambda i, pos: (i, 0)),
                      pl.BlockSpec((TQ, LANES), lambda i, pos: (i, 0)),
                      pl.BlockSpec(gfin.shape, lambda i, pos: (0, 0)),
                      pl.BlockSpec(memory_space=pl.ANY)],
            out_specs=pl.BlockSpec((TQ, D), lambda i, pos: (jnp.minimum(i, nt - 2), 0)),
            scratch_shapes=[pltpu.VMEM((2, 2, TQ, D), F32), pltpu.SemaphoreType.DMA((2,))]),
        out_shape=jax.ShapeDtypeStruct((n_out, D), F32),
        compiler_params=_params(("arbitrary",)),
        name="moe_combine",
    )(pos, h1, gates, gfin, ys)


def _sparse_moe(h1, gf, wrt, brt, wg, wu, wd, gfin, n_out):
    R, D = h1.shape
    route, gates, cnt = _rank_call(h1, gf, wrt, brt)
    counts = cnt[:, 0]
    padded = (counts + TE - 1) // TE * TE
    pad_end = jnp.cumsum(padded)
    offs = pad_end - padded
    nb = -(-2 * R // TE) + N_EXPERTS
    eids = jnp.arange(N_EXPERTS, dtype=jnp.int32)
    slot_of = lambda e, r: r + jnp.sum(jnp.where(e[:, None] == eids[None, :], offs[None, :], 0), axis=1)
    pos = jnp.concatenate([slot_of(route[0], route[1]), slot_of(route[2], route[3])]).astype(jnp.int32)
    blk_start = jnp.arange(nb, dtype=jnp.int32) * TE
    blk_expert = jnp.minimum(jnp.sum((blk_start[:, None] >= pad_end[None, :]).astype(jnp.int32), axis=1),
                             N_EXPERTS - 1).astype(jnp.int32)
    n_used = (pad_end[-1] // TE).astype(jnp.int32).reshape(1)
    xs = _dispatch_call(pos, h1, gf, nb * TE)
    ys = _ffn_call(blk_expert, n_used, xs, wg, wu, wd)
    return _combine_call(pos, h1, gates, gfin, ys, n_out)


def kernel(x, meta_tokens, norm_mix_g, w_in, mla_cq_norm_g, mla_ckv_norm_g, w_mla_uq, w_mla_ukv,
           w_branch_a, w_branch_b, w_out, rel_bias, norm_ffn_g, w_router_group, b_router_group,
           w_router_expert, b_router_expert, w_exp_gate, w_exp_up, w_exp_down, norm_final_g):
    B, S, D = x.shape
    assert S % TQ == 0 and norm_mix_g.shape[0] == 1
    k_sel = min(K_SEL_MAX, S // 4)
    x2d = x.reshape(B * S, D)
    meta_tile = jnp.concatenate([meta_tokens.astype(x.dtype), jnp.zeros((TQ - N_META, D), x.dtype)], axis=0)

    pos = np.concatenate([N_META + np.arange(S), np.minimum(np.arange(TQ), N_META)]).astype(np.int32)
    ct, st = _rope_tables(jnp.asarray(pos))

    qm, km, vmt, qb, kb, vbt, qi, ki, wi, ga, gb = _proj_call(
        x2d, meta_tile, norm_mix_g.reshape(1, D), _pack_w1(w_in[0]), mla_cq_norm_g.reshape(1, -1),
        mla_ckv_norm_g.reshape(1, -1), _pack_wuq(w_mla_uq[0]), _pack_wukv(w_mla_ukv[0]), ct, st, S)

    o_a = _mla_call(qm, km, vmt, B, S)
    o_b = _dsa_call(qi, wi, qb, ki, kb, vbt, _bias_tiles(rel_bias), B, S, k_sel)

    pad_r = _ROUTER_ROWS - N_EXPERTS - N_GROUPS
    w_r = jnp.concatenate([w_router_expert[0].T, w_router_group[0].T, jnp.zeros((pad_r, D), F32)], axis=0)
    b_r = jnp.concatenate([b_router_expert[0], b_router_group[0],
                           jnp.zeros((pad_r,), F32)]).reshape(_ROUTER_ROWS, 1)
    h1 = _merge_call(x2d, meta_tile, o_a, o_b, ga, gb, w_branch_a[0].astype(BF16),
                     w_branch_b[0].astype(BF16), w_out[0].astype(BF16))
    out = _sparse_moe(h1, norm_ffn_g.reshape(1, D), w_r, b_r, w_exp_gate[0], w_exp_up[0], w_exp_down[0],
                      norm_final_g.reshape(1, D), B * S)
    return out.reshape(B, S, D)
```

```python
import functools
import math

import numpy as np
import jax
import jax.numpy as jnp
from jax import lax
from jax.experimental import pallas as pl
from jax.experimental.pallas import tpu as pltpu

CHUNK = 64
N_META = 16
NEG_INF = -1e30
RMS_EPS = 1e-6
ROPE_BASE = 10000.0
MLA_HEADS = 8
MLA_Q_LORA = 256
MLA_KV_LORA = 128
MLA_NOPE = 64
MLA_ROPE = 32
MLA_V = 64
DSA_HEADS = 8
DSA_HEAD_DIM = 64
IDX_HEADS = 8
IDX_DIM = 64
K_SEL_MAX = 256
REL_BUCKETS = 32
REL_MAX_DIST = 128
N_GROUPS = 4
EXPERTS_PER_GROUP = 8
N_EXPERTS = N_GROUPS * EXPERTS_PER_GROUP
D_EXPERT = 256

LANES = 128
TQ = 256
HEAD_PAD = 128
VMEM_LIMIT = 56 * 1024 * 1024
INT_MIN = -2 ** 31
LOG2E = math.log2(math.e)

F32 = jnp.float32
BF16 = jnp.bfloat16


def _params(sem):
    return pltpu.CompilerParams(dimension_semantics=sem, vmem_limit_bytes=VMEM_LIMIT)


def _rms(x, g):
    return x * lax.rsqrt(jnp.mean(x * x, axis=-1, keepdims=True) + RMS_EPS) * g


def _dot(a, b):
    return jnp.dot(a, b, preferred_element_type=F32)


def _dot_t(a, b):
    return lax.dot_general(a, b, (((1,), (1,)), ((), ())), preferred_element_type=F32)


_C_CQ = (0, 256)
_C_CKV = (256, 384)
_C_KRX = (384, 512)
_C_KRY = (512, 640)
_C_QB = (640, 1152)
_C_KB = (1152, 1664)
_C_VB = (1664, 2176)
_C_QI = (2176, 2688)
_C_KI = (2688, 2816)
_C_WI = (2816, 2944)
_C_GA = (2944, 3968)
_C_GB = (3968, 4992)
_W1_COLS = 4992


def _proj_kernel(x_ref, meta_ref, g_ref, w1_ref, gcq_ref, gckv_ref, wuq_ref, wukv_ref, ct_ref, st_ref,
                 qm_ref, km_ref, vm_ref, qb_ref, kb_ref, vb_ref, qi_ref, ki_ref, wi_ref,
                 ga_ref, gb_ref):
    is_meta = pl.program_id(0) == pl.num_programs(0) - 1
    h = jnp.where(is_meta, meta_ref[...], x_ref[...])
    xb = _rms(h, g_ref[...]).astype(BF16)

    def seg(c):
        return _dot(xb, w1_ref[:, c[0]:c[1]])

    ct = ct_ref[...]
    st = st_ref[...]
    nq = _rms(seg(_C_CQ), gcq_ref[...]).astype(BF16)
    qa = _dot(nq, wuq_ref[...])
    half = MLA_HEADS * HEAD_PAD
    for hd in range(MLA_HEADS):
        lo, hi = hd * HEAD_PAD, (hd + 1) * HEAD_PAD
        qm_ref[:, lo:hi] = (qa[:, lo:hi] * ct + qa[:, half + lo:half + hi] * st).astype(BF16)
    nkv = _rms(seg(_C_CKV), gckv_ref[...]).astype(BF16)
    kva = _dot(nkv, wukv_ref[...])
    kr = seg(_C_KRX) * ct + seg(_C_KRY) * st
    for hd in range(MLA_HEADS):
        lo, hi = hd * HEAD_PAD, (hd + 1) * HEAD_PAD
        km_ref[:, lo:hi] = (kva[:, lo:hi] + kr).astype(BF16)
    vm_ref[0] = kva[:, half:].T.astype(BF16)
    qb_ref[...] = seg(_C_QB).astype(BF16)
    kb_ref[...] = seg(_C_KB).astype(BF16)
    vb_ref[0] = seg(_C_VB).T.astype(BF16)
    qi_ref[...] = seg(_C_QI).astype(BF16)
    ki_ref[...] = seg(_C_KI).astype(BF16)
    wi_ref[...] = seg(_C_WI)
    ga_ref[...] = seg(_C_GA)
    gb_ref[...] = seg(_C_GB)


def _pack_w1(w_in):
    D = w_in.shape[0]
    offs = np.cumsum([0, MLA_Q_LORA, MLA_KV_LORA, MLA_ROPE, 512, 512, 512, 512, IDX_DIM, IDX_HEADS, D, D])
    c = lambda i: w_in[:, offs[i]:offs[i + 1]]
    z = lambda n: jnp.zeros((D, n), w_in.dtype)
    k_r = c(2)
    hr = MLA_ROPE // 2
    k_r_sw = jnp.concatenate([k_r[:, hr:], k_r[:, :hr]], axis=1)
    cols = [
        c(0), c(1),
        z(MLA_NOPE), k_r, z(LANES - MLA_NOPE - MLA_ROPE),
        z(MLA_NOPE), k_r_sw, z(LANES - MLA_NOPE - MLA_ROPE),
        c(3), c(4), c(5),
        c(6) * (IDX_DIM ** -0.5),
        c(7), c(7),
        c(8), z(LANES - IDX_HEADS),
        c(9), c(10),
    ]
    w1 = jnp.concatenate(cols, axis=1)
    assert w1.shape[1] == _W1_COLS
    return w1.astype(BF16)


def _pack_wuq(w_uq):
    R = w_uq.shape[0]
    w = w_uq.reshape(R, MLA_HEADS, MLA_NOPE + MLA_ROPE)
    nope, rope = w[..., :MLA_NOPE], w[..., MLA_NOPE:]
    hr = MLA_ROPE // 2
    rope_sw = jnp.concatenate([rope[..., hr:], rope[..., :hr]], axis=-1)
    zt = jnp.zeros((R, MLA_HEADS, HEAD_PAD - MLA_NOPE - MLA_ROPE), w.dtype)
    a = jnp.concatenate([nope, rope, zt], axis=-1).reshape(R, MLA_HEADS * HEAD_PAD)
    b = jnp.concatenate([jnp.zeros_like(nope), rope_sw, zt], axis=-1).reshape(R, MLA_HEADS * HEAD_PAD)
    return jnp.concatenate([a, b], axis=1).astype(BF16)


def _pack_wukv(w_ukv):
    R = w_ukv.shape[0]
    w = w_ukv.reshape(R, MLA_HEADS, MLA_NOPE + MLA_V)
    k = jnp.concatenate([w[..., :MLA_NOPE], jnp.zeros((R, MLA_HEADS, HEAD_PAD - MLA_NOPE), w.dtype)], axis=-1)
    v = w[..., MLA_NOPE:]
    return jnp.concatenate([k.reshape(R, -1), v.reshape(R, -1)], axis=1).astype(BF16)


def _rope_tables(pos):
    half = MLA_ROPE // 2
    inv = ROPE_BASE ** (-jnp.arange(half, dtype=F32) / half)
    ang = pos.astype(F32)[:, None] * inv[None, :]
    cos, sin = jnp.cos(ang), jnp.sin(ang)
    n = pos.shape[0]
    tail = jnp.zeros((n, HEAD_PAD - MLA_NOPE - MLA_ROPE), F32)
    ct = jnp.concatenate([jnp.ones((n, MLA_NOPE), F32), cos, cos, tail], axis=1)
    st = jnp.concatenate([jnp.zeros((n, MLA_NOPE), F32), -sin, sin, tail], axis=1)
    return ct, st


def _proj_call(x2d, meta_tile, g, w1, gcq, gckv, wuq, wukv, ct, st, S):
    BS, D = x2d.shape
    nt = BS // TQ + 1
    per_b = S // TQ
    row = lambda w: pl.BlockSpec((TQ, w), lambda i: (i, 0))
    full = lambda a: pl.BlockSpec(a.shape, lambda i: (0, 0))
    tab = pl.BlockSpec((TQ, HEAD_PAD), lambda i: (jnp.where(i == nt - 1, per_b, i % per_b), 0))
    widths = [(MLA_HEADS * HEAD_PAD, BF16), (MLA_HEADS * HEAD_PAD, BF16), (None, BF16),
              (512, BF16), (512, BF16), (None, BF16), (512, BF16), (LANES, BF16), (LANES, F32),
              (D, F32), (D, F32)]
    vt_rows = MLA_HEADS * MLA_V
    vt_spec = pl.BlockSpec((1, vt_rows, TQ), lambda i: (i, 0, 0))
    vt_shape = jax.ShapeDtypeStruct((nt, vt_rows, TQ), BF16)
    return pl.pallas_call(
        _proj_kernel,
        grid=(nt,),
        in_specs=[pl.BlockSpec((TQ, D), lambda i: (jnp.minimum(i, nt - 2), 0)), full(meta_tile),
                  full(g), full(w1), full(gcq), full(gckv), full(wuq), full(wukv), tab, tab],
        out_specs=[vt_spec if w is None else row(w) for w, _ in widths],
        out_shape=[vt_shape if w is None else jax.ShapeDtypeStruct((nt * TQ, w), dt) for w, dt in widths],
        compiler_params=_params(("parallel",)),
        name="proj",
    )(x2d, meta_tile, g, w1, gcq, gckv, wuq, wukv, ct, st)


def _colmax(z):
    return jnp.max(z, axis=0, keepdims=True)


def _pv(vt, z, m, l, acc):
    p = jnp.exp2(z - m)
    return l + jnp.sum(p, axis=0, keepdims=True), acc + _dot(vt, p.astype(BF16))


def _tile_ids(nq):
    g = pl.program_id(1)
    is_meta = g == pl.num_programs(1) - 1
    return is_meta, jnp.where(is_meta, 0, g % nq)


def _chunk_mask():
    key = lax.broadcasted_iota(jnp.int32, (TQ, TQ), 0)
    qry = lax.broadcasted_iota(jnp.int32, (TQ, TQ), 1)
    shift = CHUNK.bit_length() - 1
    return (key >> shift) <= (qry >> shift)


_CHUNKS = (8, 4, 2, 1)


def _fori_chunks(lo, hi, body_n, carry):
    for n in _CHUNKS:
        cnt = (hi - lo) // n
        carry = lax.fori_loop(0, cnt, lambda u, c, lo=lo, n=n: body_n(lo + n * u, c, n), carry)
        lo = lo + cnt * n
    return carry


def _fori_each(lo, hi, body, carry):
    def body_n(t, c, n):
        for i in range(n):
            c = body(t + i, c)
        return c
    return _fori_chunks(lo, hi, body_n, carry)


def _loop(lo, hi, fn):
    _fori_each(lo, hi, lambda t, c: (fn(t), c)[1], 0)


def _mla_kernel(q_ref, k_ref, vt_ref, km_ref, vtm_ref, o_ref, s_scr, sm_scr, *, nq):
    is_meta, j = _tile_ids(nq)
    n_full = jnp.where(is_meta, 0, j)
    n_vis = jnp.where(is_meta, 0, j + 1)
    c = (MLA_NOPE + MLA_ROPE) ** -0.5 * LOG2E
    meta_ok = lax.broadcasted_iota(jnp.int32, (LANES, TQ), 0) < N_META
    diag_ok = _chunk_mask()
    heads = lambda p: (2 * p, 2 * p + 1)
    hcol = lambda h: slice(h * HEAD_PAD, (h + 1) * HEAD_PAD)
    vrow = lambda h: slice(h * MLA_V, (h + 1) * MLA_V)

    def score_meta(p):
        ms = []
        for hh, h in enumerate(heads(p)):
            z = jnp.where(meta_ok, _dot_t(km_ref[:, hcol(h)], q_ref[:, hcol(h)]) * c, NEG_INF)
            sm_scr[p % 2, hh] = z
            ms.append(_colmax(z))
        return tuple(ms)

    def score_tiles(p, t, ms, n, masked):
        ks = pl.multiple_of(t * TQ, TQ)
        out = []
        for hh, h in enumerate(heads(p)):
            z = _dot_t(k_ref[pl.ds(ks, n * TQ), hcol(h)], q_ref[:, hcol(h)]) * c
            if masked:
                z = jnp.where(diag_ok, z, NEG_INF)
            for i in range(n):
                s_scr[p % 2, hh, t + i] = z[i * TQ:(i + 1) * TQ]
            out.append(jnp.maximum(ms[hh], _colmax(z)))
        return tuple(out)

    def pv_meta(p, ms):
        st = []
        for hh, h in enumerate(heads(p)):
            st += list(_pv(vtm_ref[0, vrow(h), 0:LANES], sm_scr[p % 2, hh], ms[hh],
                           jnp.zeros((1, TQ), F32), jnp.zeros((MLA_V, TQ), F32)))
        return tuple(st)

    def pv_tiles(p, t, st, ms, n):
        for i in range(n):
            out = []
            for hh, h in enumerate(heads(p)):
                out += list(_pv(vt_ref[t + i, vrow(h), :], s_scr[p % 2, hh, t + i], ms[hh],
                                st[2 * hh], st[2 * hh + 1]))
            st = tuple(out)
        return st

    pairs = MLA_HEADS // 2
    ms = score_meta(0)
    ms = _fori_chunks(0, n_full, lambda t, m, n: score_tiles(0, t, m, n, False), ms)
    ms = lax.fori_loop(n_full, n_vis, lambda t, m: score_tiles(0, t, m, 1, True), ms)
    for p in range(pairs):
        st = pv_meta(p, ms)
        if p + 1 < pairs:
            ms_next = score_meta(p + 1)
            st, ms_next = _fori_chunks(
                0, n_full, lambda t, c, n, p=p, ms=ms: (pv_tiles(p, t, c[0], ms, n),
                                                        score_tiles(p + 1, t, c[1], n, False)), (st, ms_next))
            st, ms_next = lax.fori_loop(
                n_full, n_vis, lambda t, c, p=p, ms=ms: (pv_tiles(p, t, c[0], ms, 1),
                                                         score_tiles(p + 1, t, c[1], 1, True)), (st, ms_next))
        else:
            st = _fori_chunks(0, n_vis, lambda t, s, n, p=p, ms=ms: pv_tiles(p, t, s, ms, n), st)
            ms_next = None
        ot = jnp.concatenate([st[1] / st[0], st[3] / st[2]], axis=0)
        o_ref[:, p * 2 * MLA_V:(p + 1) * 2 * MLA_V] = ot.T.astype(BF16)
        ms = ms_next


def _mla_call(qm, km, vmt, B, S):
    nq = S // TQ
    ng = B * nq + 1
    bidx = lambda g: jnp.minimum(g // nq, B - 1)
    mrow = B * S // LANES
    wk, wv = MLA_HEADS * HEAD_PAD, MLA_HEADS * MLA_V
    return pl.pallas_call(
        functools.partial(_mla_kernel, nq=nq),
        grid=(1, ng),
        in_specs=[
            pl.BlockSpec((TQ, wk), lambda _, g: (g, 0)),
            pl.BlockSpec((S, wk), lambda _, g: (bidx(g), 0)),
            pl.BlockSpec((nq, wv, TQ), lambda _, g: (bidx(g), 0, 0)),
            pl.BlockSpec((LANES, wk), lambda _, g: (mrow, 0)),
            pl.BlockSpec((1, wv, TQ), lambda _, g: (ng - 1, 0, 0)),
        ],
        out_specs=pl.BlockSpec((TQ, wv), lambda _, g: (g, 0)),
        out_shape=jax.ShapeDtypeStruct((ng * TQ, wv), BF16),
        scratch_shapes=[pltpu.VMEM((2, 2, nq, TQ, TQ), F32),
                        pltpu.VMEM((2, 2, LANES, TQ), F32)],
        compiler_params=_params(("arbitrary", "arbitrary")),
        name="mla_attn",
    )(qm, km, vmt, km, vmt)


_IDX_BITS = 13
_B_DIAG, _B_PREV, _B_META_FIRST, _B_META_META = 0, 1, 2, 3


def _dsa_kernel(qi_ref, wi_ref, qb_ref, ki_ref, kb_ref, vbt_ref, kim_ref, kbm_ref, vbtm_ref, bias_ref, o_ref,
                keys_scr, keym_scr, selm_scr, s_scr, sm_scr, qs_scr, mstar_scr,
                dig_scr, dm_scr, *, nq, k_sel):
    is_meta, j = _tile_ids(nq)
    n_full = jnp.where(is_meta, 0, j)
    n_vis = jnp.where(is_meta, 0, j + 1)
    lane = lax.broadcasted_iota(jnp.int32, (TQ, LANES), 1)
    low = lane < DSA_HEAD_DIM
    krow_m = lax.broadcasted_iota(jnp.int32, (LANES, TQ), 0)
    krow = lax.broadcasted_iota(jnp.int32, (TQ, TQ), 0)
    meta_ok = krow_m < N_META
    diag_ok = _chunk_mask()

    for p in range(IDX_HEADS // 2):
        qp = qi_ref[:, p * LANES:(p + 1) * LANES]
        qs_scr[2 * p] = jnp.where(low, qp, jnp.zeros_like(qp))
        qs_scr[2 * p + 1] = jnp.where(low, jnp.zeros_like(qp), qp)
    wt = (wi_ref[...] * (IDX_HEADS ** -0.5)).T

    def sort_key(score):
        bits = pltpu.bitcast(score + 0.0, jnp.int32)
        return bits ^ ((bits >> 31) & 0x7FFFFFFF)

    def score_of(kt):
        score = jnp.zeros((kt.shape[0], TQ), F32)
        for h in range(IDX_HEADS):
            score = score + wt[h:h + 1, :] * jnp.maximum(_dot_t(kt, qs_scr[h]), 0.0)
        return sort_key(score)

    keym_scr[...] = jnp.where(meta_ok, score_of(kim_ref[...]), INT_MIN)

    def score_tiles(t, n, masked):
        key = score_of(ki_ref[pl.ds(pl.multiple_of(t * TQ, TQ), n * TQ), :])
        if masked:
            key = jnp.where(diag_ok, key, INT_MIN)
        for i in range(n):
            keys_scr[t + i] = key[i * TQ:(i + 1) * TQ]

    _fori_chunks(0, n_full, lambda t, c, n: (score_tiles(t, n, False), c)[1], 0)
    lax.fori_loop(n_full, n_vis, lambda t, c: (score_tiles(t, 1, True), c)[1], 0)

    one = lambda m: jnp.where(m, 1, 0)
    fold = lambda a: jnp.sum(a.reshape(-1, 8, TQ), axis=0)

    def count(pred):
        def cbody(t, acc):
            return acc + fold(pred(keys_scr[t], LANES + t * TQ + krow))
        acc = _fori_each(0, n_vis, cbody, fold(pred(keym_scr[...], krow_m)))
        return jnp.sum(acc, axis=0, keepdims=True)

    def fold16(m):
        parts = [m[r:r + 16] for r in range(0, m.shape[0], 16)]
        while len(parts) > 1:
            parts = [a + b for a, b in zip(parts[::2], parts[1::2])]
        return parts[0].astype(F32)

    one16 = lambda m: jnp.where(m, jnp.ones((), BF16), jnp.zeros((), BF16))

    def count16(meta_scr, tile_scr, pred):
        acc = _fori_each(0, n_vis, lambda t, a: a + fold16(one16(pred(tile_scr[t]))),
                         fold16(one16(pred(meta_scr[...]))))
        return jnp.sum(acc, axis=0, keepdims=True)

    def digit_search(prefix, shift, need):
        top, last = shift == 24, shift == 0

        def digits(key):
            d = ((key >> shift) + 128 if top else (key >> shift) & 0xFF).astype(F32)
            if not top:
                d = jnp.where((key >> (shift + 8)) == prefix, d, -1.0)
            return d.astype(BF16)

        dm_scr[...] = digits(keym_scr[...])
        _loop(0, n_vis, lambda t: dig_scr.__setitem__(t, digits(keys_scr[t])))

        def d_body(i, c):
            d, above, at_least = c
            cand = d + lax.shift_left(jnp.int32(1), 7 - i)
            cb = cand.astype(F32).astype(BF16)
            cnt = count16(dm_scr, dig_scr, lambda v: v >= cb)
            ok = cnt >= need
            return jnp.where(ok, cand, d), jnp.where(ok, above, cnt), jnp.where(ok, cnt, at_least)

        zero = jnp.zeros((1, TQ), F32)
        all_cands = count16(dm_scr, dig_scr, lambda v: v >= jnp.zeros((), BF16)) if last else zero
        return lax.fori_loop(0, 8, d_body, (jnp.zeros((1, TQ), jnp.int32), zero, all_cands))

    t = jnp.zeros((1, TQ), jnp.int32)
    need_d = jnp.full((1, TQ), float(k_sel), F32)
    for shift in (24, 16, 8, 0):
        d, above, at_least = digit_search(t, shift, need_d)
        t = (d - 128) if shift == 24 else (t << 8) | d
        need_d = need_d - above
    need = need_d.astype(jnp.int32)
    cnt_ge = k_sel - need + (at_least - above).astype(jnp.int32)

    mstar_scr[...] = jnp.full((1, TQ), 2 ** _IDX_BITS, jnp.int32)
    tied = jnp.max(jnp.where((cnt_ge > k_sel) & (t != INT_MIN), 1, 0))

    @pl.when(tied > 0)
    def _():
        def tie_body(i, r):
            cand = r | lax.shift_left(jnp.int32(1), _IDX_BITS - 1 - i)
            f = count(lambda b, idx: jnp.where(b == t, one(idx < cand), 0))
            return jnp.where(f < need, cand, r)
        r = lax.fori_loop(0, _IDX_BITS, tie_body, jnp.zeros((1, TQ), jnp.int32))
        mstar_scr[...] = r + 1

    mstar = mstar_scr[...]

    def sel_mask(b, idx):
        tie_ok = jnp.where(b == t, one(idx < mstar), 0)
        sel = jnp.where(b == INT_MIN, 0, jnp.where(b > t, 1, tie_ok))
        return jnp.where(sel > 0, 0.0, NEG_INF)

    selm_scr[...] = sel_mask(keym_scr[...], krow_m)
    _loop(0, n_vis, lambda t: keys_scr.__setitem__(
        t, pltpu.bitcast(sel_mask(keys_scr[t], LANES + t * TQ + krow), jnp.int32)))

    c = DSA_HEAD_DIM ** -0.5 * LOG2E
    n_far = jnp.maximum(n_full - 1, 0)
    use_mb = jnp.where(is_meta | (j == 0), 1.0, 0.0)
    mb_idx = jnp.where(is_meta, _B_META_META, _B_META_FIRST)
    pcols = lambda p: slice(p * LANES, (p + 1) * LANES)
    vrow = lambda h: slice(h * DSA_HEAD_DIM, (h + 1) * DSA_HEAD_DIM)

    def masked_q(p):
        qp = qb_ref[:, pcols(p)]
        return [jnp.where(low, qp, jnp.zeros_like(qp)), jnp.where(low, jnp.zeros_like(qp), qp)]

    def score_meta(p):
        qs = masked_q(p)
        ms = []
        for hh in range(2):
            z = (_dot_t(kbm_ref[:, pcols(p)], qs[hh]) * c + bias_ref[2 * p + hh, mb_idx][:LANES, :] * use_mb
                 + selm_scr[...])
            sm_scr[p % 2, hh] = z
            ms.append(_colmax(z))
        return tuple(ms)

    def score_tiles(p, t, ms, n, near):
        qs = masked_q(p)
        k = kb_ref[pl.ds(pl.multiple_of(t * TQ, TQ), n * TQ), pcols(p)]
        out = []
        for hh in range(2):
            zz = _dot_t(k, qs[hh]) * c
            m = ms[hh]
            for i in range(n):
                z = zz[i * TQ:(i + 1) * TQ] + pltpu.bitcast(keys_scr[t + i], F32)
                if near is not None:
                    z = z + bias_ref[2 * p + hh, near]
                s_scr[p % 2, hh, t + i] = z
                m = jnp.maximum(m, _colmax(z))
            out.append(m)
        return tuple(out)

    def pv_meta(p, ms):
        st = []
        for hh in range(2):
            st += list(_pv(vbtm_ref[0, vrow(2 * p + hh), 0:LANES], sm_scr[p % 2, hh], ms[hh],
                           jnp.zeros((1, TQ), F32), jnp.zeros((DSA_HEAD_DIM, TQ), F32)))
        return tuple(st)

    def pv_tiles(p, t, st, ms, n):
        for i in range(n):
            out = []
            for hh in range(2):
                out += list(_pv(vbt_ref[t + i, vrow(2 * p + hh), :], s_scr[p % 2, hh, t + i], ms[hh],
                                st[2 * hh], st[2 * hh + 1]))
            st = tuple(out)
        return st

    def score_all(p, ms):
        ms = _fori_chunks(0, n_far, lambda t, m, n: score_tiles(p, t, m, n, None), ms)
        ms = lax.fori_loop(n_far, n_full, lambda t, m: score_tiles(p, t, m, 1, _B_PREV), ms)
        return lax.fori_loop(n_full, n_vis, lambda t, m: score_tiles(p, t, m, 1, _B_DIAG), ms)

    pairs = DSA_HEADS // 2
    ms = score_all(0, score_meta(0))
    for p in range(pairs):
        st = pv_meta(p, ms)
        if p + 1 < pairs:
            both = lambda near, n=1, p=p, ms=ms: (
                lambda t, c, n=n: (pv_tiles(p, t, c[0], ms, n), score_tiles(p + 1, t, c[1], n, near)))
            carry = (st, score_meta(p + 1))
            carry = _fori_chunks(0, n_far, lambda t, c, n, p=p, ms=ms: (
                pv_tiles(p, t, c[0], ms, n), score_tiles(p + 1, t, c[1], n, None)), carry)
            carry = lax.fori_loop(n_far, n_full, both(_B_PREV), carry)
            st, ms_next = lax.fori_loop(n_full, n_vis, both(_B_DIAG), carry)
        else:
            st = _fori_chunks(0, n_vis, lambda t, s, n, p=p, ms=ms: pv_tiles(p, t, s, ms, n), st)
            ms_next = None
        ot = jnp.concatenate([st[1] / st[0], st[3] / st[2]], axis=0)
        o_ref[:, pcols(p)] = ot.T.astype(BF16)
        ms = ms_next


def _dsa_call(qi, wi, qb, ki, kb, vbt, bias, B, S, k_sel):
    nq = S // TQ
    ng = B * nq + 1
    W = DSA_HEADS * DSA_HEAD_DIM
    bidx = lambda g: jnp.minimum(g // nq, B - 1)
    mrow = B * S // LANES
    qrow = lambda w: pl.BlockSpec((TQ, w), lambda _, g: (g, 0))
    kv = lambda w: pl.BlockSpec((S, w), lambda _, g: (bidx(g), 0))
    kvm = lambda w: pl.BlockSpec((LANES, w), lambda _, g: (mrow, 0))
    return pl.pallas_call(
        functools.partial(_dsa_kernel, nq=nq, k_sel=k_sel),
        grid=(1, ng),
        in_specs=[qrow(W), qrow(LANES), qrow(W), kv(LANES), kv(W),
                  pl.BlockSpec((nq, W, TQ), lambda _, g: (bidx(g), 0, 0)),
                  kvm(LANES), kvm(W),
                  pl.BlockSpec((1, W, TQ), lambda _, g: (ng - 1, 0, 0)),
                  pl.BlockSpec(bias.shape, lambda _, g: (0, 0, 0, 0), pipeline_mode=pl.Buffered(1))],
        out_specs=qrow(W),
        out_shape=jax.ShapeDtypeStruct((ng * TQ, W), BF16),
        scratch_shapes=[
            pltpu.VMEM((nq, TQ, TQ), jnp.int32),
            pltpu.VMEM((LANES, TQ), jnp.int32),
            pltpu.VMEM((LANES, TQ), F32),
            pltpu.VMEM((2, 2, nq, TQ, TQ), F32),
            pltpu.VMEM((2, 2, LANES, TQ), F32),
            pltpu.VMEM((IDX_HEADS, TQ, LANES), BF16),
            pltpu.VMEM((1, TQ), jnp.int32),
            pltpu.VMEM((nq, TQ, TQ), BF16),
            pltpu.VMEM((LANES, TQ), BF16),
        ],
        compiler_params=_params(("arbitrary", "arbitrary")),
        name="dsa",
    )(qi, wi, qb, ki, kb, vbt, ki, kb, vbt, bias)


def _t5_bucket(rel):
    nb = REL_BUCKETS // 2
    max_exact = nb // 2
    n = jnp.abs(rel)
    large = max_exact + (jnp.log(jnp.maximum(n, 1).astype(F32) / max_exact)
                         / math.log(REL_MAX_DIST / max_exact) * (nb - max_exact)).astype(jnp.int32)
    large = jnp.minimum(large, nb - 1)
    return jnp.where(rel > 0, nb, 0) + jnp.where(n < max_exact, n, large)


def _bias_kernel(bucket_ref, rb_ref, o_ref, *, far_bucket):
    h = pl.program_id(0)
    far = rb_ref[far_bucket, h]
    for tile in range(bucket_ref.shape[0]):
        bkt = bucket_ref[tile]
        acc = jnp.zeros(bkt.shape, F32)
        for b in range(REL_BUCKETS):
            acc = jnp.where(bkt == b, rb_ref[b, h], acc)
        o_ref[0, tile] = (acc - far) * LOG2E


def _bias_tiles(rel_bias):
    k = jnp.arange(TQ, dtype=jnp.int32)[:, None]
    q = jnp.arange(TQ, dtype=jnp.int32)[None, :]
    rels = jnp.stack([k - q, k - q - TQ, k - (q + N_META), k - q])
    far_bucket = REL_BUCKETS // 2 - 1
    return pl.pallas_call(
        functools.partial(_bias_kernel, far_bucket=far_bucket),
        grid=(DSA_HEADS,),
        in_specs=[pl.BlockSpec((4, TQ, TQ), lambda h: (0, 0, 0)),
                  pl.BlockSpec(memory_space=pltpu.SMEM)],
        out_specs=pl.BlockSpec((1, 4, TQ, TQ), lambda h: (h, 0, 0, 0)),
        out_shape=jax.ShapeDtypeStruct((DSA_HEADS, 4, TQ, TQ), F32),
        compiler_params=_params(("parallel",)),
        name="bias_tiles",
    )(_t5_bucket(rels), rel_bias)


_MERGE_ROWS = 128


def _merge_kernel(x_ref, meta_ref, oa_ref, ob_ref, ga_ref, gb_ref, wa_ref, wb_ref, wo_ref, gf_ref, wrt_ref, brt_ref,
                  h1_ref, route_ref, gate_ref, cnt_ref, run_scr):
    is_meta = pl.program_id(0) == pl.num_programs(0) - 1
    for rows in (slice(s, s + _MERGE_ROWS) for s in range(0, x_ref.shape[0], _MERGE_ROWS)):
        h = jnp.where(is_meta, meta_ref[rows, :], x_ref[rows, :])
        y = (jax.nn.sigmoid(ga_ref[rows, :]) * _dot(oa_ref[rows, :], wa_ref[...])
             + jax.nn.sigmoid(gb_ref[rows, :]) * _dot(ob_ref[rows, :], wb_ref[...]))
        h1_ref[rows, :] = h + _dot(y.astype(BF16), wo_ref[...])
    _rank_tile(h1_ref[...], gf_ref[...], wrt_ref[...], brt_ref[...], route_ref, gate_ref, cnt_ref, run_scr)


def _merge_call(x2d, meta_tile, oa, ob, ga, gb, wa, wb, wo, gf, wrt, brt):
    BS, D = x2d.shape
    nt = BS // TQ + 1
    R = nt * TQ
    row = lambda w: pl.BlockSpec((TQ, w), lambda i: (i, 0))
    full = lambda a: pl.BlockSpec(a.shape, lambda i: (0, 0))
    return pl.pallas_call(
        _merge_kernel,
        grid=(nt,),
        in_specs=[pl.BlockSpec((TQ, D), lambda i: (jnp.minimum(i, nt - 2), 0)), full(meta_tile),
                  row(oa.shape[1]), row(ob.shape[1]), row(D), row(D), full(wa), full(wb), full(wo),
                  full(gf), full(wrt), full(brt)],
        out_specs=[row(D), pl.BlockSpec((8, TQ), lambda i: (0, i)), row(LANES),
                   pl.BlockSpec((N_EXPERTS, LANES), lambda i: (0, 0))],
        out_shape=[jax.ShapeDtypeStruct((R, D), F32), jax.ShapeDtypeStruct((8, R), jnp.int32),
                   jax.ShapeDtypeStruct((R, LANES), F32), jax.ShapeDtypeStruct((N_EXPERTS, LANES), jnp.int32)],
        scratch_shapes=[pltpu.VMEM((N_EXPERTS, LANES), F32)],
        compiler_params=_params(("arbitrary",)),
        name="merge_route",
    )(x2d, meta_tile, oa, ob, ga, gb, wa, wb, wo, gf, wrt, brt)


_ROUTER_ROWS = 40
TE = 256
_ISSUE_UNROLL = 8


def _route_rows(xn, wrt, brt):
    nr = _ROUTER_ROWS
    tm = xn.shape[0]
    lt = lax.dot_general(wrt, xn, (((1,), (1,)), ((), ())), preferred_element_type=F32,
                         precision=lax.Precision.HIGHEST) + brt
    row = lax.broadcasted_iota(jnp.int32, (nr, tm), 0)
    ninf = -jnp.inf
    cmax = lambda a: jnp.max(a, axis=0, keepdims=True)
    cmin = lambda a: jnp.min(a, axis=0, keepdims=True)
    gl = jnp.where((row >= N_EXPERTS) & (row < N_EXPERTS + N_GROUPS), lt, ninf)
    gmax = cmax(gl)
    gsel = cmin(jnp.where(gl == gmax, row, nr)) - N_EXPERTS
    p_group = 1.0 / jnp.sum(jnp.exp(gl - gmax), axis=0, keepdims=True)
    lo = gsel * EXPERTS_PER_GROUP
    el = jnp.where((row >= lo) & (row < lo + EXPERTS_PER_GROUP), lt, ninf)
    m1 = cmax(el)
    i1 = cmin(jnp.where(el == m1, row, nr))
    el2 = jnp.where(row == i1, ninf, el)
    m2 = cmax(el2)
    i2 = cmin(jnp.where(el2 == m2, row, nr))
    e2 = jnp.exp(m2 - m1)
    return i1, i2, p_group / (1.0 + e2), p_group * e2 / (1.0 + e2)


def _rank_tile(h1, gf, wrt, brt, route_ref, gate_ref, cnt_ref, run_scr):
    @pl.when(pl.program_id(0) == 0)
    def _():
        run_scr[...] = jnp.zeros_like(run_scr)

    tm = h1.shape[0]
    i1, i2, w1, w2 = _route_rows(_rms(h1, gf), wrt, brt)
    row = lax.broadcasted_iota(jnp.int32, (N_EXPERTS, tm), 0)
    o1 = jnp.where(row == i1, 1.0, 0.0)
    o2 = jnp.where(row == i2, 1.0, 0.0)
    a = lax.broadcasted_iota(jnp.int32, (tm, tm), 0)
    b = lax.broadcasted_iota(jnp.int32, (tm, tm), 1)
    before = jnp.where(a < b, 1.0, 0.0).astype(BF16)
    p1 = _dot(o1.astype(BF16), before)
    p2 = _dot(o2.astype(BF16), before)
    run = run_scr[:, 0:1]
    c1 = jnp.sum(o1, axis=1, keepdims=True)
    c2 = jnp.sum(o2, axis=1, keepdims=True)
    r1 = jnp.sum(o1 * (run + p1), axis=0, keepdims=True)
    r2 = jnp.sum(o2 * (run + c1 + p2), axis=0, keepdims=True)
    new_run = run + c1 + c2
    run_scr[...] = jnp.broadcast_to(new_run, run_scr.shape)
    cnt_ref[...] = jnp.broadcast_to(new_run, cnt_ref.shape).astype(jnp.int32)
    z = jnp.zeros((4, tm), jnp.int32)
    route_ref[...] = jnp.concatenate([i1, r1.astype(jnp.int32), i2, r2.astype(jnp.int32), z], axis=0)
    gt = jnp.concatenate([w1, w2, jnp.zeros((LANES - 2, tm), F32)], axis=0)
    gate_ref[...] = gt.T


_PAD_BITS = tuple(1 << b for b in reversed(range(TE.bit_length() - 1)))


def _dispatch_kernel(pos_ref, pad_ref, h1_ref, gf_ref, xs_hbm, buf, zbuf, sem, zsem, *, n_rows):
    i = pl.program_id(0)
    nt = pl.num_programs(0)
    slot = i % 2
    tm = h1_ref.shape[0]

    @pl.when(i == 0)
    def _():
        zbuf[...] = jnp.zeros_like(zbuf)

        def pad_copies(e, wait):
            start, n = pad_ref[e], pad_ref[N_EXPERTS + e]

            def copy(dst, rows):
                cp = pltpu.make_async_copy(zbuf.at[pl.ds(0, rows), :], xs_hbm.at[pl.ds(dst, rows), :], zsem)
                cp.wait() if wait else cp.start()

            end = start + n
            for b in (b for b in _PAD_BITS if b >= 8):
                end = end - (n & b)
                pl.when((n & b) != 0)(functools.partial(copy, pl.multiple_of(end, 8), b))
            for r in range(7):
                pl.when(r < (n & 7))(functools.partial(copy, start + r, 1))

        def tail_copies(blk, wait):
            for h in range(0, TE, zbuf.shape[0]):
                cp = pltpu.make_async_copy(
                    zbuf, xs_hbm.at[pl.ds(pl.multiple_of(blk * TE + h, zbuf.shape[0]), zbuf.shape[0]), :], zsem)
                cp.wait() if wait else cp.start()

        n_used, n_blk = pad_ref[2 * N_EXPERTS], xs_hbm.shape[0] // TE
        for wait in (False, True):
            _loop_plain(N_EXPERTS, lambda e: pad_copies(e, wait))
            lax.fori_loop(n_used, n_blk, lambda b, c: (tail_copies(b, wait), c)[1], 0)

    def wait_slot(s):
        for _ in range(2):
            pltpu.make_async_copy(buf.at[s], xs_hbm.at[pl.ds(0, tm), :], sem.at[s]).wait()

    @pl.when(i >= 2)
    def _():
        wait_slot(slot)

    buf[slot] = _rms(h1_ref[...], gf_ref[...])

    def issue(u, c):
        for v in range(_ISSUE_UNROLL):
            r = u * _ISSUE_UNROLL + v
            src = buf.at[slot, pl.ds(r, 1), :]
            for k in range(2):
                p = pos_ref[k * n_rows + i * tm + r]
                pltpu.make_async_copy(src, xs_hbm.at[pl.ds(p, 1), :], sem.at[slot]).start()
        return c

    lax.fori_loop(0, tm // _ISSUE_UNROLL, issue, 0)

    @pl.when(i == nt - 1)
    def _():
        wait_slot(slot)

        @pl.when(nt >= 2)
        def _():
            wait_slot(1 - slot)


def _loop_plain(n, fn):
    lax.fori_loop(0, n, lambda e, c: (fn(e), c)[1], 0)


def _dispatch_call(pos, pad_runs, h1, gf, n_slots):
    R, D = h1.shape
    return pl.pallas_call(
        functools.partial(_dispatch_kernel, n_rows=R),
        grid_spec=pltpu.PrefetchScalarGridSpec(
            num_scalar_prefetch=2, grid=(R // TQ,),
            in_specs=[pl.BlockSpec((TQ, D), lambda i, pos, pad: (i, 0)),
                      pl.BlockSpec(gf.shape, lambda i, pos, pad: (0, 0))],
            out_specs=pl.BlockSpec(memory_space=pl.ANY),
            scratch_shapes=[pltpu.VMEM((2, TQ, D), F32), pltpu.VMEM((_PAD_BITS[0], D), F32),
                            pltpu.SemaphoreType.DMA((2,)), pltpu.SemaphoreType.DMA]),
        out_shape=jax.ShapeDtypeStruct((n_slots, D), F32),
        compiler_params=_params(("arbitrary",)),
        name="moe_dispatch",
    )(pos, pad_runs, h1, gf)


def _ffn_kernel(be_ref, nu_ref, x_ref, wg_ref, wu_ref, wd_ref, y_ref):
    i = pl.program_id(0)

    @pl.when(i < nu_ref[0])
    def _():
        x = x_ref[...].astype(BF16)
        a = _dot(x, wg_ref[0].astype(BF16))
        u = _dot(x, wu_ref[0].astype(BF16))
        hmid = (a * jax.nn.sigmoid(a) * u).astype(BF16)
        y_ref[...] = _dot(hmid, wd_ref[0].astype(BF16))

    @pl.when(i >= nu_ref[0])
    def _():
        y_ref[...] = jnp.zeros_like(y_ref)


def _ffn_call(blk_expert, n_used, xs, wg, wu, wd):
    NS, D = xs.shape
    wspec = lambda shp: pl.BlockSpec((1,) + shp, lambda i, be, nu: (be[i], 0, 0))
    return pl.pallas_call(
        _ffn_kernel,
        grid_spec=pltpu.PrefetchScalarGridSpec(
            num_scalar_prefetch=2, grid=(NS // TE,),
            in_specs=[pl.BlockSpec((TE, D), lambda i, be, nu: (jnp.minimum(i, nu[0] - 1), 0)),
                      wspec((D, D_EXPERT)), wspec((D, D_EXPERT)), wspec((D_EXPERT, D))],
            out_specs=pl.BlockSpec((TE, D), lambda i, be, nu: (i, 0))),
        out_shape=jax.ShapeDtypeStruct((NS, D), F32),
        compiler_params=_params(("arbitrary",)),
        name="moe_ffn",
    )(blk_expert, n_used, xs, wg, wu, wd)


def _combine_kernel(pos_ref, h1_ref, gate_ref, gfin_ref, ys_hbm, o_ref, buf, sem, *, n_rows):
    i = pl.program_id(0)
    nt = pl.num_programs(0)
    slot = i % 2
    tm = h1_ref.shape[0]

    def fetch(tile, s):
        def issue(u, c):
            for v in range(_ISSUE_UNROLL):
                r = u * _ISSUE_UNROLL + v
                for k in range(2):
                    p = pos_ref[k * n_rows + tile * tm + r]
                    pltpu.make_async_copy(ys_hbm.at[pl.ds(p, 1), :], buf.at[s, k, pl.ds(r, 1), :],
                                          sem.at[s]).start()
            return c
        lax.fori_loop(0, tm // _ISSUE_UNROLL, issue, 0)

    @pl.when(i == 0)
    def _():
        fetch(0, 0)

    @pl.when(i + 1 < nt)
    def _():
        fetch(i + 1, 1 - slot)

    for k in range(2):
        pltpu.make_async_copy(ys_hbm.at[pl.ds(0, tm), :], buf.at[slot, k], sem.at[slot]).wait()

    g = gate_ref[...]
    ffn = g[:, 0:1] * buf[slot, 0] + g[:, 1:2] * buf[slot, 1]

    @pl.when(i < nt - 1)
    def _():
        o_ref[...] = _rms(h1_ref[...] + ffn, gfin_ref[...])


def _combine_call(pos, h1, gates, gfin, ys, n_out):
    R, D = h1.shape
    nt = R // TQ
    return pl.pallas_call(
        functools.partial(_combine_kernel, n_rows=R),
        grid_spec=pltpu.PrefetchScalarGridSpec(
            num_scalar_prefetch=1, grid=(nt,),
            in_specs=[pl.BlockSpec((TQ, D), lambda i, pos: (i, 0)),
                      pl.BlockSpec((TQ, LANES), lambda i, pos: (i, 0)),
                      pl.BlockSpec(gfin.shape, lambda i, pos: (0, 0)),
                      pl.BlockSpec(memory_space=pl.ANY)],
            out_specs=pl.BlockSpec((TQ, D), lambda i, pos: (jnp.minimum(i, nt - 2), 0)),
            scratch_shapes=[pltpu.VMEM((2, 2, TQ, D), F32), pltpu.SemaphoreType.DMA((2,))]),
        out_shape=jax.ShapeDtypeStruct((n_out, D), F32),
        compiler_params=_params(("arbitrary",)),
        name="moe_combine",
    )(pos, h1, gates, gfin, ys)


def _sparse_moe(h1, route, gates, cnt, gf, wg, wu, wd, gfin, n_out):
    R, D = h1.shape
    counts = cnt[:, 0]
    padded = (counts + TE - 1) // TE * TE
    pad_end = jnp.cumsum(padded)
    offs = pad_end - padded
    nb = -(-2 * R // TE) + N_EXPERTS
    eids = jnp.arange(N_EXPERTS, dtype=jnp.int32)
    slot_of = lambda e, r: r + jnp.sum(jnp.where(e[:, None] == eids[None, :], offs[None, :], 0), axis=1)
    pos = jnp.concatenate([slot_of(route[0], route[1]), slot_of(route[2], route[3])]).astype(jnp.int32)
    blk_start = jnp.arange(nb, dtype=jnp.int32) * TE
    blk_expert = jnp.minimum(jnp.sum((blk_start[:, None] >= pad_end[None, :]).astype(jnp.int32), axis=1),
                             N_EXPERTS - 1).astype(jnp.int32)
    n_used = (pad_end[-1] // TE).astype(jnp.int32).reshape(1)
    pad_runs = jnp.concatenate([offs + counts, padded - counts, n_used]).astype(jnp.int32)
    xs = _dispatch_call(pos, pad_runs, h1, gf, nb * TE)
    ys = _ffn_call(blk_expert, n_used, xs, wg, wu, wd)
    return _combine_call(pos, h1, gates, gfin, ys, n_out)


def kernel(x, meta_tokens, norm_mix_g, w_in, mla_cq_norm_g, mla_ckv_norm_g, w_mla_uq, w_mla_ukv,
           w_branch_a, w_branch_b, w_out, rel_bias, norm_ffn_g, w_router_group, b_router_group,
           w_router_expert, b_router_expert, w_exp_gate, w_exp_up, w_exp_down, norm_final_g):
    B, S, D = x.shape
    assert S % TQ == 0 and norm_mix_g.shape[0] == 1
    k_sel = min(K_SEL_MAX, S // 4)
    x2d = x.reshape(B * S, D)
    meta_tile = jnp.concatenate([meta_tokens.astype(x.dtype), jnp.zeros((TQ - N_META, D), x.dtype)], axis=0)

    pos = np.concatenate([N_META + np.arange(S), np.minimum(np.arange(TQ), N_META)]).astype(np.int32)
    ct, st = _rope_tables(jnp.asarray(pos))

    qm, km, vmt, qb, kb, vbt, qi, ki, wi, ga, gb = _proj_call(
        x2d, meta_tile, norm_mix_g.reshape(1, D), _pack_w1(w_in[0]), mla_cq_norm_g.reshape(1, -1),
        mla_ckv_norm_g.reshape(1, -1), _pack_wuq(w_mla_uq[0]), _pack_wukv(w_mla_ukv[0]), ct, st, S)

    o_a = _mla_call(qm, km, vmt, B, S)
    o_b = _dsa_call(qi, wi, qb, ki, kb, vbt, _bias_tiles(rel_bias), B, S, k_sel)

    pad_r = _ROUTER_ROWS - N_EXPERTS - N_GROUPS
    w_r = jnp.concatenate([w_router_expert[0].T, w_router_group[0].T, jnp.zeros((pad_r, D), F32)], axis=0)
    b_r = jnp.concatenate([b_router_expert[0], b_router_group[0],
                           jnp.zeros((pad_r,), F32)]).reshape(_ROUTER_ROWS, 1)
    gf = norm_ffn_g.reshape(1, D)
    h1, route, gates, cnt = _merge_call(x2d, meta_tile, o_a, o_b, ga, gb, w_branch_a[0].astype(BF16),
                                        w_branch_b[0].astype(BF16), w_out[0].astype(BF16), gf, w_r, b_r)
    out = _sparse_moe(h1, route, gates, cnt, gf, w_exp_gate[0], w_exp_up[0], w_exp_down[0],
                      norm_final_g.reshape(1, D), B * S)
    return out.reshape(B, S, D)
```

```python
import functools
import math

import numpy as np
import jax
import jax.numpy as jnp
from jax import lax
from jax.experimental import pallas as pl
from jax.experimental.pallas import tpu as pltpu

CHUNK = 64
N_META = 16
NEG_INF = -1e30
RMS_EPS = 1e-6
ROPE_BASE = 10000.0
MLA_HEADS = 8
MLA_Q_LORA = 256
MLA_KV_LORA = 128
MLA_NOPE = 64
MLA_ROPE = 32
MLA_V = 64
DSA_HEADS = 8
DSA_HEAD_DIM = 64
IDX_HEADS = 8
IDX_DIM = 64
K_SEL_MAX = 256
REL_BUCKETS = 32
REL_MAX_DIST = 128
N_GROUPS = 4
EXPERTS_PER_GROUP = 8
N_EXPERTS = N_GROUPS * EXPERTS_PER_GROUP
D_EXPERT = 256

LANES = 128
TQ = 256
HEAD_PAD = 128
VMEM_LIMIT = 56 * 1024 * 1024
INT_MIN = -2 ** 31
LOG2E = math.log2(math.e)

F32 = jnp.float32
BF16 = jnp.bfloat16


def _params(sem):
    return pltpu.CompilerParams(dimension_semantics=sem, vmem_limit_bytes=VMEM_LIMIT)


def _rms(x, g):
    return x * lax.rsqrt(jnp.mean(x * x, axis=-1, keepdims=True) + RMS_EPS) * g


def _dot(a, b):
    return jnp.dot(a, b, preferred_element_type=F32)


def _dot_t(a, b):
    return lax.dot_general(a, b, (((1,), (1,)), ((), ())), preferred_element_type=F32)


_C_CQ = (0, 256)
_C_CKV = (256, 384)
_C_KRX = (384, 512)
_C_KRY = (512, 640)
_C_QB = (640, 1152)
_C_KB = (1152, 1664)
_C_VB = (1664, 2176)
_C_QI = (2176, 2688)
_C_KI = (2688, 2816)
_C_WI = (2816, 2944)
_C_GA = (2944, 3968)
_C_GB = (3968, 4992)
_W1_COLS = 4992


def _proj_kernel(x_ref, meta_ref, g_ref, w1_ref, gcq_ref, gckv_ref, wuq_ref, wukv_ref, ct_ref, st_ref,
                 qm_ref, km_ref, vm_ref, qb_ref, kb_ref, vb_ref, qi_ref, ki_ref, wi_ref,
                 ga_ref, gb_ref):
    is_meta = pl.program_id(0) == pl.num_programs(0) - 1
    h = jnp.where(is_meta, meta_ref[...], x_ref[...])
    xb = _rms(h, g_ref[...]).astype(BF16)

    def seg(c):
        return _dot(xb, w1_ref[:, c[0]:c[1]])

    ct = ct_ref[...]
    st = st_ref[...]
    nq = _rms(seg(_C_CQ), gcq_ref[...]).astype(BF16)
    qa = _dot(nq, wuq_ref[...])
    half = MLA_HEADS * HEAD_PAD
    for hd in range(MLA_HEADS):
        lo, hi = hd * HEAD_PAD, (hd + 1) * HEAD_PAD
        qm_ref[:, lo:hi] = (qa[:, lo:hi] * ct + qa[:, half + lo:half + hi] * st).astype(BF16)
    nkv = _rms(seg(_C_CKV), gckv_ref[...]).astype(BF16)
    kva = _dot(nkv, wukv_ref[...])
    kr = seg(_C_KRX) * ct + seg(_C_KRY) * st
    for hd in range(MLA_HEADS):
        lo, hi = hd * HEAD_PAD, (hd + 1) * HEAD_PAD
        km_ref[:, lo:hi] = (kva[:, lo:hi] + kr).astype(BF16)
    vm_ref[0] = kva[:, half:].T.astype(BF16)
    qb_ref[...] = seg(_C_QB).astype(BF16)
    kb_ref[...] = seg(_C_KB).astype(BF16)
    vb_ref[0] = seg(_C_VB).T.astype(BF16)
    qi_ref[...] = seg(_C_QI).astype(BF16)
    ki_ref[...] = seg(_C_KI).astype(BF16)
    wi_ref[...] = seg(_C_WI)
    ga_ref[...] = seg(_C_GA)
    gb_ref[...] = seg(_C_GB)


def _pack_w1(w_in):
    D = w_in.shape[0]
    offs = np.cumsum([0, MLA_Q_LORA, MLA_KV_LORA, MLA_ROPE, 512, 512, 512, 512, IDX_DIM, IDX_HEADS, D, D])
    c = lambda i: w_in[:, offs[i]:offs[i + 1]]
    z = lambda n: jnp.zeros((D, n), w_in.dtype)
    k_r = c(2)
    hr = MLA_ROPE // 2
    k_r_sw = jnp.concatenate([k_r[:, hr:], k_r[:, :hr]], axis=1)
    cols = [
        c(0), c(1),
        z(MLA_NOPE), k_r, z(LANES - MLA_NOPE - MLA_ROPE),
        z(MLA_NOPE), k_r_sw, z(LANES - MLA_NOPE - MLA_ROPE),
        c(3), c(4), c(5),
        c(6) * (IDX_DIM ** -0.5),
        c(7), c(7),
        c(8), z(LANES - IDX_HEADS),
        c(9), c(10),
    ]
    w1 = jnp.concatenate(cols, axis=1)
    assert w1.shape[1] == _W1_COLS
    return w1.astype(BF16)


def _pack_wuq(w_uq):
    R = w_uq.shape[0]
    w = w_uq.reshape(R, MLA_HEADS, MLA_NOPE + MLA_ROPE)
    nope, rope = w[..., :MLA_NOPE], w[..., MLA_NOPE:]
    hr = MLA_ROPE // 2
    rope_sw = jnp.concatenate([rope[..., hr:], rope[..., :hr]], axis=-1)
    zt = jnp.zeros((R, MLA_HEADS, HEAD_PAD - MLA_NOPE - MLA_ROPE), w.dtype)
    a = jnp.concatenate([nope, rope, zt], axis=-1).reshape(R, MLA_HEADS * HEAD_PAD)
    b = jnp.concatenate([jnp.zeros_like(nope), rope_sw, zt], axis=-1).reshape(R, MLA_HEADS * HEAD_PAD)
    return jnp.concatenate([a, b], axis=1).astype(BF16)


def _pack_wukv(w_ukv):
    R = w_ukv.shape[0]
    w = w_ukv.reshape(R, MLA_HEADS, MLA_NOPE + MLA_V)
    k = jnp.concatenate([w[..., :MLA_NOPE], jnp.zeros((R, MLA_HEADS, HEAD_PAD - MLA_NOPE), w.dtype)], axis=-1)
    v = w[..., MLA_NOPE:]
    return jnp.concatenate([k.reshape(R, -1), v.reshape(R, -1)], axis=1).astype(BF16)


def _rope_tables(pos):
    half = MLA_ROPE // 2
    inv = ROPE_BASE ** (-jnp.arange(half, dtype=F32) / half)
    ang = pos.astype(F32)[:, None] * inv[None, :]
    cos, sin = jnp.cos(ang), jnp.sin(ang)
    n = pos.shape[0]
    tail = jnp.zeros((n, HEAD_PAD - MLA_NOPE - MLA_ROPE), F32)
    ct = jnp.concatenate([jnp.ones((n, MLA_NOPE), F32), cos, cos, tail], axis=1)
    st = jnp.concatenate([jnp.zeros((n, MLA_NOPE), F32), -sin, sin, tail], axis=1)
    return ct, st


def _proj_call(x2d, meta_tile, g, w1, gcq, gckv, wuq, wukv, ct, st, S):
    BS, D = x2d.shape
    nt = BS // TQ + 1
    per_b = S // TQ
    row = lambda w: pl.BlockSpec((TQ, w), lambda i: (i, 0))
    full = lambda a: pl.BlockSpec(a.shape, lambda i: (0, 0))
    tab = pl.BlockSpec((TQ, HEAD_PAD), lambda i: (jnp.where(i == nt - 1, per_b, i % per_b), 0))
    widths = [(MLA_HEADS * HEAD_PAD, BF16), (MLA_HEADS * HEAD_PAD, BF16), (None, BF16),
              (512, BF16), (512, BF16), (None, BF16), (512, BF16), (LANES, BF16), (LANES, F32),
              (D, F32), (D, F32)]
    vt_rows = MLA_HEADS * MLA_V
    vt_spec = pl.BlockSpec((1, vt_rows, TQ), lambda i: (i, 0, 0))
    vt_shape = jax.ShapeDtypeStruct((nt, vt_rows, TQ), BF16)
    return pl.pallas_call(
        _proj_kernel,
        grid=(nt,),
        in_specs=[pl.BlockSpec((TQ, D), lambda i: (jnp.minimum(i, nt - 2), 0)), full(meta_tile),
                  full(g), full(w1), full(gcq), full(gckv), full(wuq), full(wukv), tab, tab],
        out_specs=[vt_spec if w is None else row(w) for w, _ in widths],
        out_shape=[vt_shape if w is None else jax.ShapeDtypeStruct((nt * TQ, w), dt) for w, dt in widths],
        compiler_params=_params(("parallel",)),
        name="proj",
    )(x2d, meta_tile, g, w1, gcq, gckv, wuq, wukv, ct, st)


def _colmax(z):
    return jnp.max(z, axis=0, keepdims=True)


def _pv(vt, z, m, l, acc):
    p = jnp.exp2(z - m)
    return l + jnp.sum(p, axis=0, keepdims=True), acc + _dot(vt, p.astype(BF16))


def _tile_ids(nq):
    g = pl.program_id(1)
    is_meta = g == pl.num_programs(1) - 1
    return is_meta, jnp.where(is_meta, 0, g % nq)


def _chunk_mask():
    key = lax.broadcasted_iota(jnp.int32, (TQ, TQ), 0)
    qry = lax.broadcasted_iota(jnp.int32, (TQ, TQ), 1)
    shift = CHUNK.bit_length() - 1
    return (key >> shift) <= (qry >> shift)


_CHUNKS = (8, 4, 2, 1)


def _fori_chunks(lo, hi, body_n, carry):
    for n in _CHUNKS:
        cnt = (hi - lo) // n
        carry = lax.fori_loop(0, cnt, lambda u, c, lo=lo, n=n: body_n(lo + n * u, c, n), carry)
        lo = lo + cnt * n
    return carry


def _fori_each(lo, hi, body, carry):
    def body_n(t, c, n):
        for i in range(n):
            c = body(t + i, c)
        return c
    return _fori_chunks(lo, hi, body_n, carry)


def _loop(lo, hi, fn):
    _fori_each(lo, hi, lambda t, c: (fn(t), c)[1], 0)


def _mla_kernel(q_ref, k_ref, vt_ref, km_ref, vtm_ref, o_ref, s_scr, sm_scr, *, nq):
    is_meta, j = _tile_ids(nq)
    n_full = jnp.where(is_meta, 0, j)
    n_vis = jnp.where(is_meta, 0, j + 1)
    c = (MLA_NOPE + MLA_ROPE) ** -0.5 * LOG2E
    meta_ok = lax.broadcasted_iota(jnp.int32, (LANES, TQ), 0) < N_META
    diag_ok = _chunk_mask()
    heads = lambda p: (2 * p, 2 * p + 1)
    hcol = lambda h: slice(h * HEAD_PAD, (h + 1) * HEAD_PAD)
    vrow = lambda h: slice(h * MLA_V, (h + 1) * MLA_V)

    def score_meta(p):
        ms = []
        for hh, h in enumerate(heads(p)):
            z = jnp.where(meta_ok, _dot_t(km_ref[:, hcol(h)], q_ref[:, hcol(h)]) * c, NEG_INF)
            sm_scr[p % 2, hh] = z
            ms.append(_colmax(z))
        return tuple(ms)

    def score_tiles(p, t, ms, n, masked):
        ks = pl.multiple_of(t * TQ, TQ)
        out = []
        for hh, h in enumerate(heads(p)):
            z = _dot_t(k_ref[pl.ds(ks, n * TQ), hcol(h)], q_ref[:, hcol(h)]) * c
            if masked:
                z = jnp.where(diag_ok, z, NEG_INF)
            for i in range(n):
                s_scr[p % 2, hh, t + i] = z[i * TQ:(i + 1) * TQ]
            out.append(jnp.maximum(ms[hh], _colmax(z)))
        return tuple(out)

    def pv_meta(p, ms):
        st = []
        for hh, h in enumerate(heads(p)):
            st += list(_pv(vtm_ref[0, vrow(h), 0:LANES], sm_scr[p % 2, hh], ms[hh],
                           jnp.zeros((1, TQ), F32), jnp.zeros((MLA_V, TQ), F32)))
        return tuple(st)

    def pv_tiles(p, t, st, ms, n):
        for i in range(n):
            out = []
            for hh, h in enumerate(heads(p)):
                out += list(_pv(vt_ref[t + i, vrow(h), :], s_scr[p % 2, hh, t + i], ms[hh],
                                st[2 * hh], st[2 * hh + 1]))
            st = tuple(out)
        return st

    pairs = MLA_HEADS // 2
    ms = score_meta(0)
    ms = _fori_chunks(0, n_full, lambda t, m, n: score_tiles(0, t, m, n, False), ms)
    ms = lax.fori_loop(n_full, n_vis, lambda t, m: score_tiles(0, t, m, 1, True), ms)
    for p in range(pairs):
        st = pv_meta(p, ms)
        if p + 1 < pairs:
            ms_next = score_meta(p + 1)
            st, ms_next = _fori_chunks(
                0, n_full, lambda t, c, n, p=p, ms=ms: (pv_tiles(p, t, c[0], ms, n),
                                                        score_tiles(p + 1, t, c[1], n, False)), (st, ms_next))
            st, ms_next = lax.fori_loop(
                n_full, n_vis, lambda t, c, p=p, ms=ms: (pv_tiles(p, t, c[0], ms, 1),
                                                         score_tiles(p + 1, t, c[1], 1, True)), (st, ms_next))
        else:
            st = _fori_chunks(0, n_vis, lambda t, s, n, p=p, ms=ms: pv_tiles(p, t, s, ms, n), st)
            ms_next = None
        ot = jnp.concatenate([st[1] / st[0], st[3] / st[2]], axis=0)
        o_ref[:, p * 2 * MLA_V:(p + 1) * 2 * MLA_V] = ot.T.astype(BF16)
        ms = ms_next


def _mla_call(qm, km, vmt, B, S):
    nq = S // TQ
    ng = B * nq + 1
    bidx = lambda g: jnp.minimum(g // nq, B - 1)
    mrow = B * S // LANES
    wk, wv = MLA_HEADS * HEAD_PAD, MLA_HEADS * MLA_V
    return pl.pallas_call(
        functools.partial(_mla_kernel, nq=nq),
        grid=(1, ng),
        in_specs=[
            pl.BlockSpec((TQ, wk), lambda _, g: (g, 0)),
            pl.BlockSpec((S, wk), lambda _, g: (bidx(g), 0)),
            pl.BlockSpec((nq, wv, TQ), lambda _, g: (bidx(g), 0, 0)),
            pl.BlockSpec((LANES, wk), lambda _, g: (mrow, 0)),
            pl.BlockSpec((1, wv, TQ), lambda _, g: (ng - 1, 0, 0)),
        ],
        out_specs=pl.BlockSpec((TQ, wv), lambda _, g: (g, 0)),
        out_shape=jax.ShapeDtypeStruct((ng * TQ, wv), BF16),
        scratch_shapes=[pltpu.VMEM((2, 2, nq, TQ, TQ), F32),
                        pltpu.VMEM((2, 2, LANES, TQ), F32)],
        compiler_params=_params(("arbitrary", "arbitrary")),
        name="mla_attn",
    )(qm, km, vmt, km, vmt)


_IDX_BITS = 13
_B_DIAG, _B_PREV, _B_META_FIRST, _B_META_META = 0, 1, 2, 3


def _dsa_kernel(qi_ref, wi_ref, qb_ref, ki_ref, kb_ref, vbt_ref, kim_ref, kbm_ref, vbtm_ref, bias_ref, o_ref,
                keys_scr, keym_scr, selm_scr, s_scr, sm_scr, qs_scr,
                dig_scr, dm_scr, *, nq, k_sel):
    is_meta, j = _tile_ids(nq)
    n_full = jnp.where(is_meta, 0, j)
    n_vis = jnp.where(is_meta, 0, j + 1)
    lane = lax.broadcasted_iota(jnp.int32, (TQ, LANES), 1)
    low = lane < DSA_HEAD_DIM
    krow_m = lax.broadcasted_iota(jnp.int32, (LANES, TQ), 0)
    krow = lax.broadcasted_iota(jnp.int32, (TQ, TQ), 0)
    meta_ok = krow_m < N_META
    diag_ok = _chunk_mask()

    for p in range(IDX_HEADS // 2):
        qp = qi_ref[:, p * LANES:(p + 1) * LANES]
        qs_scr[2 * p] = jnp.where(low, qp, jnp.zeros_like(qp))
        qs_scr[2 * p + 1] = jnp.where(low, jnp.zeros_like(qp), qp)
    wt = (wi_ref[...] * (IDX_HEADS ** -0.5)).T

    def sort_key(score):
        bits = pltpu.bitcast(score + 0.0, jnp.int32)
        return bits ^ ((bits >> 31) & 0x7FFFFFFF)

    def score_of(kt):
        score = jnp.zeros((kt.shape[0], TQ), F32)
        for h in range(IDX_HEADS):
            score = score + wt[h:h + 1, :] * jnp.maximum(_dot_t(kt, qs_scr[h]), 0.0)
        return sort_key(score)

    keym_scr[...] = jnp.where(meta_ok, score_of(kim_ref[...]), INT_MIN)

    def score_tiles(t, n, masked):
        key = score_of(ki_ref[pl.ds(pl.multiple_of(t * TQ, TQ), n * TQ), :])
        if masked:
            key = jnp.where(diag_ok, key, INT_MIN)
        for i in range(n):
            keys_scr[t + i] = key[i * TQ:(i + 1) * TQ]

    _fori_chunks(0, n_full, lambda t, c, n: (score_tiles(t, n, False), c)[1], 0)
    lax.fori_loop(n_full, n_vis, lambda t, c: (score_tiles(t, 1, True), c)[1], 0)

    one = lambda m: jnp.where(m, 1, 0)
    fold = lambda a: jnp.sum(a.reshape(-1, 8, TQ), axis=0)

    def count(pred):
        def cbody(t, acc):
            return acc + fold(pred(keys_scr[t], LANES + t * TQ + krow))
        acc = _fori_each(0, n_vis, cbody, fold(pred(keym_scr[...], krow_m)))
        return jnp.sum(acc, axis=0, keepdims=True)

    def fold16(m):
        parts = [m[r:r + 16] for r in range(0, m.shape[0], 16)]
        while len(parts) > 1:
            parts = [a + b for a, b in zip(parts[::2], parts[1::2])]
        return parts[0].astype(F32)

    one16 = lambda m: jnp.where(m, jnp.ones((), BF16), jnp.zeros((), BF16))

    def count16(meta_scr, tile_scr, pred):
        acc = _fori_each(0, n_vis, lambda t, a: a + fold16(one16(pred(tile_scr[t]))),
                         fold16(one16(pred(meta_scr[...]))))
        return jnp.sum(acc, axis=0, keepdims=True)

    def digit_search(d_prev, shift, need):
        top, last = shift == 24, shift == 0

        def digits(key, prev):
            d = ((key >> shift) + 128 if top else (key >> shift) & 0xFF).astype(F32).astype(BF16)
            if top:
                return d
            return jnp.where(prev == d_prev.astype(F32).astype(BF16), d, -jnp.ones((), BF16))

        dm_scr[...] = digits(keym_scr[...], None if top else dm_scr[...])
        _loop(0, n_vis, lambda t: dig_scr.__setitem__(t, digits(keys_scr[t], None if top else dig_scr[t])))

        def d_body(i, c):
            d, above, at_least = c
            cand = d + lax.shift_left(jnp.int32(1), 7 - i)
            cb = cand.astype(F32).astype(BF16)
            cnt = count16(dm_scr, dig_scr, lambda v: v >= cb)
            ok = cnt >= need
            return jnp.where(ok, cand, d), jnp.where(ok, above, cnt), jnp.where(ok, cnt, at_least)

        zero = jnp.zeros((1, TQ), F32)
        all_cands = count16(dm_scr, dig_scr, lambda v: v >= jnp.zeros((), BF16)) if last else zero
        return lax.fori_loop(0, 8, d_body, (jnp.zeros((1, TQ), jnp.int32), zero, all_cands))

    t = jnp.zeros((1, TQ), jnp.int32)
    need_d = jnp.full((1, TQ), float(k_sel), F32)
    d = None
    for shift in (24, 16, 8, 0):
        d, above, at_least = digit_search(d, shift, need_d)
        t = (d - 128) if shift == 24 else (t << 8) | d
        need_d = need_d - above
    need = need_d.astype(jnp.int32)
    cnt_ge = k_sel - need + (at_least - above).astype(jnp.int32)

    tied = jnp.max(jnp.where((cnt_ge > k_sel) & (t != INT_MIN), 1, 0))

    def write_masks(sel_mask):
        selm_scr[...] = sel_mask(keym_scr[...], krow_m)
        _loop(0, n_vis, lambda tt: keys_scr.__setitem__(
            tt, pltpu.bitcast(sel_mask(keys_scr[tt], LANES + tt * TQ + krow), jnp.int32)))

    @pl.when(tied == 0)
    def _():
        t_min = jnp.maximum(t, INT_MIN + 1)
        write_masks(lambda b, _: jnp.where(b >= t_min, 0.0, NEG_INF))

    @pl.when(tied > 0)
    def _():
        def tie_body(i, r):
            cand = r | lax.shift_left(jnp.int32(1), _IDX_BITS - 1 - i)
            f = count(lambda b, idx: jnp.where(b == t, one(idx < cand), 0))
            return jnp.where(f < need, cand, r)
        mstar = lax.fori_loop(0, _IDX_BITS, tie_body, jnp.zeros((1, TQ), jnp.int32)) + 1

        def sel_mask(b, idx):
            tie_ok = jnp.where(b == t, one(idx < mstar), 0)
            sel = jnp.where(b == INT_MIN, 0, jnp.where(b > t, 1, tie_ok))
            return jnp.where(sel > 0, 0.0, NEG_INF)

        write_masks(sel_mask)

    c = DSA_HEAD_DIM ** -0.5 * LOG2E
    n_far = jnp.maximum(n_full - 1, 0)
    use_mb = jnp.where(is_meta | (j == 0), 1.0, 0.0)
    mb_idx = jnp.where(is_meta, _B_META_META, _B_META_FIRST)
    pcols = lambda p: slice(p * LANES, (p + 1) * LANES)
    vrow = lambda h: slice(h * DSA_HEAD_DIM, (h + 1) * DSA_HEAD_DIM)

    def masked_q(p):
        qp = qb_ref[:, pcols(p)]
        return [jnp.where(low, qp, jnp.zeros_like(qp)), jnp.where(low, jnp.zeros_like(qp), qp)]

    def score_meta(p):
        qs = masked_q(p)
        ms = []
        for hh in range(2):
            z = (_dot_t(kbm_ref[:, pcols(p)], qs[hh]) * c + bias_ref[2 * p + hh, mb_idx][:LANES, :] * use_mb
                 + selm_scr[...])
            sm_scr[p % 2, hh] = z
            ms.append(_colmax(z))
        return tuple(ms)

    def score_tiles(p, t, ms, n, near):
        qs = masked_q(p)
        k = kb_ref[pl.ds(pl.multiple_of(t * TQ, TQ), n * TQ), pcols(p)]
        out = []
        for hh in range(2):
            zz = _dot_t(k, qs[hh]) * c
            m = ms[hh]
            for i in range(n):
                z = zz[i * TQ:(i + 1) * TQ] + pltpu.bitcast(keys_scr[t + i], F32)
                if near is not None:
                    z = z + bias_ref[2 * p + hh, near]
                s_scr[p % 2, hh, t + i] = z
                m = jnp.maximum(m, _colmax(z))
            out.append(m)
        return tuple(out)

    def pv_meta(p, ms):
        st = []
        for hh in range(2):
            st += list(_pv(vbtm_ref[0, vrow(2 * p + hh), 0:LANES], sm_scr[p % 2, hh], ms[hh],
                           jnp.zeros((1, TQ), F32), jnp.zeros((DSA_HEAD_DIM, TQ), F32)))
        return tuple(st)

    def pv_tiles(p, t, st, ms, n):
        for i in range(n):
            out = []
            for hh in range(2):
                out += list(_pv(vbt_ref[t + i, vrow(2 * p + hh), :], s_scr[p % 2, hh, t + i], ms[hh],
                                st[2 * hh], st[2 * hh + 1]))
            st = tuple(out)
        return st

    def score_all(p, ms):
        ms = _fori_chunks(0, n_far, lambda t, m, n: score_tiles(p, t, m, n, None), ms)
        ms = lax.fori_loop(n_far, n_full, lambda t, m: score_tiles(p, t, m, 1, _B_PREV), ms)
        return lax.fori_loop(n_full, n_vis, lambda t, m: score_tiles(p, t, m, 1, _B_DIAG), ms)

    pairs = DSA_HEADS // 2
    ms = score_all(0, score_meta(0))
    for p in range(pairs):
        st = pv_meta(p, ms)
        if p + 1 < pairs:
            both = lambda near, n=1, p=p, ms=ms: (
                lambda t, c, n=n: (pv_tiles(p, t, c[0], ms, n), score_tiles(p + 1, t, c[1], n, near)))
            carry = (st, score_meta(p + 1))
            carry = _fori_chunks(0, n_far, lambda t, c, n, p=p, ms=ms: (
                pv_tiles(p, t, c[0], ms, n), score_tiles(p + 1, t, c[1], n, None)), carry)
            carry = lax.fori_loop(n_far, n_full, both(_B_PREV), carry)
            st, ms_next = lax.fori_loop(n_full, n_vis, both(_B_DIAG), carry)
        else:
            st = _fori_chunks(0, n_vis, lambda t, s, n, p=p, ms=ms: pv_tiles(p, t, s, ms, n), st)
            ms_next = None
        ot = jnp.concatenate([st[1] / st[0], st[3] / st[2]], axis=0)
        o_ref[:, pcols(p)] = ot.T.astype(BF16)
        ms = ms_next


def _dsa_call(qi, wi, qb, ki, kb, vbt, bias, B, S, k_sel):
    nq = S // TQ
    ng = B * nq + 1
    W = DSA_HEADS * DSA_HEAD_DIM
    bidx = lambda g: jnp.minimum(g // nq, B - 1)
    mrow = B * S // LANES
    qrow = lambda w: pl.BlockSpec((TQ, w), lambda _, g: (g, 0))
    kv = lambda w: pl.BlockSpec((S, w), lambda _, g: (bidx(g), 0))
    kvm = lambda w: pl.BlockSpec((LANES, w), lambda _, g: (mrow, 0))
    return pl.pallas_call(
        functools.partial(_dsa_kernel, nq=nq, k_sel=k_sel),
        grid=(1, ng),
        in_specs=[qrow(W), qrow(LANES), qrow(W), kv(LANES), kv(W),
                  pl.BlockSpec((nq, W, TQ), lambda _, g: (bidx(g), 0, 0)),
                  kvm(LANES), kvm(W),
                  pl.BlockSpec((1, W, TQ), lambda _, g: (ng - 1, 0, 0)),
                  pl.BlockSpec(bias.shape, lambda _, g: (0, 0, 0, 0), pipeline_mode=pl.Buffered(1))],
        out_specs=qrow(W),
        out_shape=jax.ShapeDtypeStruct((ng * TQ, W), BF16),
        scratch_shapes=[
            pltpu.VMEM((nq, TQ, TQ), jnp.int32),
            pltpu.VMEM((LANES, TQ), jnp.int32),
            pltpu.VMEM((LANES, TQ), F32),
            pltpu.VMEM((2, 2, nq, TQ, TQ), F32),
            pltpu.VMEM((2, 2, LANES, TQ), F32),
            pltpu.VMEM((IDX_HEADS, TQ, LANES), BF16),
            pltpu.VMEM((nq, TQ, TQ), BF16),
            pltpu.VMEM((LANES, TQ), BF16),
        ],
        compiler_params=_params(("arbitrary", "arbitrary")),
        name="dsa",
    )(qi, wi, qb, ki, kb, vbt, ki, kb, vbt, bias)


def _t5_bucket(rel):
    nb = REL_BUCKETS // 2
    max_exact = nb // 2
    n = jnp.abs(rel)
    large = max_exact + (jnp.log(jnp.maximum(n, 1).astype(F32) / max_exact)
                         / math.log(REL_MAX_DIST / max_exact) * (nb - max_exact)).astype(jnp.int32)
    large = jnp.minimum(large, nb - 1)
    return jnp.where(rel > 0, nb, 0) + jnp.where(n < max_exact, n, large)


def _bias_kernel(bucket_ref, rb_ref, o_ref, *, far_bucket):
    h = pl.program_id(0)
    far = rb_ref[far_bucket, h]
    for tile in range(bucket_ref.shape[0]):
        bkt = bucket_ref[tile]
        acc = jnp.zeros(bkt.shape, F32)
        for b in range(REL_BUCKETS):
            acc = jnp.where(bkt == b, rb_ref[b, h], acc)
        o_ref[0, tile] = (acc - far) * LOG2E


def _bias_tiles(rel_bias):
    k = jnp.arange(TQ, dtype=jnp.int32)[:, None]
    q = jnp.arange(TQ, dtype=jnp.int32)[None, :]
    rels = jnp.stack([k - q, k - q - TQ, k - (q + N_META), k - q])
    far_bucket = REL_BUCKETS // 2 - 1
    return pl.pallas_call(
        functools.partial(_bias_kernel, far_bucket=far_bucket),
        grid=(DSA_HEADS,),
        in_specs=[pl.BlockSpec((4, TQ, TQ), lambda h: (0, 0, 0)),
                  pl.BlockSpec(memory_space=pltpu.SMEM)],
        out_specs=pl.BlockSpec((1, 4, TQ, TQ), lambda h: (h, 0, 0, 0)),
        out_shape=jax.ShapeDtypeStruct((DSA_HEADS, 4, TQ, TQ), F32),
        compiler_params=_params(("parallel",)),
        name="bias_tiles",
    )(_t5_bucket(rels), rel_bias)


_MERGE_ROWS = 128


def _merge_kernel(x_ref, meta_ref, oa_ref, ob_ref, ga_ref, gb_ref, wa_ref, wb_ref, wo_ref, gf_ref, wrt_ref, brt_ref,
                  h1_ref, route_ref, gate_ref, cnt_ref, run_scr):
    is_meta = pl.program_id(0) == pl.num_programs(0) - 1
    for rows in (slice(s, s + _MERGE_ROWS) for s in range(0, x_ref.shape[0], _MERGE_ROWS)):
        h = jnp.where(is_meta, meta_ref[rows, :], x_ref[rows, :])
        y = (jax.nn.sigmoid(ga_ref[rows, :]) * _dot(oa_ref[rows, :], wa_ref[...])
             + jax.nn.sigmoid(gb_ref[rows, :]) * _dot(ob_ref[rows, :], wb_ref[...]))
        h1_ref[rows, :] = h + _dot(y.astype(BF16), wo_ref[...])
    _rank_tile(h1_ref[...], gf_ref[...], wrt_ref[...], brt_ref[...], route_ref, gate_ref, cnt_ref, run_scr)


def _merge_call(x2d, meta_tile, oa, ob, ga, gb, wa, wb, wo, gf, wrt, brt):
    BS, D = x2d.shape
    nt = BS // TQ + 1
    R = nt * TQ
    row = lambda w: pl.BlockSpec((TQ, w), lambda i: (i, 0))
    full = lambda a: pl.BlockSpec(a.shape, lambda i: (0, 0))
    return pl.pallas_call(
        _merge_kernel,
        grid=(nt,),
        in_specs=[pl.BlockSpec((TQ, D), lambda i: (jnp.minimum(i, nt - 2), 0)), full(meta_tile),
                  row(oa.shape[1]), row(ob.shape[1]), row(D), row(D), full(wa), full(wb), full(wo),
                  full(gf), full(wrt), full(brt)],
        out_specs=[row(D), pl.BlockSpec((8, TQ), lambda i: (0, i)), row(LANES),
                   pl.BlockSpec((N_EXPERTS, LANES), lambda i: (0, 0))],
        out_shape=[jax.ShapeDtypeStruct((R, D), F32), jax.ShapeDtypeStruct((8, R), jnp.int32),
                   jax.ShapeDtypeStruct((R, LANES), F32), jax.ShapeDtypeStruct((N_EXPERTS, LANES), jnp.int32)],
        scratch_shapes=[pltpu.VMEM((N_EXPERTS, LANES), F32)],
        compiler_params=_params(("arbitrary",)),
        name="merge_route",
    )(x2d, meta_tile, oa, ob, ga, gb, wa, wb, wo, gf, wrt, brt)


_ROUTER_ROWS = 40
TE = 256
_ISSUE_UNROLL = 8


def _route_rows(xn, wrt, brt):
    nr = _ROUTER_ROWS
    tm = xn.shape[0]
    lt = lax.dot_general(wrt, xn, (((1,), (1,)), ((), ())), preferred_element_type=F32,
                         precision=lax.Precision.HIGHEST) + brt
    row = lax.broadcasted_iota(jnp.int32, (nr, tm), 0)
    ninf = -jnp.inf
    cmax = lambda a: jnp.max(a, axis=0, keepdims=True)
    cmin = lambda a: jnp.min(a, axis=0, keepdims=True)
    gl = jnp.where((row >= N_EXPERTS) & (row < N_EXPERTS + N_GROUPS), lt, ninf)
    gmax = cmax(gl)
    gsel = cmin(jnp.where(gl == gmax, row, nr)) - N_EXPERTS
    p_group = 1.0 / jnp.sum(jnp.exp(gl - gmax), axis=0, keepdims=True)
    lo = gsel * EXPERTS_PER_GROUP
    el = jnp.where((row >= lo) & (row < lo + EXPERTS_PER_GROUP), lt, ninf)
    m1 = cmax(el)
    i1 = cmin(jnp.where(el == m1, row, nr))
    el2 = jnp.where(row == i1, ninf, el)
    m2 = cmax(el2)
    i2 = cmin(jnp.where(el2 == m2, row, nr))
    e2 = jnp.exp(m2 - m1)
    return i1, i2, p_group / (1.0 + e2), p_group * e2 / (1.0 + e2)


def _rank_tile(h1, gf, wrt, brt, route_ref, gate_ref, cnt_ref, run_scr):
    @pl.when(pl.program_id(0) == 0)
    def _():
        run_scr[...] = jnp.zeros_like(run_scr)

    tm = h1.shape[0]
    i1, i2, w1, w2 = _route_rows(_rms(h1, gf), wrt, brt)
    row = lax.broadcasted_iota(jnp.int32, (N_EXPERTS, tm), 0)
    o1 = jnp.where(row == i1, 1.0, 0.0)
    o2 = jnp.where(row == i2, 1.0, 0.0)
    a = lax.broadcasted_iota(jnp.int32, (tm, tm), 0)
    b = lax.broadcasted_iota(jnp.int32, (tm, tm), 1)
    before = jnp.where(a < b, 1.0, 0.0).astype(BF16)
    p1 = _dot(o1.astype(BF16), before)
    p2 = _dot(o2.astype(BF16), before)
    run = run_scr[:, 0:1]
    c1 = jnp.sum(o1, axis=1, keepdims=True)
    c2 = jnp.sum(o2, axis=1, keepdims=True)
    r1 = jnp.sum(o1 * (run + p1), axis=0, keepdims=True)
    r2 = jnp.sum(o2 * (run + c1 + p2), axis=0, keepdims=True)
    new_run = run + c1 + c2
    run_scr[...] = jnp.broadcast_to(new_run, run_scr.shape)
    cnt_ref[...] = jnp.broadcast_to(new_run, cnt_ref.shape).astype(jnp.int32)
    z = jnp.zeros((4, tm), jnp.int32)
    route_ref[...] = jnp.concatenate([i1, r1.astype(jnp.int32), i2, r2.astype(jnp.int32), z], axis=0)
    gt = jnp.concatenate([w1, w2, jnp.zeros((LANES - 2, tm), F32)], axis=0)
    gate_ref[...] = gt.T


_PAD_BITS = tuple(1 << b for b in reversed(range(TE.bit_length() - 1)))


def _dispatch_kernel(pos_ref, pad_ref, h1_ref, gf_ref, xs_hbm, buf, zbuf, sem, zsem, *, n_rows):
    i = pl.program_id(0)
    nt = pl.num_programs(0)
    slot = i % 2
    tm = h1_ref.shape[0]

    @pl.when(i == 0)
    def _():
        zbuf[...] = jnp.zeros_like(zbuf)

        def pad_copies(e, wait):
            start, n = pad_ref[e], pad_ref[N_EXPERTS + e]

            def copy(dst, rows):
                cp = pltpu.make_async_copy(zbuf.at[pl.ds(0, rows), :], xs_hbm.at[pl.ds(dst, rows), :], zsem)
                cp.wait() if wait else cp.start()

            end = start + n
            for b in (b for b in _PAD_BITS if b >= 8):
                end = end - (n & b)
                pl.when((n & b) != 0)(functools.partial(copy, pl.multiple_of(end, 8), b))
            for r in range(7):
                pl.when(r < (n & 7))(functools.partial(copy, start + r, 1))

        def tail_copies(blk, wait):
            for h in range(0, TE, zbuf.shape[0]):
                cp = pltpu.make_async_copy(
                    zbuf, xs_hbm.at[pl.ds(pl.multiple_of(blk * TE + h, zbuf.shape[0]), zbuf.shape[0]), :], zsem)
                cp.wait() if wait else cp.start()

        n_used, n_blk = pad_ref[2 * N_EXPERTS], xs_hbm.shape[0] // TE
        for wait in (False, True):
            _loop_plain(N_EXPERTS, lambda e: pad_copies(e, wait))
            lax.fori_loop(n_used, n_blk, lambda b, c: (tail_copies(b, wait), c)[1], 0)

    def wait_slot(s):
        for _ in range(2):
            pltpu.make_async_copy(buf.at[s], xs_hbm.at[pl.ds(0, tm), :], sem.at[s]).wait()

    @pl.when(i >= 2)
    def _():
        wait_slot(slot)

    buf[slot] = _rms(h1_ref[...], gf_ref[...])

    def issue(u, c):
        for v in range(_ISSUE_UNROLL):
            r = u * _ISSUE_UNROLL + v
            src = buf.at[slot, pl.ds(r, 1), :]
            for k in range(2):
                p = pos_ref[k * n_rows + i * tm + r]
                pltpu.make_async_copy(src, xs_hbm.at[pl.ds(p, 1), :], sem.at[slot]).start()
        return c

    lax.fori_loop(0, tm // _ISSUE_UNROLL, issue, 0)

    @pl.when(i == nt - 1)
    def _():
        wait_slot(slot)

        @pl.when(nt >= 2)
        def _():
            wait_slot(1 - slot)


def _loop_plain(n, fn):
    lax.fori_loop(0, n, lambda e, c: (fn(e), c)[1], 0)


def _dispatch_call(pos, pad_runs, h1, gf, n_slots):
    R, D = h1.shape
    return pl.pallas_call(
        functools.partial(_dispatch_kernel, n_rows=R),
        grid_spec=pltpu.PrefetchScalarGridSpec(
            num_scalar_prefetch=2, grid=(R // TQ,),
            in_specs=[pl.BlockSpec((TQ, D), lambda i, pos, pad: (i, 0)),
                      pl.BlockSpec(gf.shape, lambda i, pos, pad: (0, 0))],
            out_specs=pl.BlockSpec(memory_space=pl.ANY),
            scratch_shapes=[pltpu.VMEM((2, TQ, D), F32), pltpu.VMEM((_PAD_BITS[0], D), F32),
                            pltpu.SemaphoreType.DMA((2,)), pltpu.SemaphoreType.DMA]),
        out_shape=jax.ShapeDtypeStruct((n_slots, D), F32),
        compiler_params=_params(("arbitrary",)),
        name="moe_dispatch",
    )(pos, pad_runs, h1, gf)


def _ffn_kernel(be_ref, nu_ref, x_ref, wg_ref, wu_ref, wd_ref, y_ref):
    i = pl.program_id(0)

    @pl.when(i < nu_ref[0])
    def _():
        x = x_ref[...].astype(BF16)
        a = _dot(x, wg_ref[0].astype(BF16))
        u = _dot(x, wu_ref[0].astype(BF16))
        hmid = (a * jax.nn.sigmoid(a) * u).astype(BF16)
        y_ref[...] = _dot(hmid, wd_ref[0].astype(BF16))

    @pl.when(i >= nu_ref[0])
    def _():
        y_ref[...] = jnp.zeros_like(y_ref)


def _ffn_call(blk_expert, n_used, xs, wg, wu, wd):
    NS, D = xs.shape
    wspec = lambda shp: pl.BlockSpec((1,) + shp, lambda i, be, nu: (be[i], 0, 0))
    return pl.pallas_call(
        _ffn_kernel,
        grid_spec=pltpu.PrefetchScalarGridSpec(
            num_scalar_prefetch=2, grid=(NS // TE,),
            in_specs=[pl.BlockSpec((TE, D), lambda i, be, nu: (jnp.minimum(i, nu[0] - 1), 0)),
                      wspec((D, D_EXPERT)), wspec((D, D_EXPERT)), wspec((D_EXPERT, D))],
            out_specs=pl.BlockSpec((TE, D), lambda i, be, nu: (i, 0))),
        out_shape=jax.ShapeDtypeStruct((NS, D), F32),
        compiler_params=_params(("arbitrary",)),
        name="moe_ffn",
    )(blk_expert, n_used, xs, wg, wu, wd)


def _combine_kernel(pos_ref, h1_ref, gate_ref, gfin_ref, ys_hbm, o_ref, buf, sem, *, n_rows):
    i = pl.program_id(0)
    nt = pl.num_programs(0)
    slot = i % 2
    tm = h1_ref.shape[0]

    def fetch(tile, s):
        def issue(u, c):
            for v in range(_ISSUE_UNROLL):
                r = u * _ISSUE_UNROLL + v
                for k in range(2):
                    p = pos_ref[k * n_rows + tile * tm + r]
                    pltpu.make_async_copy(ys_hbm.at[pl.ds(p, 1), :], buf.at[s, k, pl.ds(r, 1), :],
                                          sem.at[s]).start()
            return c
        lax.fori_loop(0, tm // _ISSUE_UNROLL, issue, 0)

    @pl.when(i == 0)
    def _():
        fetch(0, 0)

    @pl.when(i + 1 < nt)
    def _():
        fetch(i + 1, 1 - slot)

    for k in range(2):
        pltpu.make_async_copy(ys_hbm.at[pl.ds(0, tm), :], buf.at[slot, k], sem.at[slot]).wait()

    g = gate_ref[...]
    ffn = g[:, 0:1] * buf[slot, 0] + g[:, 1:2] * buf[slot, 1]

    @pl.when(i < nt - 1)
    def _():
        o_ref[...] = _rms(h1_ref[...] + ffn, gfin_ref[...])


def _combine_call(pos, h1, gates, gfin, ys, n_out):
    R, D = h1.shape
    nt = R // TQ
    return pl.pallas_call(
        functools.partial(_combine_kernel, n_rows=R),
        grid_spec=pltpu.PrefetchScalarGridSpec(
            num_scalar_prefetch=1, grid=(nt,),
            in_specs=[pl.BlockSpec((TQ, D), lambda i, pos: (i, 0)),
                      pl.BlockSpec((TQ, LANES), lambda i, pos: (i, 0)),
                      pl.BlockSpec(gfin.shape, lambda i, pos: (0, 0)),
                      pl.BlockSpec(memory_space=pl.ANY)],
            out_specs=pl.BlockSpec((TQ, D), lambda i, pos: (jnp.minimum(i, nt - 2), 0)),
            scratch_shapes=[pltpu.VMEM((2, 2, TQ, D), F32), pltpu.SemaphoreType.DMA((2,))]),
        out_shape=jax.ShapeDtypeStruct((n_out, D), F32),
        compiler_params=_params(("arbitrary",)),
        name="moe_combine",
    )(pos, h1, gates, gfin, ys)


def _sparse_moe(h1, route, gates, cnt, gf, wg, wu, wd, gfin, n_out):
    R, D = h1.shape
    counts = cnt[:, 0]
    padded = (counts + TE - 1) // TE * TE
    pad_end = jnp.cumsum(padded)
    offs = pad_end - padded
    nb = -(-2 * R // TE) + N_EXPERTS
    eids = jnp.arange(N_EXPERTS, dtype=jnp.int32)
    slot_of = lambda e, r: r + jnp.sum(jnp.where(e[:, None] == eids[None, :], offs[None, :], 0), axis=1)
    pos = jnp.concatenate([slot_of(route[0], route[1]), slot_of(route[2], route[3])]).astype(jnp.int32)
    blk_start = jnp.arange(nb, dtype=jnp.int32) * TE
    blk_expert = jnp.minimum(jnp.sum((blk_start[:, None] >= pad_end[None, :]).astype(jnp.int32), axis=1),
                             N_EXPERTS - 1).astype(jnp.int32)
    n_used = (pad_end[-1] // TE).astype(jnp.int32).reshape(1)
    pad_runs = jnp.concatenate([offs + counts, padded - counts, n_used]).astype(jnp.int32)
    xs = _dispatch_call(pos, pad_runs, h1, gf, nb * TE)
    ys = _ffn_call(blk_expert, n_used, xs, wg, wu, wd)
    return _combine_call(pos, h1, gates, gfin, ys, n_out)


def kernel(x, meta_tokens, norm_mix_g, w_in, mla_cq_norm_g, mla_ckv_norm_g, w_mla_uq, w_mla_ukv,
           w_branch_a, w_branch_b, w_out, rel_bias, norm_ffn_g, w_router_group, b_router_group,
           w_router_expert, b_router_expert, w_exp_gate, w_exp_up, w_exp_down, norm_final_g):
    B, S, D = x.shape
    assert S % TQ == 0 and norm_mix_g.shape[0] == 1
    k_sel = min(K_SEL_MAX, S // 4)
    x2d = x.reshape(B * S, D)
    meta_tile = jnp.concatenate([meta_tokens.astype(x.dtype), jnp.zeros((TQ - N_META, D), x.dtype)], axis=0)

    pos = np.concatenate([N_META + np.arange(S), np.minimum(np.arange(TQ), N_META)]).astype(np.int32)
    ct, st = _rope_tables(jnp.asarray(pos))

    qm, km, vmt, qb, kb, vbt, qi, ki, wi, ga, gb = _proj_call(
        x2d, meta_tile, norm_mix_g.reshape(1, D), _pack_w1(w_in[0]), mla_cq_norm_g.reshape(1, -1),
        mla_ckv_norm_g.reshape(1, -1), _pack_wuq(w_mla_uq[0]), _pack_wukv(w_mla_ukv[0]), ct, st, S)

    o_a = _mla_call(qm, km, vmt, B, S)
    o_b = _dsa_call(qi, wi, qb, ki, kb, vbt, _bias_tiles(rel_bias), B, S, k_sel)

    pad_r = _ROUTER_ROWS - N_EXPERTS - N_GROUPS
    w_r = jnp.concatenate([w_router_expert[0].T, w_router_group[0].T, jnp.zeros((pad_r, D), F32)], axis=0)
    b_r = jnp.concatenate([b_router_expert[0], b_router_group[0],
                           jnp.zeros((pad_r,), F32)]).reshape(_ROUTER_ROWS, 1)
    gf = norm_ffn_g.reshape(1, D)
    h1, route, gates, cnt = _merge_call(x2d, meta_tile, o_a, o_b, ga, gb, w_branch_a[0].astype(BF16),
                                        w_branch_b[0].astype(BF16), w_out[0].astype(BF16), gf, w_r, b_r)
    out = _sparse_moe(h1, route, gates, cnt, gf, w_exp_gate[0], w_exp_up[0], w_exp_down[0],
                      norm_final_g.reshape(1, D), B * S)
    return out.reshape(B, S, D)
```

```python
import functools
import math

import numpy as np
import jax
import jax.numpy as jnp
from jax import lax
from jax.experimental import pallas as pl
from jax.experimental.pallas import tpu as pltpu

CHUNK = 64
N_META = 16
NEG_INF = -1e30
RMS_EPS = 1e-6
ROPE_BASE = 10000.0
MLA_HEADS = 8
MLA_Q_LORA = 256
MLA_KV_LORA = 128
MLA_NOPE = 64
MLA_ROPE = 32
MLA_V = 64
DSA_HEADS = 8
DSA_HEAD_DIM = 64
IDX_HEADS = 8
IDX_DIM = 64
K_SEL_MAX = 256
REL_BUCKETS = 32
REL_MAX_DIST = 128
N_GROUPS = 4
EXPERTS_PER_GROUP = 8
N_EXPERTS = N_GROUPS * EXPERTS_PER_GROUP
D_EXPERT = 256

LANES = 128
TQ = 256
HEAD_PAD = 128
VMEM_LIMIT = 56 * 1024 * 1024
INT_MIN = -2 ** 31
LOG2E = math.log2(math.e)

F32 = jnp.float32
BF16 = jnp.bfloat16


def _params(sem):
    return pltpu.CompilerParams(dimension_semantics=sem, vmem_limit_bytes=VMEM_LIMIT)


def _rms(x, g):
    return x * lax.rsqrt(jnp.mean(x * x, axis=-1, keepdims=True) + RMS_EPS) * g


def _dot(a, b):
    return jnp.dot(a, b, preferred_element_type=F32)


def _dot_t(a, b):
    return lax.dot_general(a, b, (((1,), (1,)), ((), ())), preferred_element_type=F32)


_C_CQ = (0, 256)
_C_CKV = (256, 384)
_C_KRX = (384, 512)
_C_KRY = (512, 640)
_C_QB = (640, 1152)
_C_KB = (1152, 1664)
_C_VB = (1664, 2176)
_C_QI = (2176, 2688)
_C_KI = (2688, 2816)
_C_WI = (2816, 2944)
_C_GA = (2944, 3968)
_C_GB = (3968, 4992)
_W1_COLS = 4992


def _proj_kernel(x_ref, meta_ref, g_ref, w1_ref, gcq_ref, gckv_ref, wuq_ref, wukv_ref, ct_ref, st_ref,
                 qm_ref, km_ref, vm_ref, qb_ref, kb_ref, vb_ref, qi_ref, ki_ref, wi_ref,
                 ga_ref, gb_ref):
    is_meta = pl.program_id(0) == pl.num_programs(0) - 1
    h = jnp.where(is_meta, meta_ref[...], x_ref[...])
    xb = _rms(h, g_ref[...]).astype(BF16)

    def seg(c):
        return _dot(xb, w1_ref[:, c[0]:c[1]])

    ct = ct_ref[...]
    st = st_ref[...]
    nq = _rms(seg(_C_CQ), gcq_ref[...]).astype(BF16)
    qa = _dot(nq, wuq_ref[...])
    half = MLA_HEADS * HEAD_PAD
    for hd in range(MLA_HEADS):
        lo, hi = hd * HEAD_PAD, (hd + 1) * HEAD_PAD
        qm_ref[:, lo:hi] = (qa[:, lo:hi] * ct + qa[:, half + lo:half + hi] * st).astype(BF16)
    nkv = _rms(seg(_C_CKV), gckv_ref[...]).astype(BF16)
    kva = _dot(nkv, wukv_ref[...])
    kr = seg(_C_KRX) * ct + seg(_C_KRY) * st
    for hd in range(MLA_HEADS):
        lo, hi = hd * HEAD_PAD, (hd + 1) * HEAD_PAD
        km_ref[:, lo:hi] = (kva[:, lo:hi] + kr).astype(BF16)
    vm_ref[0] = kva[:, half:].T.astype(BF16)
    qb_ref[...] = seg(_C_QB).astype(BF16)
    kb_ref[...] = seg(_C_KB).astype(BF16)
    vb_ref[0] = seg(_C_VB).T.astype(BF16)
    qi_ref[...] = seg(_C_QI).astype(BF16)
    ki_ref[...] = seg(_C_KI).astype(BF16)
    wi_ref[...] = seg(_C_WI)
    ga_ref[...] = seg(_C_GA)
    gb_ref[...] = seg(_C_GB)


def _pack_w1(w_in):
    D = w_in.shape[0]
    offs = np.cumsum([0, MLA_Q_LORA, MLA_KV_LORA, MLA_ROPE, 512, 512, 512, 512, IDX_DIM, IDX_HEADS, D, D])
    c = lambda i: w_in[:, offs[i]:offs[i + 1]]
    z = lambda n: jnp.zeros((D, n), w_in.dtype)
    k_r = c(2)
    hr = MLA_ROPE // 2
    k_r_sw = jnp.concatenate([k_r[:, hr:], k_r[:, :hr]], axis=1)
    cols = [
        c(0), c(1),
        z(MLA_NOPE), k_r, z(LANES - MLA_NOPE - MLA_ROPE),
        z(MLA_NOPE), k_r_sw, z(LANES - MLA_NOPE - MLA_ROPE),
        c(3), c(4), c(5),
        c(6) * (IDX_DIM ** -0.5),
        c(7), c(7),
        c(8), z(LANES - IDX_HEADS),
        c(9), c(10),
    ]
    w1 = jnp.concatenate(cols, axis=1)
    assert w1.shape[1] == _W1_COLS
    return w1.astype(BF16)


def _pack_wuq(w_uq):
    R = w_uq.shape[0]
    w = w_uq.reshape(R, MLA_HEADS, MLA_NOPE + MLA_ROPE)
    nope, rope = w[..., :MLA_NOPE], w[..., MLA_NOPE:]
    hr = MLA_ROPE // 2
    rope_sw = jnp.concatenate([rope[..., hr:], rope[..., :hr]], axis=-1)
    zt = jnp.zeros((R, MLA_HEADS, HEAD_PAD - MLA_NOPE - MLA_ROPE), w.dtype)
    a = jnp.concatenate([nope, rope, zt], axis=-1).reshape(R, MLA_HEADS * HEAD_PAD)
    b = jnp.concatenate([jnp.zeros_like(nope), rope_sw, zt], axis=-1).reshape(R, MLA_HEADS * HEAD_PAD)
    return jnp.concatenate([a, b], axis=1).astype(BF16)


def _pack_wukv(w_ukv):
    R = w_ukv.shape[0]
    w = w_ukv.reshape(R, MLA_HEADS, MLA_NOPE + MLA_V)
    k = jnp.concatenate([w[..., :MLA_NOPE], jnp.zeros((R, MLA_HEADS, HEAD_PAD - MLA_NOPE), w.dtype)], axis=-1)
    v = w[..., MLA_NOPE:]
    return jnp.concatenate([k.reshape(R, -1), v.reshape(R, -1)], axis=1).astype(BF16)


def _rope_tables(pos):
    half = MLA_ROPE // 2
    inv = ROPE_BASE ** (-jnp.arange(half, dtype=F32) / half)
    ang = pos.astype(F32)[:, None] * inv[None, :]
    cos, sin = jnp.cos(ang), jnp.sin(ang)
    n = pos.shape[0]
    tail = jnp.zeros((n, HEAD_PAD - MLA_NOPE - MLA_ROPE), F32)
    ct = jnp.concatenate([jnp.ones((n, MLA_NOPE), F32), cos, cos, tail], axis=1)
    st = jnp.concatenate([jnp.zeros((n, MLA_NOPE), F32), -sin, sin, tail], axis=1)
    return ct, st


def _proj_call(x2d, meta_tile, g, w1, gcq, gckv, wuq, wukv, ct, st, S):
    BS, D = x2d.shape
    nt = BS // TQ + 1
    per_b = S // TQ
    row = lambda w: pl.BlockSpec((TQ, w), lambda i: (i, 0))
    full = lambda a: pl.BlockSpec(a.shape, lambda i: (0, 0))
    tab = pl.BlockSpec((TQ, HEAD_PAD), lambda i: (jnp.where(i == nt - 1, per_b, i % per_b), 0))
    widths = [(MLA_HEADS * HEAD_PAD, BF16), (MLA_HEADS * HEAD_PAD, BF16), (None, BF16),
              (512, BF16), (512, BF16), (None, BF16), (512, BF16), (LANES, BF16), (LANES, F32),
              (D, F32), (D, F32)]
    vt_rows = MLA_HEADS * MLA_V
    vt_spec = pl.BlockSpec((1, vt_rows, TQ), lambda i: (i, 0, 0))
    vt_shape = jax.ShapeDtypeStruct((nt, vt_rows, TQ), BF16)
    return pl.pallas_call(
        _proj_kernel,
        grid=(nt,),
        in_specs=[pl.BlockSpec((TQ, D), lambda i: (jnp.minimum(i, nt - 2), 0)), full(meta_tile),
                  full(g), full(w1), full(gcq), full(gckv), full(wuq), full(wukv), tab, tab],
        out_specs=[vt_spec if w is None else row(w) for w, _ in widths],
        out_shape=[vt_shape if w is None else jax.ShapeDtypeStruct((nt * TQ, w), dt) for w, dt in widths],
        compiler_params=_params(("parallel",)),
        name="proj",
    )(x2d, meta_tile, g, w1, gcq, gckv, wuq, wukv, ct, st)


def _colmax(z):
    return jnp.max(z, axis=0, keepdims=True)


def _pv(vt, z, m, l, acc):
    p = jnp.exp2(z - m)
    return l + jnp.sum(p, axis=0, keepdims=True), acc + _dot(vt, p.astype(BF16))


def _tile_ids(nq):
    g = pl.program_id(1)
    is_meta = g == pl.num_programs(1) - 1
    return is_meta, jnp.where(is_meta, 0, g % nq)


def _chunk_mask():
    key = lax.broadcasted_iota(jnp.int32, (TQ, TQ), 0)
    qry = lax.broadcasted_iota(jnp.int32, (TQ, TQ), 1)
    shift = CHUNK.bit_length() - 1
    return (key >> shift) <= (qry >> shift)


_CHUNKS = (8, 4, 2, 1)


def _fori_chunks(lo, hi, body_n, carry):
    for n in _CHUNKS:
        cnt = (hi - lo) // n
        carry = lax.fori_loop(0, cnt, lambda u, c, lo=lo, n=n: body_n(lo + n * u, c, n), carry)
        lo = lo + cnt * n
    return carry


def _fori_each(lo, hi, body, carry):
    def body_n(t, c, n):
        for i in range(n):
            c = body(t + i, c)
        return c
    return _fori_chunks(lo, hi, body_n, carry)


def _loop(lo, hi, fn):
    _fori_each(lo, hi, lambda t, c: (fn(t), c)[1], 0)


def _mla_kernel(q_ref, k_ref, vt_ref, km_ref, vtm_ref, o_ref, s_scr, sm_scr, *, nq):
    is_meta, j = _tile_ids(nq)
    n_full = jnp.where(is_meta, 0, j)
    n_vis = jnp.where(is_meta, 0, j + 1)
    c = (MLA_NOPE + MLA_ROPE) ** -0.5 * LOG2E
    meta_ok = lax.broadcasted_iota(jnp.int32, (LANES, TQ), 0) < N_META
    diag_ok = _chunk_mask()
    heads = lambda p: (2 * p, 2 * p + 1)
    hcol = lambda h: slice(h * HEAD_PAD, (h + 1) * HEAD_PAD)
    vrow = lambda h: slice(h * MLA_V, (h + 1) * MLA_V)

    def score_meta(p):
        ms = []
        for hh, h in enumerate(heads(p)):
            z = jnp.where(meta_ok, _dot_t(km_ref[:, hcol(h)], q_ref[:, hcol(h)]) * c, NEG_INF)
            sm_scr[p % 2, hh] = z
            ms.append(_colmax(z))
        return tuple(ms)

    def score_tiles(p, t, ms, n, masked):
        ks = pl.multiple_of(t * TQ, TQ)
        out = []
        for hh, h in enumerate(heads(p)):
            z = _dot_t(k_ref[pl.ds(ks, n * TQ), hcol(h)], q_ref[:, hcol(h)]) * c
            if masked:
                z = jnp.where(diag_ok, z, NEG_INF)
            for i in range(n):
                s_scr[p % 2, hh, t + i] = z[i * TQ:(i + 1) * TQ]
            out.append(jnp.maximum(ms[hh], _colmax(z)))
        return tuple(out)

    def pv_meta(p, ms):
        st = []
        for hh, h in enumerate(heads(p)):
            st += list(_pv(vtm_ref[0, vrow(h), 0:LANES], sm_scr[p % 2, hh], ms[hh],
                           jnp.zeros((1, TQ), F32), jnp.zeros((MLA_V, TQ), F32)))
        return tuple(st)

    def pv_tiles(p, t, st, ms, n):
        for i in range(n):
            out = []
            for hh, h in enumerate(heads(p)):
                out += list(_pv(vt_ref[t + i, vrow(h), :], s_scr[p % 2, hh, t + i], ms[hh],
                                st[2 * hh], st[2 * hh + 1]))
            st = tuple(out)
        return st

    pairs = MLA_HEADS // 2
    ms = score_meta(0)
    ms = _fori_chunks(0, n_full, lambda t, m, n: score_tiles(0, t, m, n, False), ms)
    ms = lax.fori_loop(n_full, n_vis, lambda t, m: score_tiles(0, t, m, 1, True), ms)
    for p in range(pairs):
        st = pv_meta(p, ms)
        if p + 1 < pairs:
            ms_next = score_meta(p + 1)
            st, ms_next = _fori_chunks(
                0, n_full, lambda t, c, n, p=p, ms=ms: (pv_tiles(p, t, c[0], ms, n),
                                                        score_tiles(p + 1, t, c[1], n, False)), (st, ms_next))
            st, ms_next = lax.fori_loop(
                n_full, n_vis, lambda t, c, p=p, ms=ms: (pv_tiles(p, t, c[0], ms, 1),
                                                         score_tiles(p + 1, t, c[1], 1, True)), (st, ms_next))
        else:
            st = _fori_chunks(0, n_vis, lambda t, s, n, p=p, ms=ms: pv_tiles(p, t, s, ms, n), st)
            ms_next = None
        ot = jnp.concatenate([st[1] / st[0], st[3] / st[2]], axis=0)
        o_ref[:, p * 2 * MLA_V:(p + 1) * 2 * MLA_V] = ot.T.astype(BF16)
        ms = ms_next


def _mla_call(qm, km, vmt, B, S):
    nq = S // TQ
    ng = B * nq + 1
    bidx = lambda g: jnp.minimum(g // nq, B - 1)
    mrow = B * S // LANES
    wk, wv = MLA_HEADS * HEAD_PAD, MLA_HEADS * MLA_V
    return pl.pallas_call(
        functools.partial(_mla_kernel, nq=nq),
        grid=(1, ng),
        in_specs=[
            pl.BlockSpec((TQ, wk), lambda _, g: (g, 0)),
            pl.BlockSpec((S, wk), lambda _, g: (bidx(g), 0)),
            pl.BlockSpec((nq, wv, TQ), lambda _, g: (bidx(g), 0, 0)),
            pl.BlockSpec((LANES, wk), lambda _, g: (mrow, 0)),
            pl.BlockSpec((1, wv, TQ), lambda _, g: (ng - 1, 0, 0)),
        ],
        out_specs=pl.BlockSpec((TQ, wv), lambda _, g: (g, 0)),
        out_shape=jax.ShapeDtypeStruct((ng * TQ, wv), BF16),
        scratch_shapes=[pltpu.VMEM((2, 2, nq, TQ, TQ), F32),
                        pltpu.VMEM((2, 2, LANES, TQ), F32)],
        compiler_params=_params(("arbitrary", "arbitrary")),
        name="mla_attn",
    )(qm, km, vmt, km, vmt)


_IDX_BITS = 13
_B_DIAG, _B_PREV, _B_META_FIRST, _B_META_META = 0, 1, 2, 3


def _dsa_kernel(qi_ref, wi_ref, qb_ref, ki_ref, kb_ref, vbt_ref, kim_ref, kbm_ref, vbtm_ref, bias_ref, o_ref,
                keys_scr, keym_scr, selm_scr, s_scr, sm_scr, qs_scr,
                dig_scr, dm_scr, *, nq, k_sel):
    is_meta, j = _tile_ids(nq)
    n_full = jnp.where(is_meta, 0, j)
    n_vis = jnp.where(is_meta, 0, j + 1)
    lane = lax.broadcasted_iota(jnp.int32, (TQ, LANES), 1)
    low = lane < DSA_HEAD_DIM
    krow_m = lax.broadcasted_iota(jnp.int32, (LANES, TQ), 0)
    krow = lax.broadcasted_iota(jnp.int32, (TQ, TQ), 0)
    meta_ok = krow_m < N_META
    diag_ok = _chunk_mask()

    for p in range(IDX_HEADS // 2):
        qp = qi_ref[:, p * LANES:(p + 1) * LANES]
        qs_scr[2 * p] = jnp.where(low, qp, jnp.zeros_like(qp))
        qs_scr[2 * p + 1] = jnp.where(low, jnp.zeros_like(qp), qp)
    wt = (wi_ref[...] * (IDX_HEADS ** -0.5)).T

    def sort_key(score):
        bits = pltpu.bitcast(score + 0.0, jnp.int32)
        return bits ^ ((bits >> 31) & 0x7FFFFFFF)

    def score_of(kt):
        score = jnp.zeros((kt.shape[0], TQ), F32)
        for h in range(IDX_HEADS):
            score = score + wt[h:h + 1, :] * jnp.maximum(_dot_t(kt, qs_scr[h]), 0.0)
        return sort_key(score)

    keym_scr[...] = jnp.where(meta_ok, score_of(kim_ref[...]), INT_MIN)

    def score_tiles(t, n, masked):
        key = score_of(ki_ref[pl.ds(pl.multiple_of(t * TQ, TQ), n * TQ), :])
        if masked:
            key = jnp.where(diag_ok, key, INT_MIN)
        for i in range(n):
            keys_scr[t + i] = key[i * TQ:(i + 1) * TQ]

    _fori_chunks(0, n_full, lambda t, c, n: (score_tiles(t, n, False), c)[1], 0)
    lax.fori_loop(n_full, n_vis, lambda t, c: (score_tiles(t, 1, True), c)[1], 0)

    one = lambda m: jnp.where(m, 1, 0)
    fold = lambda a: jnp.sum(a.reshape(-1, 8, TQ), axis=0)

    def count(pred):
        def cbody(t, acc):
            return acc + fold(pred(keys_scr[t], LANES + t * TQ + krow))
        acc = _fori_each(0, n_vis, cbody, fold(pred(keym_scr[...], krow_m)))
        return jnp.sum(acc, axis=0, keepdims=True)

    def fold16(m):
        parts = [m[r:r + 16] for r in range(0, m.shape[0], 16)]
        while len(parts) > 1:
            parts = [a + b for a, b in zip(parts[::2], parts[1::2])]
        return parts[0].astype(F32)

    one16 = lambda m: jnp.where(m, jnp.ones((), BF16), jnp.zeros((), BF16))

    def count16(meta_scr, tile_scr, pred):
        acc = _fori_each(0, n_vis, lambda t, a: a + fold16(one16(pred(tile_scr[t]))),
                         fold16(one16(pred(meta_scr[...]))))
        return jnp.sum(acc, axis=0, keepdims=True)

    def digit_search(d_prev, shift, need):
        top, last = shift == 24, shift == 0

        def digits(key, prev):
            d = ((key >> shift) + 128 if top else (key >> shift) & 0xFF).astype(F32).astype(BF16)
            if top:
                return d
            return jnp.where(prev == d_prev.astype(F32).astype(BF16), d, -jnp.ones((), BF16))

        dm_scr[...] = digits(keym_scr[...], None if top else dm_scr[...])
        _loop(0, n_vis, lambda t: dig_scr.__setitem__(t, digits(keys_scr[t], None if top else dig_scr[t])))

        def d_body(i, c):
            d, above, at_least = c
            cand = d + lax.shift_left(jnp.int32(1), 7 - i)
            cb = cand.astype(F32).astype(BF16)
            cnt = count16(dm_scr, dig_scr, lambda v: v >= cb)
            ok = cnt >= need
            return jnp.where(ok, cand, d), jnp.where(ok, above, cnt), jnp.where(ok, cnt, at_least)

        zero = jnp.zeros((1, TQ), F32)
        all_cands = count16(dm_scr, dig_scr, lambda v: v >= jnp.zeros((), BF16)) if last else zero
        return lax.fori_loop(0, 8, d_body, (jnp.zeros((1, TQ), jnp.int32), zero, all_cands))

    t = jnp.zeros((1, TQ), jnp.int32)
    need_d = jnp.full((1, TQ), float(k_sel), F32)
    d = None
    for shift in (24, 16, 8, 0):
        d, above, at_least = digit_search(d, shift, need_d)
        t = (d - 128) if shift == 24 else (t << 8) | d
        need_d = need_d - above
    need = need_d.astype(jnp.int32)
    cnt_ge = k_sel - need + (at_least - above).astype(jnp.int32)

    tied = jnp.max(jnp.where((cnt_ge > k_sel) & (t != INT_MIN), 1, 0))

    def write_masks(sel_mask):
        selm_scr[...] = sel_mask(keym_scr[...], krow_m)
        _loop(0, n_vis, lambda tt: keys_scr.__setitem__(
            tt, pltpu.bitcast(sel_mask(keys_scr[tt], LANES + tt * TQ + krow), jnp.int32)))

    @pl.when(tied == 0)
    def _():
        t_min = jnp.maximum(t, INT_MIN + 1)
        write_masks(lambda b, _: jnp.where(b >= t_min, 0.0, NEG_INF))

    @pl.when(tied > 0)
    def _():
        def tie_body(i, r):
            cand = r | lax.shift_left(jnp.int32(1), _IDX_BITS - 1 - i)
            f = count(lambda b, idx: jnp.where(b == t, one(idx < cand), 0))
            return jnp.where(f < need, cand, r)
        mstar = lax.fori_loop(0, _IDX_BITS, tie_body, jnp.zeros((1, TQ), jnp.int32)) + 1

        def sel_mask(b, idx):
            tie_ok = jnp.where(b == t, one(idx < mstar), 0)
            sel = jnp.where(b == INT_MIN, 0, jnp.where(b > t, 1, tie_ok))
            return jnp.where(sel > 0, 0.0, NEG_INF)

        write_masks(sel_mask)

    c = DSA_HEAD_DIM ** -0.5 * LOG2E
    n_far = jnp.maximum(n_full - 1, 0)
    use_mb = jnp.where(is_meta | (j == 0), 1.0, 0.0)
    mb_idx = jnp.where(is_meta, _B_META_META, _B_META_FIRST)
    pcols = lambda p: slice(p * LANES, (p + 1) * LANES)
    vrow = lambda h: slice(h * DSA_HEAD_DIM, (h + 1) * DSA_HEAD_DIM)

    def masked_q(p):
        qp = qb_ref[:, pcols(p)]
        return [jnp.where(low, qp, jnp.zeros_like(qp)), jnp.where(low, jnp.zeros_like(qp), qp)]

    def score_meta(p):
        qs = masked_q(p)
        ms = []
        for hh in range(2):
            z = (_dot_t(kbm_ref[:, pcols(p)], qs[hh]) * c + bias_ref[2 * p + hh, mb_idx][:LANES, :] * use_mb
                 + selm_scr[...])
            sm_scr[p % 2, hh] = z
            ms.append(_colmax(z))
        return tuple(ms)

    def score_tiles(p, t, ms, n, near):
        qs = masked_q(p)
        k = kb_ref[pl.ds(pl.multiple_of(t * TQ, TQ), n * TQ), pcols(p)]
        out = []
        for hh in range(2):
            zz = _dot_t(k, qs[hh]) * c
            m = ms[hh]
            for i in range(n):
                z = zz[i * TQ:(i + 1) * TQ] + pltpu.bitcast(keys_scr[t + i], F32)
                if near is not None:
                    z = z + bias_ref[2 * p + hh, near]
                s_scr[p % 2, hh, t + i] = z
                m = jnp.maximum(m, _colmax(z))
            out.append(m)
        return tuple(out)

    def pv_meta(p, ms):
        st = []
        for hh in range(2):
            st += list(_pv(vbtm_ref[0, vrow(2 * p + hh), 0:LANES], sm_scr[p % 2, hh], ms[hh],
                           jnp.zeros((1, TQ), F32), jnp.zeros((DSA_HEAD_DIM, TQ), F32)))
        return tuple(st)

    def pv_tiles(p, t, st, ms, n):
        for i in range(n):
            out = []
            for hh in range(2):
                out += list(_pv(vbt_ref[t + i, vrow(2 * p + hh), :], s_scr[p % 2, hh, t + i], ms[hh],
                                st[2 * hh], st[2 * hh + 1]))
            st = tuple(out)
        return st

    def score_all(p, ms):
        ms = _fori_chunks(0, n_far, lambda t, m, n: score_tiles(p, t, m, n, None), ms)
        ms = lax.fori_loop(n_far, n_full, lambda t, m: score_tiles(p, t, m, 1, _B_PREV), ms)
        return lax.fori_loop(n_full, n_vis, lambda t, m: score_tiles(p, t, m, 1, _B_DIAG), ms)

    pairs = DSA_HEADS // 2
    ms = score_all(0, score_meta(0))
    for p in range(pairs):
        st = pv_meta(p, ms)
        if p + 1 < pairs:
            both = lambda near, n=1, p=p, ms=ms: (
                lambda t, c, n=n: (pv_tiles(p, t, c[0], ms, n), score_tiles(p + 1, t, c[1], n, near)))
            carry = (st, score_meta(p + 1))
            carry = _fori_chunks(0, n_far, lambda t, c, n, p=p, ms=ms: (
                pv_tiles(p, t, c[0], ms, n), score_tiles(p + 1, t, c[1], n, None)), carry)
            carry = lax.fori_loop(n_far, n_full, both(_B_PREV), carry)
            st, ms_next = lax.fori_loop(n_full, n_vis, both(_B_DIAG), carry)
        else:
            st = _fori_chunks(0, n_vis, lambda t, s, n, p=p, ms=ms: pv_tiles(p, t, s, ms, n), st)
            ms_next = None
        ot = jnp.concatenate([st[1] / st[0], st[3] / st[2]], axis=0)
        o_ref[:, pcols(p)] = ot.T.astype(BF16)
        ms = ms_next


def _dsa_call(qi, wi, qb, ki, kb, vbt, bias, B, S, k_sel):
    nq = S // TQ
    ng = B * nq + 1
    W = DSA_HEADS * DSA_HEAD_DIM
    bidx = lambda g: jnp.minimum(g // nq, B - 1)
    mrow = B * S // LANES
    qrow = lambda w: pl.BlockSpec((TQ, w), lambda _, g: (g, 0))
    kv = lambda w: pl.BlockSpec((S, w), lambda _, g: (bidx(g), 0))
    kvm = lambda w: pl.BlockSpec((LANES, w), lambda _, g: (mrow, 0))
    return pl.pallas_call(
        functools.partial(_dsa_kernel, nq=nq, k_sel=k_sel),
        grid=(1, ng),
        in_specs=[qrow(W), qrow(LANES), qrow(W), kv(LANES), kv(W),
                  pl.BlockSpec((nq, W, TQ), lambda _, g: (bidx(g), 0, 0)),
                  kvm(LANES), kvm(W),
                  pl.BlockSpec((1, W, TQ), lambda _, g: (ng - 1, 0, 0)),
                  pl.BlockSpec(bias.shape, lambda _, g: (0, 0, 0, 0), pipeline_mode=pl.Buffered(1))],
        out_specs=qrow(W),
        out_shape=jax.ShapeDtypeStruct((ng * TQ, W), BF16),
        scratch_shapes=[
            pltpu.VMEM((nq, TQ, TQ), jnp.int32),
            pltpu.VMEM((LANES, TQ), jnp.int32),
            pltpu.VMEM((LANES, TQ), F32),
            pltpu.VMEM((2, 2, nq, TQ, TQ), F32),
            pltpu.VMEM((2, 2, LANES, TQ), F32),
            pltpu.VMEM((IDX_HEADS, TQ, LANES), BF16),
            pltpu.VMEM((nq, TQ, TQ), BF16),
            pltpu.VMEM((LANES, TQ), BF16),
        ],
        compiler_params=_params(("arbitrary", "arbitrary")),
        name="dsa",
    )(qi, wi, qb, ki, kb, vbt, ki, kb, vbt, bias)


def _t5_bucket(rel):
    nb = REL_BUCKETS // 2
    max_exact = nb // 2
    n = jnp.abs(rel)
    large = max_exact + (jnp.log(jnp.maximum(n, 1).astype(F32) / max_exact)
                         / math.log(REL_MAX_DIST / max_exact) * (nb - max_exact)).astype(jnp.int32)
    large = jnp.minimum(large, nb - 1)
    return jnp.where(rel > 0, nb, 0) + jnp.where(n < max_exact, n, large)


def _bias_kernel(bucket_ref, rb_ref, o_ref, *, far_bucket):
    h = pl.program_id(0)
    far = rb_ref[far_bucket, h]
    for tile in range(bucket_ref.shape[0]):
        bkt = bucket_ref[tile]
        acc = jnp.zeros(bkt.shape, F32)
        for b in range(REL_BUCKETS):
            acc = jnp.where(bkt == b, rb_ref[b, h], acc)
        o_ref[0, tile] = (acc - far) * LOG2E


def _bias_tiles(rel_bias):
    k = jnp.arange(TQ, dtype=jnp.int32)[:, None]
    q = jnp.arange(TQ, dtype=jnp.int32)[None, :]
    rels = jnp.stack([k - q, k - q - TQ, k - (q + N_META), k - q])
    far_bucket = REL_BUCKETS // 2 - 1
    return pl.pallas_call(
        functools.partial(_bias_kernel, far_bucket=far_bucket),
        grid=(DSA_HEADS,),
        in_specs=[pl.BlockSpec((4, TQ, TQ), lambda h: (0, 0, 0)),
                  pl.BlockSpec(memory_space=pltpu.SMEM)],
        out_specs=pl.BlockSpec((1, 4, TQ, TQ), lambda h: (h, 0, 0, 0)),
        out_shape=jax.ShapeDtypeStruct((DSA_HEADS, 4, TQ, TQ), F32),
        compiler_params=_params(("parallel",)),
        name="bias_tiles",
    )(_t5_bucket(rels), rel_bias)


_MERGE_ROWS = 128


def _merge_kernel(x_ref, meta_ref, oa_ref, ob_ref, ga_ref, gb_ref, wa_ref, wb_ref, wo_ref, gf_ref, wrt_ref, brt_ref,
                  h1_ref, route_ref, gate_ref, cnt_ref, run_scr):
    is_meta = pl.program_id(0) == pl.num_programs(0) - 1
    for rows in (slice(s, s + _MERGE_ROWS) for s in range(0, x_ref.shape[0], _MERGE_ROWS)):
        h = jnp.where(is_meta, meta_ref[rows, :], x_ref[rows, :])
        y = (jax.nn.sigmoid(ga_ref[rows, :]) * _dot(oa_ref[rows, :], wa_ref[...])
             + jax.nn.sigmoid(gb_ref[rows, :]) * _dot(ob_ref[rows, :], wb_ref[...]))
        h1_ref[rows, :] = h + _dot(y.astype(BF16), wo_ref[...])
    _rank_tile(h1_ref[...], gf_ref[...], wrt_ref[...], brt_ref[...], route_ref, gate_ref, cnt_ref, run_scr)


def _merge_call(x2d, meta_tile, oa, ob, ga, gb, wa, wb, wo, gf, wrt, brt):
    BS, D = x2d.shape
    nt = BS // TQ + 1
    R = nt * TQ
    row = lambda w: pl.BlockSpec((TQ, w), lambda i: (i, 0))
    full = lambda a: pl.BlockSpec(a.shape, lambda i: (0, 0))
    return pl.pallas_call(
        _merge_kernel,
        grid=(nt,),
        in_specs=[pl.BlockSpec((TQ, D), lambda i: (jnp.minimum(i, nt - 2), 0)), full(meta_tile),
                  row(oa.shape[1]), row(ob.shape[1]), row(D), row(D), full(wa), full(wb), full(wo),
                  full(gf), full(wrt), full(brt)],
        out_specs=[row(D), pl.BlockSpec((8, TQ), lambda i: (0, i)), row(LANES),
                   pl.BlockSpec((N_EXPERTS, LANES), lambda i: (0, 0))],
        out_shape=[jax.ShapeDtypeStruct((R, D), F32), jax.ShapeDtypeStruct((8, R), jnp.int32),
                   jax.ShapeDtypeStruct((R, LANES), F32), jax.ShapeDtypeStruct((N_EXPERTS, LANES), jnp.int32)],
        scratch_shapes=[pltpu.VMEM((N_EXPERTS, LANES), F32)],
        compiler_params=_params(("arbitrary",)),
        name="merge_route",
    )(x2d, meta_tile, oa, ob, ga, gb, wa, wb, wo, gf, wrt, brt)


_ROUTER_ROWS = 40
TE = 256
_ISSUE_UNROLL = TQ


def _route_rows(xn, wrt, brt):
    nr = _ROUTER_ROWS
    tm = xn.shape[0]
    lt = lax.dot_general(wrt, xn, (((1,), (1,)), ((), ())), preferred_element_type=F32,
                         precision=lax.Precision.HIGHEST) + brt
    row = lax.broadcasted_iota(jnp.int32, (nr, tm), 0)
    ninf = -jnp.inf
    cmax = lambda a: jnp.max(a, axis=0, keepdims=True)
    cmin = lambda a: jnp.min(a, axis=0, keepdims=True)
    gl = jnp.where((row >= N_EXPERTS) & (row < N_EXPERTS + N_GROUPS), lt, ninf)
    gmax = cmax(gl)
    gsel = cmin(jnp.where(gl == gmax, row, nr)) - N_EXPERTS
    p_group = 1.0 / jnp.sum(jnp.exp(gl - gmax), axis=0, keepdims=True)
    lo = gsel * EXPERTS_PER_GROUP
    el = jnp.where((row >= lo) & (row < lo + EXPERTS_PER_GROUP), lt, ninf)
    m1 = cmax(el)
    i1 = cmin(jnp.where(el == m1, row, nr))
    el2 = jnp.where(row == i1, ninf, el)
    m2 = cmax(el2)
    i2 = cmin(jnp.where(el2 == m2, row, nr))
    e2 = jnp.exp(m2 - m1)
    return i1, i2, p_group / (1.0 + e2), p_group * e2 / (1.0 + e2)


def _rank_tile(h1, gf, wrt, brt, route_ref, gate_ref, cnt_ref, run_scr):
    @pl.when(pl.program_id(0) == 0)
    def _():
        run_scr[...] = jnp.zeros_like(run_scr)

    tm = h1.shape[0]
    i1, i2, w1, w2 = _route_rows(_rms(h1, gf), wrt, brt)
    row = lax.broadcasted_iota(jnp.int32, (N_EXPERTS, tm), 0)
    o1 = jnp.where(row == i1, 1.0, 0.0)
    o2 = jnp.where(row == i2, 1.0, 0.0)
    a = lax.broadcasted_iota(jnp.int32, (tm, tm), 0)
    b = lax.broadcasted_iota(jnp.int32, (tm, tm), 1)
    before = jnp.where(a < b, 1.0, 0.0).astype(BF16)
    p1 = _dot(o1.astype(BF16), before)
    p2 = _dot(o2.astype(BF16), before)
    run = run_scr[:, 0:1]
    c1 = jnp.sum(o1, axis=1, keepdims=True)
    c2 = jnp.sum(o2, axis=1, keepdims=True)
    r1 = jnp.sum(o1 * (run + p1), axis=0, keepdims=True)
    r2 = jnp.sum(o2 * (run + c1 + p2), axis=0, keepdims=True)
    new_run = run + c1 + c2
    run_scr[...] = jnp.broadcast_to(new_run, run_scr.shape)
    cnt_ref[...] = jnp.broadcast_to(new_run, cnt_ref.shape).astype(jnp.int32)
    z = jnp.zeros((4, tm), jnp.int32)
    route_ref[...] = jnp.concatenate([i1, r1.astype(jnp.int32), i2, r2.astype(jnp.int32), z], axis=0)
    gt = jnp.concatenate([w1, w2, jnp.zeros((LANES - 2, tm), F32)], axis=0)
    gate_ref[...] = gt.T


_PAD_BITS = tuple(1 << b for b in reversed(range(TE.bit_length() - 1)))


def _dispatch_kernel(pos_ref, pad_ref, h1_ref, gf_ref, xs_hbm, buf, zbuf, sem, zsem, *, n_rows):
    i = pl.program_id(0)
    nt = pl.num_programs(0)
    slot = i % 2
    tm = h1_ref.shape[0]

    @pl.when(i == 0)
    def _():
        zbuf[...] = jnp.zeros_like(zbuf)

        def pad_copies(e, wait):
            start, n = pad_ref[e], pad_ref[N_EXPERTS + e]

            def copy(dst, rows):
                cp = pltpu.make_async_copy(zbuf.at[pl.ds(0, rows), :], xs_hbm.at[pl.ds(dst, rows), :], zsem)
                cp.wait() if wait else cp.start()

            end = start + n
            for b in (b for b in _PAD_BITS if b >= 8):
                end = end - (n & b)
                pl.when((n & b) != 0)(functools.partial(copy, pl.multiple_of(end, 8), b))
            for r in range(7):
                pl.when(r < (n & 7))(functools.partial(copy, start + r, 1))

        def tail_copies(blk, wait):
            for h in range(0, TE, zbuf.shape[0]):
                cp = pltpu.make_async_copy(
                    zbuf, xs_hbm.at[pl.ds(pl.multiple_of(blk * TE + h, zbuf.shape[0]), zbuf.shape[0]), :], zsem)
                cp.wait() if wait else cp.start()

        n_used, n_blk = pad_ref[2 * N_EXPERTS], xs_hbm.shape[0] // TE
        for wait in (False, True):
            _loop_plain(N_EXPERTS, lambda e: pad_copies(e, wait))
            lax.fori_loop(n_used, n_blk, lambda b, c: (tail_copies(b, wait), c)[1], 0)

    def wait_slot(s):
        for _ in range(2):
            pltpu.make_async_copy(buf.at[s], xs_hbm.at[pl.ds(0, tm), :], sem.at[s]).wait()

    @pl.when(i >= 2)
    def _():
        wait_slot(slot)

    buf[slot] = _rms(h1_ref[...], gf_ref[...])

    def issue(u, c):
        for v in range(_ISSUE_UNROLL):
            r = u * _ISSUE_UNROLL + v
            src = buf.at[slot, pl.ds(r, 1), :]
            for k in range(2):
                p = pos_ref[k * n_rows + i * tm + r]
                pltpu.make_async_copy(src, xs_hbm.at[pl.ds(p, 1), :], sem.at[slot]).start()
        return c

    lax.fori_loop(0, tm // _ISSUE_UNROLL, issue, 0)

    @pl.when(i == nt - 1)
    def _():
        wait_slot(slot)

        @pl.when(nt >= 2)
        def _():
            wait_slot(1 - slot)


def _loop_plain(n, fn):
    lax.fori_loop(0, n, lambda e, c: (fn(e), c)[1], 0)


def _dispatch_call(pos, pad_runs, h1, gf, n_slots):
    R, D = h1.shape
    return pl.pallas_call(
        functools.partial(_dispatch_kernel, n_rows=R),
        grid_spec=pltpu.PrefetchScalarGridSpec(
            num_scalar_prefetch=2, grid=(R // TQ,),
            in_specs=[pl.BlockSpec((TQ, D), lambda i, pos, pad: (i, 0)),
                      pl.BlockSpec(gf.shape, lambda i, pos, pad: (0, 0))],
            out_specs=pl.BlockSpec(memory_space=pl.ANY),
            scratch_shapes=[pltpu.VMEM((2, TQ, D), F32), pltpu.VMEM((_PAD_BITS[0], D), F32),
                            pltpu.SemaphoreType.DMA((2,)), pltpu.SemaphoreType.DMA]),
        out_shape=jax.ShapeDtypeStruct((n_slots, D), F32),
        compiler_params=_params(("arbitrary",)),
        name="moe_dispatch",
    )(pos, pad_runs, h1, gf)


def _ffn_kernel(be_ref, nu_ref, x_ref, wg_ref, wu_ref, wd_ref, y_ref):
    i = pl.program_id(0)

    @pl.when(i < nu_ref[0])
    def _():
        x = x_ref[...].astype(BF16)
        a = _dot(x, wg_ref[0].astype(BF16))
        u = _dot(x, wu_ref[0].astype(BF16))
        hmid = (a * jax.nn.sigmoid(a) * u).astype(BF16)
        y_ref[...] = _dot(hmid, wd_ref[0].astype(BF16))

    @pl.when(i >= nu_ref[0])
    def _():
        y_ref[...] = jnp.zeros_like(y_ref)


def _ffn_call(blk_expert, n_used, xs, wg, wu, wd):
    NS, D = xs.shape
    wspec = lambda shp: pl.BlockSpec((1,) + shp, lambda i, be, nu: (be[i], 0, 0))
    return pl.pallas_call(
        _ffn_kernel,
        grid_spec=pltpu.PrefetchScalarGridSpec(
            num_scalar_prefetch=2, grid=(NS // TE,),
            in_specs=[pl.BlockSpec((TE, D), lambda i, be, nu: (jnp.minimum(i, nu[0] - 1), 0)),
                      wspec((D, D_EXPERT)), wspec((D, D_EXPERT)), wspec((D_EXPERT, D))],
            out_specs=pl.BlockSpec((TE, D), lambda i, be, nu: (i, 0))),
        out_shape=jax.ShapeDtypeStruct((NS, D), F32),
        compiler_params=_params(("arbitrary",)),
        name="moe_ffn",
    )(blk_expert, n_used, xs, wg, wu, wd)


def _combine_kernel(pos_ref, h1_ref, gate_ref, gfin_ref, ys_hbm, o_ref, buf, sem, *, n_rows):
    i = pl.program_id(0)
    nt = pl.num_programs(0)
    slot = i % 2
    tm = h1_ref.shape[0]

    def fetch(tile, s):
        def issue(u, c):
            for v in range(_ISSUE_UNROLL):
                r = u * _ISSUE_UNROLL + v
                for k in range(2):
                    p = pos_ref[k * n_rows + tile * tm + r]
                    pltpu.make_async_copy(ys_hbm.at[pl.ds(p, 1), :], buf.at[s, k, pl.ds(r, 1), :],
                                          sem.at[s]).start()
            return c
        lax.fori_loop(0, tm // _ISSUE_UNROLL, issue, 0)

    @pl.when(i == 0)
    def _():
        fetch(0, 0)

    @pl.when(i + 1 < nt)
    def _():
        fetch(i + 1, 1 - slot)

    for k in range(2):
        pltpu.make_async_copy(ys_hbm.at[pl.ds(0, tm), :], buf.at[slot, k], sem.at[slot]).wait()

    g = gate_ref[...]
    ffn = g[:, 0:1] * buf[slot, 0] + g[:, 1:2] * buf[slot, 1]

    @pl.when(i < nt - 1)
    def _():
        o_ref[...] = _rms(h1_ref[...] + ffn, gfin_ref[...])


def _combine_call(pos, h1, gates, gfin, ys, n_out):
    R, D = h1.shape
    nt = R // TQ
    return pl.pallas_call(
        functools.partial(_combine_kernel, n_rows=R),
        grid_spec=pltpu.PrefetchScalarGridSpec(
            num_scalar_prefetch=1, grid=(nt,),
            in_specs=[pl.BlockSpec((TQ, D), lambda i, pos: (i, 0)),
                      pl.BlockSpec((TQ, LANES), lambda i, pos: (i, 0)),
                      pl.BlockSpec(gfin.shape, lambda i, pos: (0, 0)),
                      pl.BlockSpec(memory_space=pl.ANY)],
            out_specs=pl.BlockSpec((TQ, D), lambda i, pos: (jnp.minimum(i, nt - 2), 0)),
            scratch_shapes=[pltpu.VMEM((2, 2, TQ, D), F32), pltpu.SemaphoreType.DMA((2,))]),
        out_shape=jax.ShapeDtypeStruct((n_out, D), F32),
        compiler_params=_params(("arbitrary",)),
        name="moe_combine",
    )(pos, h1, gates, gfin, ys)


def _sparse_moe(h1, route, gates, cnt, gf, wg, wu, wd, gfin, n_out):
    R, D = h1.shape
    counts = cnt[:, 0]
    padded = (counts + TE - 1) // TE * TE
    pad_end = jnp.cumsum(padded)
    offs = pad_end - padded
    nb = -(-2 * R // TE) + N_EXPERTS
    eids = jnp.arange(N_EXPERTS, dtype=jnp.int32)
    slot_of = lambda e, r: r + jnp.sum(jnp.where(e[:, None] == eids[None, :], offs[None, :], 0), axis=1)
    pos = jnp.concatenate([slot_of(route[0], route[1]), slot_of(route[2], route[3])]).astype(jnp.int32)
    blk_start = jnp.arange(nb, dtype=jnp.int32) * TE
    blk_expert = jnp.minimum(jnp.sum((blk_start[:, None] >= pad_end[None, :]).astype(jnp.int32), axis=1),
                             N_EXPERTS - 1).astype(jnp.int32)
    n_used = (pad_end[-1] // TE).astype(jnp.int32).reshape(1)
    pad_runs = jnp.concatenate([offs + counts, padded - counts, n_used]).astype(jnp.int32)
    xs = _dispatch_call(pos, pad_runs, h1, gf, nb * TE)
    ys = _ffn_call(blk_expert, n_used, xs, wg, wu, wd)
    return _combine_call(pos, h1, gates, gfin, ys, n_out)


def kernel(x, meta_tokens, norm_mix_g, w_in, mla_cq_norm_g, mla_ckv_norm_g, w_mla_uq, w_mla_ukv,
           w_branch_a, w_branch_b, w_out, rel_bias, norm_ffn_g, w_router_group, b_router_group,
           w_router_expert, b_router_expert, w_exp_gate, w_exp_up, w_exp_down, norm_final_g):
    B, S, D = x.shape
    assert S % TQ == 0 and norm_mix_g.shape[0] == 1
    k_sel = min(K_SEL_MAX, S // 4)
    x2d = x.reshape(B * S, D)
    meta_tile = jnp.concatenate([meta_tokens.astype(x.dtype), jnp.zeros((TQ - N_META, D), x.dtype)], axis=0)

    pos = np.concatenate([N_META + np.arange(S), np.minimum(np.arange(TQ), N_META)]).astype(np.int32)
    ct, st = _rope_tables(jnp.asarray(pos))

    qm, km, vmt, qb, kb, vbt, qi, ki, wi, ga, gb = _proj_call(
        x2d, meta_tile, norm_mix_g.reshape(1, D), _pack_w1(w_in[0]), mla_cq_norm_g.reshape(1, -1),
        mla_ckv_norm_g.reshape(1, -1), _pack_wuq(w_mla_uq[0]), _pack_wukv(w_mla_ukv[0]), ct, st, S)

    o_a = _mla_call(qm, km, vmt, B, S)
    o_b = _dsa_call(qi, wi, qb, ki, kb, vbt, _bias_tiles(rel_bias), B, S, k_sel)

    pad_r = _ROUTER_ROWS - N_EXPERTS - N_GROUPS
    w_r = jnp.concatenate([w_router_expert[0].T, w_router_group[0].T, jnp.zeros((pad_r, D), F32)], axis=0)
    b_r = jnp.concatenate([b_router_expert[0], b_router_group[0],
                           jnp.zeros((pad_r,), F32)]).reshape(_ROUTER_ROWS, 1)
    gf = norm_ffn_g.reshape(1, D)
    h1, route, gates, cnt = _merge_call(x2d, meta_tile, o_a, o_b, ga, gb, w_branch_a[0].astype(BF16),
                                        w_branch_b[0].astype(BF16), w_out[0].astype(BF16), gf, w_r, b_r)
    out = _sparse_moe(h1, route, gates, cnt, gf, w_exp_gate[0], w_exp_up[0], w_exp_down[0],
                      norm_final_g.reshape(1, D), B * S)
    return out.reshape(B, S, D)
```

```python
import functools
import math

import numpy as np
import jax
import jax.numpy as jnp
from jax import lax
from jax.experimental import pallas as pl
from jax.experimental.pallas import tpu as pltpu

CHUNK = 64
N_META = 16
NEG_INF = -1e30
RMS_EPS = 1e-6
ROPE_BASE = 10000.0
MLA_HEADS = 8
MLA_Q_LORA = 256
MLA_KV_LORA = 128
MLA_NOPE = 64
MLA_ROPE = 32
MLA_V = 64
DSA_HEADS = 8
DSA_HEAD_DIM = 64
IDX_HEADS = 8
IDX_DIM = 64
K_SEL_MAX = 256
REL_BUCKETS = 32
REL_MAX_DIST = 128
N_GROUPS = 4
EXPERTS_PER_GROUP = 8
N_EXPERTS = N_GROUPS * EXPERTS_PER_GROUP
D_EXPERT = 256

LANES = 128
TQ = 256
HEAD_PAD = 128
VMEM_LIMIT = 56 * 1024 * 1024
INT_MIN = -2 ** 31
LOG2E = math.log2(math.e)

F32 = jnp.float32
BF16 = jnp.bfloat16


def _params(sem):
    return pltpu.CompilerParams(dimension_semantics=sem, vmem_limit_bytes=VMEM_LIMIT)


def _rms(x, g):
    return x * lax.rsqrt(jnp.mean(x * x, axis=-1, keepdims=True) + RMS_EPS) * g


def _dot(a, b):
    return jnp.dot(a, b, preferred_element_type=F32)


def _dot_t(a, b):
    return lax.dot_general(a, b, (((1,), (1,)), ((), ())), preferred_element_type=F32)


_C_CQ = (0, 256)
_C_CKV = (256, 384)
_C_KRX = (384, 512)
_C_KRY = (512, 640)
_C_QB = (640, 1152)
_C_KB = (1152, 1664)
_C_VB = (1664, 2176)
_C_QI = (2176, 2688)
_C_KI = (2688, 2816)
_C_WI = (2816, 2944)
_C_GA = (2944, 3968)
_C_GB = (3968, 4992)
_W1_COLS = 4992


def _proj_kernel(x_ref, meta_ref, g_ref, w1_ref, gcq_ref, gckv_ref, wuq_ref, wukv_ref, ct_ref, st_ref,
                 qm_ref, km_ref, vm_ref, qb_ref, kb_ref, vb_ref, qi_ref, ki_ref, wi_ref,
                 ga_ref, gb_ref):
    is_meta = pl.program_id(0) == pl.num_programs(0) - 1
    h = jnp.where(is_meta, meta_ref[...], x_ref[...])
    xb = _rms(h, g_ref[...]).astype(BF16)

    def seg(c):
        return _dot(xb, w1_ref[:, c[0]:c[1]])

    ct = ct_ref[...]
    st = st_ref[...]
    nq = _rms(seg(_C_CQ), gcq_ref[...]).astype(BF16)
    qa = _dot(nq, wuq_ref[...])
    half = MLA_HEADS * HEAD_PAD
    for hd in range(MLA_HEADS):
        lo, hi = hd * HEAD_PAD, (hd + 1) * HEAD_PAD
        qm_ref[:, lo:hi] = (qa[:, lo:hi] * ct + qa[:, half + lo:half + hi] * st).astype(BF16)
    nkv = _rms(seg(_C_CKV), gckv_ref[...]).astype(BF16)
    kva = _dot(nkv, wukv_ref[...])
    kr = seg(_C_KRX) * ct + seg(_C_KRY) * st
    for hd in range(MLA_HEADS):
        lo, hi = hd * HEAD_PAD, (hd + 1) * HEAD_PAD
        km_ref[:, lo:hi] = (kva[:, lo:hi] + kr).astype(BF16)
    vm_ref[0] = kva[:, half:].T.astype(BF16)
    qb_ref[...] = seg(_C_QB).astype(BF16)
    kb_ref[...] = seg(_C_KB).astype(BF16)
    vb_ref[0] = seg(_C_VB).T.astype(BF16)
    qi_ref[...] = seg(_C_QI).astype(BF16)
    ki_ref[...] = seg(_C_KI).astype(BF16)
    wi_ref[...] = seg(_C_WI)
    ga_ref[...] = seg(_C_GA)
    gb_ref[...] = seg(_C_GB)


def _pack_w1(w_in):
    D = w_in.shape[0]
    offs = np.cumsum([0, MLA_Q_LORA, MLA_KV_LORA, MLA_ROPE, 512, 512, 512, 512, IDX_DIM, IDX_HEADS, D, D])
    c = lambda i: w_in[:, offs[i]:offs[i + 1]]
    z = lambda n: jnp.zeros((D, n), w_in.dtype)
    k_r = c(2)
    hr = MLA_ROPE // 2
    k_r_sw = jnp.concatenate([k_r[:, hr:], k_r[:, :hr]], axis=1)
    cols = [
        c(0), c(1),
        z(MLA_NOPE), k_r, z(LANES - MLA_NOPE - MLA_ROPE),
        z(MLA_NOPE), k_r_sw, z(LANES - MLA_NOPE - MLA_ROPE),
        c(3), c(4), c(5),
        c(6) * (IDX_DIM ** -0.5),
        c(7), c(7),
        c(8), z(LANES - IDX_HEADS),
        c(9), c(10),
    ]
    w1 = jnp.concatenate(cols, axis=1)
    assert w1.shape[1] == _W1_COLS
    return w1.astype(BF16)


def _pack_wuq(w_uq):
    R = w_uq.shape[0]
    w = w_uq.reshape(R, MLA_HEADS, MLA_NOPE + MLA_ROPE)
    nope, rope = w[..., :MLA_NOPE], w[..., MLA_NOPE:]
    hr = MLA_ROPE // 2
    rope_sw = jnp.concatenate([rope[..., hr:], rope[..., :hr]], axis=-1)
    zt = jnp.zeros((R, MLA_HEADS, HEAD_PAD - MLA_NOPE - MLA_ROPE), w.dtype)
    a = jnp.concatenate([nope, rope, zt], axis=-1).reshape(R, MLA_HEADS * HEAD_PAD)
    b = jnp.concatenate([jnp.zeros_like(nope), rope_sw, zt], axis=-1).reshape(R, MLA_HEADS * HEAD_PAD)
    return jnp.concatenate([a, b], axis=1).astype(BF16)


def _pack_wukv(w_ukv):
    R = w_ukv.shape[0]
    w = w_ukv.reshape(R, MLA_HEADS, MLA_NOPE + MLA_V)
    k = jnp.concatenate([w[..., :MLA_NOPE], jnp.zeros((R, MLA_HEADS, HEAD_PAD - MLA_NOPE), w.dtype)], axis=-1)
    v = w[..., MLA_NOPE:]
    return jnp.concatenate([k.reshape(R, -1), v.reshape(R, -1)], axis=1).astype(BF16)


def _rope_tables(pos):
    half = MLA_ROPE // 2
    inv = ROPE_BASE ** (-jnp.arange(half, dtype=F32) / half)
    ang = pos.astype(F32)[:, None] * inv[None, :]
    cos, sin = jnp.cos(ang), jnp.sin(ang)
    n = pos.shape[0]
    tail = jnp.zeros((n, HEAD_PAD - MLA_NOPE - MLA_ROPE), F32)
    ct = jnp.concatenate([jnp.ones((n, MLA_NOPE), F32), cos, cos, tail], axis=1)
    st = jnp.concatenate([jnp.zeros((n, MLA_NOPE), F32), -sin, sin, tail], axis=1)
    return ct, st


def _proj_call(x2d, meta_tile, g, w1, gcq, gckv, wuq, wukv, ct, st, S):
    BS, D = x2d.shape
    nt = BS // TQ + 1
    per_b = S // TQ
    row = lambda w: pl.BlockSpec((TQ, w), lambda i: (i, 0))
    full = lambda a: pl.BlockSpec(a.shape, lambda i: (0, 0))
    tab = pl.BlockSpec((TQ, HEAD_PAD), lambda i: (jnp.where(i == nt - 1, per_b, i % per_b), 0))
    widths = [(MLA_HEADS * HEAD_PAD, BF16), (MLA_HEADS * HEAD_PAD, BF16), (None, BF16),
              (512, BF16), (512, BF16), (None, BF16), (512, BF16), (LANES, BF16), (LANES, F32),
              (D, F32), (D, F32)]
    vt_rows = MLA_HEADS * MLA_V
    vt_spec = pl.BlockSpec((1, vt_rows, TQ), lambda i: (i, 0, 0))
    vt_shape = jax.ShapeDtypeStruct((nt, vt_rows, TQ), BF16)
    return pl.pallas_call(
        _proj_kernel,
        grid=(nt,),
        in_specs=[pl.BlockSpec((TQ, D), lambda i: (jnp.minimum(i, nt - 2), 0)), full(meta_tile),
                  full(g), full(w1), full(gcq), full(gckv), full(wuq), full(wukv), tab, tab],
        out_specs=[vt_spec if w is None else row(w) for w, _ in widths],
        out_shape=[vt_shape if w is None else jax.ShapeDtypeStruct((nt * TQ, w), dt) for w, dt in widths],
        compiler_params=_params(("parallel",)),
        name="proj",
    )(x2d, meta_tile, g, w1, gcq, gckv, wuq, wukv, ct, st)


def _colmax(z):
    return jnp.max(z, axis=0, keepdims=True)


def _pv(vt, z, m, l, acc):
    p = jnp.exp2(z - m)
    return l + jnp.sum(p, axis=0, keepdims=True), acc + _dot(vt, p.astype(BF16))


def _tile_ids(nq):
    g = pl.program_id(1)
    is_meta = g == pl.num_programs(1) - 1
    return is_meta, jnp.where(is_meta, 0, g % nq)


def _chunk_mask():
    key = lax.broadcasted_iota(jnp.int32, (TQ, TQ), 0)
    qry = lax.broadcasted_iota(jnp.int32, (TQ, TQ), 1)
    shift = CHUNK.bit_length() - 1
    return (key >> shift) <= (qry >> shift)


_CHUNKS = (8, 4, 2, 1)


def _fori_chunks(lo, hi, body_n, carry):
    for n in _CHUNKS:
        cnt = (hi - lo) // n
        carry = lax.fori_loop(0, cnt, lambda u, c, lo=lo, n=n: body_n(lo + n * u, c, n), carry)
        lo = lo + cnt * n
    return carry


def _fori_each(lo, hi, body, carry):
    def body_n(t, c, n):
        for i in range(n):
            c = body(t + i, c)
        return c
    return _fori_chunks(lo, hi, body_n, carry)


def _loop(lo, hi, fn):
    _fori_each(lo, hi, lambda t, c: (fn(t), c)[1], 0)


def _mla_kernel(q_ref, k_ref, vt_ref, km_ref, vtm_ref, o_ref, s_scr, sm_scr, *, nq):
    is_meta, j = _tile_ids(nq)
    n_full = jnp.where(is_meta, 0, j)
    n_vis = jnp.where(is_meta, 0, j + 1)
    c = (MLA_NOPE + MLA_ROPE) ** -0.5 * LOG2E
    meta_ok = lax.broadcasted_iota(jnp.int32, (LANES, TQ), 0) < N_META
    diag_ok = _chunk_mask()
    heads = lambda p: (2 * p, 2 * p + 1)
    hcol = lambda h: slice(h * HEAD_PAD, (h + 1) * HEAD_PAD)
    vrow = lambda h: slice(h * MLA_V, (h + 1) * MLA_V)

    def score_meta(p):
        ms = []
        for hh, h in enumerate(heads(p)):
            z = jnp.where(meta_ok, _dot_t(km_ref[:, hcol(h)], q_ref[:, hcol(h)]) * c, NEG_INF)
            sm_scr[p % 2, hh] = z
            ms.append(_colmax(z))
        return tuple(ms)

    def score_tiles(p, t, ms, n, masked):
        ks = pl.multiple_of(t * TQ, TQ)
        out = []
        for hh, h in enumerate(heads(p)):
            z = _dot_t(k_ref[pl.ds(ks, n * TQ), hcol(h)], q_ref[:, hcol(h)]) * c
            if masked:
                z = jnp.where(diag_ok, z, NEG_INF)
            for i in range(n):
                s_scr[p % 2, hh, t + i] = z[i * TQ:(i + 1) * TQ]
            out.append(jnp.maximum(ms[hh], _colmax(z)))
        return tuple(out)

    def pv_meta(p, ms):
        st = []
        for hh, h in enumerate(heads(p)):
            st += list(_pv(vtm_ref[0, vrow(h), 0:LANES], sm_scr[p % 2, hh], ms[hh],
                           jnp.zeros((1, TQ), F32), jnp.zeros((MLA_V, TQ), F32)))
        return tuple(st)

    def pv_tiles(p, t, st, ms, n):
        for i in range(n):
            out = []
            for hh, h in enumerate(heads(p)):
                out += list(_pv(vt_ref[t + i, vrow(h), :], s_scr[p % 2, hh, t + i], ms[hh],
                                st[2 * hh], st[2 * hh + 1]))
            st = tuple(out)
        return st

    pairs = MLA_HEADS // 2
    ms = score_meta(0)
    ms = _fori_chunks(0, n_full, lambda t, m, n: score_tiles(0, t, m, n, False), ms)
    ms = lax.fori_loop(n_full, n_vis, lambda t, m: score_tiles(0, t, m, 1, True), ms)
    for p in range(pairs):
        st = pv_meta(p, ms)
        if p + 1 < pairs:
            ms_next = score_meta(p + 1)
            st, ms_next = _fori_chunks(
                0, n_full, lambda t, c, n, p=p, ms=ms: (pv_tiles(p, t, c[0], ms, n),
                                                        score_tiles(p + 1, t, c[1], n, False)), (st, ms_next))
            st, ms_next = lax.fori_loop(
                n_full, n_vis, lambda t, c, p=p, ms=ms: (pv_tiles(p, t, c[0], ms, 1),
                                                         score_tiles(p + 1, t, c[1], 1, True)), (st, ms_next))
        else:
            st = _fori_chunks(0, n_vis, lambda t, s, n, p=p, ms=ms: pv_tiles(p, t, s, ms, n), st)
            ms_next = None
        ot = jnp.concatenate([st[1] / st[0], st[3] / st[2]], axis=0)
        o_ref[:, p * 2 * MLA_V:(p + 1) * 2 * MLA_V] = ot.T.astype(BF16)
        ms = ms_next


def _mla_call(qm, km, vmt, B, S):
    nq = S // TQ
    ng = B * nq + 1
    bidx = lambda g: jnp.minimum(g // nq, B - 1)
    mrow = B * S // LANES
    wk, wv = MLA_HEADS * HEAD_PAD, MLA_HEADS * MLA_V
    return pl.pallas_call(
        functools.partial(_mla_kernel, nq=nq),
        grid=(1, ng),
        in_specs=[
            pl.BlockSpec((TQ, wk), lambda _, g: (g, 0)),
            pl.BlockSpec((S, wk), lambda _, g: (bidx(g), 0)),
            pl.BlockSpec((nq, wv, TQ), lambda _, g: (bidx(g), 0, 0)),
            pl.BlockSpec((LANES, wk), lambda _, g: (mrow, 0)),
            pl.BlockSpec((1, wv, TQ), lambda _, g: (ng - 1, 0, 0)),
        ],
        out_specs=pl.BlockSpec((TQ, wv), lambda _, g: (g, 0)),
        out_shape=jax.ShapeDtypeStruct((ng * TQ, wv), BF16),
        scratch_shapes=[pltpu.VMEM((2, 2, nq, TQ, TQ), F32),
                        pltpu.VMEM((2, 2, LANES, TQ), F32)],
        compiler_params=_params(("arbitrary", "arbitrary")),
        name="mla_attn",
    )(qm, km, vmt, km, vmt)


_IDX_BITS = 13
_B_DIAG, _B_PREV, _B_META_FIRST, _B_META_META = 0, 1, 2, 3


def _dsa_kernel(qi_ref, wi_ref, qb_ref, ki_ref, kb_ref, vbt_ref, kim_ref, kbm_ref, vbtm_ref, bias_ref, o_ref,
                keys_scr, keym_scr, selm_scr, s_scr, sm_scr, qs_scr,
                dig_scr, dm_scr, *, nq, k_sel):
    is_meta, j = _tile_ids(nq)
    n_full = jnp.where(is_meta, 0, j)
    n_vis = jnp.where(is_meta, 0, j + 1)
    lane = lax.broadcasted_iota(jnp.int32, (TQ, LANES), 1)
    low = lane < DSA_HEAD_DIM
    krow_m = lax.broadcasted_iota(jnp.int32, (LANES, TQ), 0)
    krow = lax.broadcasted_iota(jnp.int32, (TQ, TQ), 0)
    meta_ok = krow_m < N_META
    diag_ok = _chunk_mask()

    for p in range(IDX_HEADS // 2):
        qp = qi_ref[:, p * LANES:(p + 1) * LANES]
        qs_scr[2 * p] = jnp.where(low, qp, jnp.zeros_like(qp))
        qs_scr[2 * p + 1] = jnp.where(low, jnp.zeros_like(qp), qp)
    wt = (wi_ref[...] * (IDX_HEADS ** -0.5)).T

    def sort_key(score):
        bits = pltpu.bitcast(score + 0.0, jnp.int32)
        return bits ^ ((bits >> 31) & 0x7FFFFFFF)

    def score_of(kt):
        score = jnp.zeros((kt.shape[0], TQ), F32)
        for h in range(IDX_HEADS):
            score = score + wt[h:h + 1, :] * jnp.maximum(_dot_t(kt, qs_scr[h]), 0.0)
        return sort_key(score)

    keym_scr[...] = jnp.where(meta_ok, score_of(kim_ref[...]), INT_MIN)

    def score_tiles(t, n, masked):
        key = score_of(ki_ref[pl.ds(pl.multiple_of(t * TQ, TQ), n * TQ), :])
        if masked:
            key = jnp.where(diag_ok, key, INT_MIN)
        for i in range(n):
            keys_scr[t + i] = key[i * TQ:(i + 1) * TQ]

    _fori_chunks(0, n_full, lambda t, c, n: (score_tiles(t, n, False), c)[1], 0)
    lax.fori_loop(n_full, n_vis, lambda t, c: (score_tiles(t, 1, True), c)[1], 0)

    one = lambda m: jnp.where(m, 1, 0)
    fold = lambda a: jnp.sum(a.reshape(-1, 8, TQ), axis=0)

    def count(pred):
        def cbody(t, acc):
            return acc + fold(pred(keys_scr[t], LANES + t * TQ + krow))
        acc = _fori_each(0, n_vis, cbody, fold(pred(keym_scr[...], krow_m)))
        return jnp.sum(acc, axis=0, keepdims=True)

    def fold16(m):
        parts = [m[r:r + 16] for r in range(0, m.shape[0], 16)]
        while len(parts) > 1:
            parts = [a + b for a, b in zip(parts[::2], parts[1::2])]
        return parts[0].astype(F32)

    one16 = lambda m: jnp.where(m, jnp.ones((), BF16), jnp.zeros((), BF16))

    def count16(meta_scr, tile_scr, pred):
        acc = _fori_each(0, n_vis, lambda t, a: a + fold16(one16(pred(tile_scr[t]))),
                         fold16(one16(pred(meta_scr[...]))))
        return jnp.sum(acc, axis=0, keepdims=True)

    def digit_search(d_prev, shift, need):
        top, last = shift == 24, shift == 0

        def digits(key, prev):
            d = ((key >> shift) + 128 if top else (key >> shift) & 0xFF).astype(F32).astype(BF16)
            if top:
                return d
            return jnp.where(prev == d_prev.astype(F32).astype(BF16), d, -jnp.ones((), BF16))

        dm_scr[...] = digits(keym_scr[...], None if top else dm_scr[...])
        _loop(0, n_vis, lambda t: dig_scr.__setitem__(t, digits(keys_scr[t], None if top else dig_scr[t])))

        def d_body(i, c):
            d, above, at_least = c
            cand = d + lax.shift_left(jnp.int32(1), 7 - i)
            cb = cand.astype(F32).astype(BF16)
            cnt = count16(dm_scr, dig_scr, lambda v: v >= cb)
            ok = cnt >= need
            return jnp.where(ok, cand, d), jnp.where(ok, above, cnt), jnp.where(ok, cnt, at_least)

        zero = jnp.zeros((1, TQ), F32)
        all_cands = count16(dm_scr, dig_scr, lambda v: v >= jnp.zeros((), BF16)) if last else zero
        return lax.fori_loop(0, 8, d_body, (jnp.zeros((1, TQ), jnp.int32), zero, all_cands))

    t = jnp.zeros((1, TQ), jnp.int32)
    need_d = jnp.full((1, TQ), float(k_sel), F32)
    d = None
    for shift in (24, 16, 8, 0):
        d, above, at_least = digit_search(d, shift, need_d)
        t = (d - 128) if shift == 24 else (t << 8) | d
        need_d = need_d - above
    need = need_d.astype(jnp.int32)
    cnt_ge = k_sel - need + (at_least - above).astype(jnp.int32)

    tied = jnp.max(jnp.where((cnt_ge > k_sel) & (t != INT_MIN), 1, 0))

    def write_masks(sel_mask):
        selm_scr[...] = sel_mask(keym_scr[...], krow_m)
        _loop(0, n_vis, lambda tt: keys_scr.__setitem__(
            tt, pltpu.bitcast(sel_mask(keys_scr[tt], LANES + tt * TQ + krow), jnp.int32)))

    @pl.when(tied == 0)
    def _():
        t_min = jnp.maximum(t, INT_MIN + 1)
        write_masks(lambda b, _: jnp.where(b >= t_min, 0.0, NEG_INF))

    @pl.when(tied > 0)
    def _():
        def tie_body(i, r):
            cand = r | lax.shift_left(jnp.int32(1), _IDX_BITS - 1 - i)
            f = count(lambda b, idx: jnp.where(b == t, one(idx < cand), 0))
            return jnp.where(f < need, cand, r)
        mstar = lax.fori_loop(0, _IDX_BITS, tie_body, jnp.zeros((1, TQ), jnp.int32)) + 1

        def sel_mask(b, idx):
            tie_ok = jnp.where(b == t, one(idx < mstar), 0)
            sel = jnp.where(b == INT_MIN, 0, jnp.where(b > t, 1, tie_ok))
            return jnp.where(sel > 0, 0.0, NEG_INF)

        write_masks(sel_mask)

    c = DSA_HEAD_DIM ** -0.5 * LOG2E
    n_far = jnp.maximum(n_full - 1, 0)
    use_mb = jnp.where(is_meta | (j == 0), 1.0, 0.0)
    mb_idx = jnp.where(is_meta, _B_META_META, _B_META_FIRST)
    pcols = lambda p: slice(p * LANES, (p + 1) * LANES)
    vrow = lambda h: slice(h * DSA_HEAD_DIM, (h + 1) * DSA_HEAD_DIM)

    def masked_q(p):
        qp = qb_ref[:, pcols(p)]
        return [jnp.where(low, qp, jnp.zeros_like(qp)), jnp.where(low, jnp.zeros_like(qp), qp)]

    def score_meta(p):
        qs = masked_q(p)
        ms = []
        for hh in range(2):
            z = (_dot_t(kbm_ref[:, pcols(p)], qs[hh]) * c + bias_ref[2 * p + hh, mb_idx][:LANES, :] * use_mb
                 + selm_scr[...])
            sm_scr[p % 2, hh] = z
            ms.append(_colmax(z))
        return tuple(ms)

    def score_tiles(p, t, ms, n, near):
        qs = masked_q(p)
        k = kb_ref[pl.ds(pl.multiple_of(t * TQ, TQ), n * TQ), pcols(p)]
        out = []
        for hh in range(2):
            zz = _dot_t(k, qs[hh]) * c
            m = ms[hh]
            for i in range(n):
                z = zz[i * TQ:(i + 1) * TQ] + pltpu.bitcast(keys_scr[t + i], F32)
                if near is not None:
                    z = z + bias_ref[2 * p + hh, near]
                s_scr[p % 2, hh, t + i] = z
                m = jnp.maximum(m, _colmax(z))
            out.append(m)
        return tuple(out)

    def pv_meta(p, ms):
        st = []
        for hh in range(2):
            st += list(_pv(vbtm_ref[0, vrow(2 * p + hh), 0:LANES], sm_scr[p % 2, hh], ms[hh],
                           jnp.zeros((1, TQ), F32), jnp.zeros((DSA_HEAD_DIM, TQ), F32)))
        return tuple(st)

    def pv_tiles(p, t, st, ms, n):
        for i in range(n):
            out = []
            for hh in range(2):
                out += list(_pv(vbt_ref[t + i, vrow(2 * p + hh), :], s_scr[p % 2, hh, t + i], ms[hh],
                                st[2 * hh], st[2 * hh + 1]))
            st = tuple(out)
        return st

    def score_all(p, ms):
        ms = _fori_chunks(0, n_far, lambda t, m, n: score_tiles(p, t, m, n, None), ms)
        ms = lax.fori_loop(n_far, n_full, lambda t, m: score_tiles(p, t, m, 1, _B_PREV), ms)
        return lax.fori_loop(n_full, n_vis, lambda t, m: score_tiles(p, t, m, 1, _B_DIAG), ms)

    pairs = DSA_HEADS // 2
    ms = score_all(0, score_meta(0))
    for p in range(pairs):
        st = pv_meta(p, ms)
        if p + 1 < pairs:
            both = lambda near, n=1, p=p, ms=ms: (
                lambda t, c, n=n: (pv_tiles(p, t, c[0], ms, n), score_tiles(p + 1, t, c[1], n, near)))
            carry = (st, score_meta(p + 1))
            carry = _fori_chunks(0, n_far, lambda t, c, n, p=p, ms=ms: (
                pv_tiles(p, t, c[0], ms, n), score_tiles(p + 1, t, c[1], n, None)), carry)
            carry = lax.fori_loop(n_far, n_full, both(_B_PREV), carry)
            st, ms_next = lax.fori_loop(n_full, n_vis, both(_B_DIAG), carry)
        else:
            st = _fori_chunks(0, n_vis, lambda t, s, n, p=p, ms=ms: pv_tiles(p, t, s, ms, n), st)
            ms_next = None
        ot = jnp.concatenate([st[1] / st[0], st[3] / st[2]], axis=0)
        o_ref[:, pcols(p)] = ot.T.astype(BF16)
        ms = ms_next


def _dsa_call(qi, wi, qb, ki, kb, vbt, bias, B, S, k_sel):
    nq = S // TQ
    ng = B * nq + 1
    W = DSA_HEADS * DSA_HEAD_DIM
    bidx = lambda g: jnp.minimum(g // nq, B - 1)
    mrow = B * S // LANES
    qrow = lambda w: pl.BlockSpec((TQ, w), lambda _, g: (g, 0))
    kv = lambda w: pl.BlockSpec((S, w), lambda _, g: (bidx(g), 0))
    kvm = lambda w: pl.BlockSpec((LANES, w), lambda _, g: (mrow, 0))
    return pl.pallas_call(
        functools.partial(_dsa_kernel, nq=nq, k_sel=k_sel),
        grid=(1, ng),
        in_specs=[qrow(W), qrow(LANES), qrow(W), kv(LANES), kv(W),
                  pl.BlockSpec((nq, W, TQ), lambda _, g: (bidx(g), 0, 0)),
                  kvm(LANES), kvm(W),
                  pl.BlockSpec((1, W, TQ), lambda _, g: (ng - 1, 0, 0)),
                  pl.BlockSpec(bias.shape, lambda _, g: (0, 0, 0, 0), pipeline_mode=pl.Buffered(1))],
        out_specs=qrow(W),
        out_shape=jax.ShapeDtypeStruct((ng * TQ, W), BF16),
        scratch_shapes=[
            pltpu.VMEM((nq, TQ, TQ), jnp.int32),
            pltpu.VMEM((LANES, TQ), jnp.int32),
            pltpu.VMEM((LANES, TQ), F32),
            pltpu.VMEM((2, 2, nq, TQ, TQ), F32),
            pltpu.VMEM((2, 2, LANES, TQ), F32),
            pltpu.VMEM((IDX_HEADS, TQ, LANES), BF16),
            pltpu.VMEM((nq, TQ, TQ), BF16),
            pltpu.VMEM((LANES, TQ), BF16),
        ],
        compiler_params=_params(("arbitrary", "arbitrary")),
        name="dsa",
    )(qi, wi, qb, ki, kb, vbt, ki, kb, vbt, bias)


def _t5_bucket(rel):
    nb = REL_BUCKETS // 2
    max_exact = nb // 2
    n = jnp.abs(rel)
    large = max_exact + (jnp.log(jnp.maximum(n, 1).astype(F32) / max_exact)
                         / math.log(REL_MAX_DIST / max_exact) * (nb - max_exact)).astype(jnp.int32)
    large = jnp.minimum(large, nb - 1)
    return jnp.where(rel > 0, nb, 0) + jnp.where(n < max_exact, n, large)


def _bias_kernel(bucket_ref, rb_ref, o_ref, *, far_bucket):
    h = pl.program_id(0)
    far = rb_ref[far_bucket, h]
    for tile in range(bucket_ref.shape[0]):
        bkt = bucket_ref[tile]
        acc = jnp.zeros(bkt.shape, F32)
        for b in range(REL_BUCKETS):
            acc = jnp.where(bkt == b, rb_ref[b, h], acc)
        o_ref[0, tile] = (acc - far) * LOG2E


def _bias_tiles(rel_bias):
    k = jnp.arange(TQ, dtype=jnp.int32)[:, None]
    q = jnp.arange(TQ, dtype=jnp.int32)[None, :]
    rels = jnp.stack([k - q, k - q - TQ, k - (q + N_META), k - q])
    far_bucket = REL_BUCKETS // 2 - 1
    return pl.pallas_call(
        functools.partial(_bias_kernel, far_bucket=far_bucket),
        grid=(DSA_HEADS,),
        in_specs=[pl.BlockSpec((4, TQ, TQ), lambda h: (0, 0, 0)),
                  pl.BlockSpec(memory_space=pltpu.SMEM)],
        out_specs=pl.BlockSpec((1, 4, TQ, TQ), lambda h: (h, 0, 0, 0)),
        out_shape=jax.ShapeDtypeStruct((DSA_HEADS, 4, TQ, TQ), F32),
        compiler_params=_params(("parallel",)),
        name="bias_tiles",
    )(_t5_bucket(rels), rel_bias)


_MERGE_ROWS = 128


def _merge_kernel(x_ref, meta_ref, oa_ref, ob_ref, ga_ref, gb_ref, wa_ref, wb_ref, wo_ref, gf_ref, wrt_ref, brt_ref,
                  h1_ref, route_ref, gate_ref, cnt_ref, run_scr):
    is_meta = pl.program_id(0) == pl.num_programs(0) - 1
    for rows in (slice(s, s + _MERGE_ROWS) for s in range(0, x_ref.shape[0], _MERGE_ROWS)):
        h = jnp.where(is_meta, meta_ref[rows, :], x_ref[rows, :])
        y = (jax.nn.sigmoid(ga_ref[rows, :]) * _dot(oa_ref[rows, :], wa_ref[...])
             + jax.nn.sigmoid(gb_ref[rows, :]) * _dot(ob_ref[rows, :], wb_ref[...]))
        h1_ref[rows, :] = h + _dot(y.astype(BF16), wo_ref[...])
    _rank_tile(h1_ref[...], gf_ref[...], wrt_ref[...], brt_ref[...], route_ref, gate_ref, cnt_ref, run_scr)


def _merge_call(x2d, meta_tile, oa, ob, ga, gb, wa, wb, wo, gf, wrt, brt):
    BS, D = x2d.shape
    nt = BS // TQ + 1
    R = nt * TQ
    row = lambda w: pl.BlockSpec((TQ, w), lambda i: (i, 0))
    full = lambda a: pl.BlockSpec(a.shape, lambda i: (0, 0))
    return pl.pallas_call(
        _merge_kernel,
        grid=(nt,),
        in_specs=[pl.BlockSpec((TQ, D), lambda i: (jnp.minimum(i, nt - 2), 0)), full(meta_tile),
                  row(oa.shape[1]), row(ob.shape[1]), row(D), row(D), full(wa), full(wb), full(wo),
                  full(gf), full(wrt), full(brt)],
        out_specs=[row(D), pl.BlockSpec((8, TQ), lambda i: (0, i)), row(LANES),
                   pl.BlockSpec((N_EXPERTS, LANES), lambda i: (0, 0))],
        out_shape=[jax.ShapeDtypeStruct((R, D), F32), jax.ShapeDtypeStruct((8, R), jnp.int32),
                   jax.ShapeDtypeStruct((R, LANES), F32), jax.ShapeDtypeStruct((N_EXPERTS, LANES), jnp.int32)],
        scratch_shapes=[pltpu.VMEM((N_EXPERTS, LANES), F32)],
        compiler_params=_params(("arbitrary",)),
        name="merge_route",
    )(x2d, meta_tile, oa, ob, ga, gb, wa, wb, wo, gf, wrt, brt)


_ROUTER_ROWS = 40
TE = 256
_ISSUE_UNROLL = TQ


def _route_rows(xn, wrt, brt):
    nr = _ROUTER_ROWS
    tm = xn.shape[0]
    def split(a):
        hi = a.astype(BF16)
        return hi, (a - hi.astype(F32)).astype(BF16)

    (w_hi, w_lo), (x_hi, x_lo) = split(wrt), split(xn)
    lt = _dot_t(w_hi, x_hi) + (_dot_t(w_hi, x_lo) + _dot_t(w_lo, x_hi)) + brt
    row = lax.broadcasted_iota(jnp.int32, (nr, tm), 0)
    ninf = -jnp.inf
    cmax = lambda a: jnp.max(a, axis=0, keepdims=True)
    cmin = lambda a: jnp.min(a, axis=0, keepdims=True)
    gl = jnp.where((row >= N_EXPERTS) & (row < N_EXPERTS + N_GROUPS), lt, ninf)
    gmax = cmax(gl)
    gsel = cmin(jnp.where(gl == gmax, row, nr)) - N_EXPERTS
    p_group = 1.0 / jnp.sum(jnp.exp(gl - gmax), axis=0, keepdims=True)
    lo = gsel * EXPERTS_PER_GROUP
    el = jnp.where((row >= lo) & (row < lo + EXPERTS_PER_GROUP), lt, ninf)
    m1 = cmax(el)
    i1 = cmin(jnp.where(el == m1, row, nr))
    el2 = jnp.where(row == i1, ninf, el)
    m2 = cmax(el2)
    i2 = cmin(jnp.where(el2 == m2, row, nr))
    e2 = jnp.exp(m2 - m1)
    return i1, i2, p_group / (1.0 + e2), p_group * e2 / (1.0 + e2)


def _rank_tile(h1, gf, wrt, brt, route_ref, gate_ref, cnt_ref, run_scr):
    @pl.when(pl.program_id(0) == 0)
    def _():
        run_scr[...] = jnp.zeros_like(run_scr)

    tm = h1.shape[0]
    i1, i2, w1, w2 = _route_rows(_rms(h1, gf), wrt, brt)
    row = lax.broadcasted_iota(jnp.int32, (N_EXPERTS, tm), 0)
    o1 = jnp.where(row == i1, 1.0, 0.0)
    o2 = jnp.where(row == i2, 1.0, 0.0)
    a = lax.broadcasted_iota(jnp.int32, (tm, tm), 0)
    b = lax.broadcasted_iota(jnp.int32, (tm, tm), 1)
    before = jnp.where(a < b, 1.0, 0.0).astype(BF16)
    p1 = _dot(o1.astype(BF16), before)
    p2 = _dot(o2.astype(BF16), before)
    run = run_scr[:, 0:1]
    c1 = jnp.sum(o1, axis=1, keepdims=True)
    c2 = jnp.sum(o2, axis=1, keepdims=True)
    r1 = jnp.sum(o1 * (run + p1), axis=0, keepdims=True)
    r2 = jnp.sum(o2 * (run + c1 + p2), axis=0, keepdims=True)
    new_run = run + c1 + c2
    run_scr[...] = jnp.broadcast_to(new_run, run_scr.shape)
    cnt_ref[...] = jnp.broadcast_to(new_run, cnt_ref.shape).astype(jnp.int32)
    z = jnp.zeros((4, tm), jnp.int32)
    route_ref[...] = jnp.concatenate([i1, r1.astype(jnp.int32), i2, r2.astype(jnp.int32), z], axis=0)
    gt = jnp.concatenate([w1, w2, jnp.zeros((LANES - 2, tm), F32)], axis=0)
    gate_ref[...] = gt.T


_PAD_BITS = tuple(1 << b for b in reversed(range(TE.bit_length() - 1)))


def _dispatch_kernel(pos_ref, pad_ref, h1_ref, gf_ref, xs_hbm, buf, zbuf, sem, zsem, *, n_rows):
    i = pl.program_id(0)
    nt = pl.num_programs(0)
    slot = i % 2
    tm = h1_ref.shape[0]

    @pl.when(i == 0)
    def _():
        zbuf[...] = jnp.zeros_like(zbuf)

        def pad_copies(e, wait):
            start, n = pad_ref[e], pad_ref[N_EXPERTS + e]

            def copy(dst, rows):
                cp = pltpu.make_async_copy(zbuf.at[pl.ds(0, rows), :], xs_hbm.at[pl.ds(dst, rows), :], zsem)
                cp.wait() if wait else cp.start()

            end = start + n
            for b in (b for b in _PAD_BITS if b >= 8):
                end = end - (n & b)
                pl.when((n & b) != 0)(functools.partial(copy, pl.multiple_of(end, 8), b))
            for r in range(7):
                pl.when(r < (n & 7))(functools.partial(copy, start + r, 1))

        def tail_copies(blk, wait):
            for h in range(0, TE, zbuf.shape[0]):
                cp = pltpu.make_async_copy(
                    zbuf, xs_hbm.at[pl.ds(pl.multiple_of(blk * TE + h, zbuf.shape[0]), zbuf.shape[0]), :], zsem)
                cp.wait() if wait else cp.start()

        n_used, n_blk = pad_ref[2 * N_EXPERTS], xs_hbm.shape[0] // TE
        for wait in (False, True):
            _loop_plain(N_EXPERTS, lambda e: pad_copies(e, wait))
            lax.fori_loop(n_used, n_blk, lambda b, c: (tail_copies(b, wait), c)[1], 0)

    def wait_slot(s):
        for _ in range(2):
            pltpu.make_async_copy(buf.at[s], xs_hbm.at[pl.ds(0, tm), :], sem.at[s]).wait()

    @pl.when(i >= 2)
    def _():
        wait_slot(slot)

    buf[slot] = _rms(h1_ref[...], gf_ref[...])

    def issue(u, c):
        for v in range(_ISSUE_UNROLL):
            r = u * _ISSUE_UNROLL + v
            src = buf.at[slot, pl.ds(r, 1), :]
            for k in range(2):
                p = pos_ref[k * n_rows + i * tm + r]
                pltpu.make_async_copy(src, xs_hbm.at[pl.ds(p, 1), :], sem.at[slot]).start()
        return c

    lax.fori_loop(0, tm // _ISSUE_UNROLL, issue, 0)

    @pl.when(i == nt - 1)
    def _():
        wait_slot(slot)

        @pl.when(nt >= 2)
        def _():
            wait_slot(1 - slot)


def _loop_plain(n, fn):
    lax.fori_loop(0, n, lambda e, c: (fn(e), c)[1], 0)


def _dispatch_call(pos, pad_runs, h1, gf, n_slots):
    R, D = h1.shape
    return pl.pallas_call(
        functools.partial(_dispatch_kernel, n_rows=R),
        grid_spec=pltpu.PrefetchScalarGridSpec(
            num_scalar_prefetch=2, grid=(R // TQ,),
            in_specs=[pl.BlockSpec((TQ, D), lambda i, pos, pad: (i, 0)),
                      pl.BlockSpec(gf.shape, lambda i, pos, pad: (0, 0))],
            out_specs=pl.BlockSpec(memory_space=pl.ANY),
            scratch_shapes=[pltpu.VMEM((2, TQ, D), F32), pltpu.VMEM((_PAD_BITS[0], D), F32),
                            pltpu.SemaphoreType.DMA((2,)), pltpu.SemaphoreType.DMA]),
        out_shape=jax.ShapeDtypeStruct((n_slots, D), F32),
        compiler_params=_params(("arbitrary",)),
        name="moe_dispatch",
    )(pos, pad_runs, h1, gf)


def _ffn_kernel(be_ref, nu_ref, x_ref, wg_ref, wu_ref, wd_ref, y_ref):
    i = pl.program_id(0)

    @pl.when(i < nu_ref[0])
    def _():
        x = x_ref[...].astype(BF16)
        a = _dot(x, wg_ref[0].astype(BF16))
        u = _dot(x, wu_ref[0].astype(BF16))
        hmid = (a * jax.nn.sigmoid(a) * u).astype(BF16)
        y_ref[...] = _dot(hmid, wd_ref[0].astype(BF16))

    @pl.when(i >= nu_ref[0])
    def _():
        y_ref[...] = jnp.zeros_like(y_ref)


def _ffn_call(blk_expert, n_used, xs, wg, wu, wd):
    NS, D = xs.shape
    wspec = lambda shp: pl.BlockSpec((1,) + shp, lambda i, be, nu: (be[i], 0, 0))
    return pl.pallas_call(
        _ffn_kernel,
        grid_spec=pltpu.PrefetchScalarGridSpec(
            num_scalar_prefetch=2, grid=(NS // TE,),
            in_specs=[pl.BlockSpec((TE, D), lambda i, be, nu: (jnp.minimum(i, nu[0] - 1), 0)),
                      wspec((D, D_EXPERT)), wspec((D, D_EXPERT)), wspec((D_EXPERT, D))],
            out_specs=pl.BlockSpec((TE, D), lambda i, be, nu: (i, 0))),
        out_shape=jax.ShapeDtypeStruct((NS, D), F32),
        compiler_params=_params(("arbitrary",)),
        name="moe_ffn",
    )(blk_expert, n_used, xs, wg, wu, wd)


def _combine_kernel(pos_ref, h1_ref, gate_ref, gfin_ref, ys_hbm, o_ref, buf, sem, *, n_rows):
    i = pl.program_id(0)
    nt = pl.num_programs(0)
    slot = i % 2
    tm = h1_ref.shape[0]

    def fetch(tile, s):
        def issue(u, c):
            for v in range(_ISSUE_UNROLL):
                r = u * _ISSUE_UNROLL + v
                for k in range(2):
                    p = pos_ref[k * n_rows + tile * tm + r]
                    pltpu.make_async_copy(ys_hbm.at[pl.ds(p, 1), :], buf.at[s, k, pl.ds(r, 1), :],
                                          sem.at[s]).start()
            return c
        lax.fori_loop(0, tm // _ISSUE_UNROLL, issue, 0)

    @pl.when(i == 0)
    def _():
        fetch(0, 0)

    @pl.when(i + 1 < nt)
    def _():
        fetch(i + 1, 1 - slot)

    for k in range(2):
        pltpu.make_async_copy(ys_hbm.at[pl.ds(0, tm), :], buf.at[slot, k], sem.at[slot]).wait()

    g = gate_ref[...]
    ffn = g[:, 0:1] * buf[slot, 0] + g[:, 1:2] * buf[slot, 1]

    @pl.when(i < nt - 1)
    def _():
        o_ref[...] = _rms(h1_ref[...] + ffn, gfin_ref[...])


def _combine_call(pos, h1, gates, gfin, ys, n_out):
    R, D = h1.shape
    nt = R // TQ
    return pl.pallas_call(
        functools.partial(_combine_kernel, n_rows=R),
        grid_spec=pltpu.PrefetchScalarGridSpec(
            num_scalar_prefetch=1, grid=(nt,),
            in_specs=[pl.BlockSpec((TQ, D), lambda i, pos: (i, 0)),
                      pl.BlockSpec((TQ, LANES), lambda i, pos: (i, 0)),
                      pl.BlockSpec(gfin.shape, lambda i, pos: (0, 0)),
                      pl.BlockSpec(memory_space=pl.ANY)],
            out_specs=pl.BlockSpec((TQ, D), lambda i, pos: (jnp.minimum(i, nt - 2), 0)),
            scratch_shapes=[pltpu.VMEM((2, 2, TQ, D), F32), pltpu.SemaphoreType.DMA((2,))]),
        out_shape=jax.ShapeDtypeStruct((n_out, D), F32),
        compiler_params=_params(("arbitrary",)),
        name="moe_combine",
    )(pos, h1, gates, gfin, ys)


def _sparse_moe(h1, route, gates, cnt, gf, wg, wu, wd, gfin, n_out):
    R, D = h1.shape
    counts = cnt[:, 0]
    padded = (counts + TE - 1) // TE * TE
    pad_end = jnp.cumsum(padded)
    offs = pad_end - padded
    nb = -(-2 * R // TE) + N_EXPERTS
    eids = jnp.arange(N_EXPERTS, dtype=jnp.int32)
    slot_of = lambda e, r: r + jnp.sum(jnp.where(e[:, None] == eids[None, :], offs[None, :], 0), axis=1)
    pos = jnp.concatenate([slot_of(route[0], route[1]), slot_of(route[2], route[3])]).astype(jnp.int32)
    blk_start = jnp.arange(nb, dtype=jnp.int32) * TE
    blk_expert = jnp.minimum(jnp.sum((blk_start[:, None] >= pad_end[None, :]).astype(jnp.int32), axis=1),
                             N_EXPERTS - 1).astype(jnp.int32)
    n_used = (pad_end[-1] // TE).astype(jnp.int32).reshape(1)
    pad_runs = jnp.concatenate([offs + counts, padded - counts, n_used]).astype(jnp.int32)
    xs = _dispatch_call(pos, pad_runs, h1, gf, nb * TE)
    ys = _ffn_call(blk_expert, n_used, xs, wg, wu, wd)
    return _combine_call(pos, h1, gates, gfin, ys, n_out)


def kernel(x, meta_tokens, norm_mix_g, w_in, mla_cq_norm_g, mla_ckv_norm_g, w_mla_uq, w_mla_ukv,
           w_branch_a, w_branch_b, w_out, rel_bias, norm_ffn_g, w_router_group, b_router_group,
           w_router_expert, b_router_expert, w_exp_gate, w_exp_up, w_exp_down, norm_final_g):
    B, S, D = x.shape
    assert S % TQ == 0 and norm_mix_g.shape[0] == 1
    k_sel = min(K_SEL_MAX, S // 4)
    x2d = x.reshape(B * S, D)
    meta_tile = jnp.concatenate([meta_tokens.astype(x.dtype), jnp.zeros((TQ - N_META, D), x.dtype)], axis=0)

    pos = np.concatenate([N_META + np.arange(S), np.minimum(np.arange(TQ), N_META)]).astype(np.int32)
    ct, st = _rope_tables(jnp.asarray(pos))

    qm, km, vmt, qb, kb, vbt, qi, ki, wi, ga, gb = _proj_call(
        x2d, meta_tile, norm_mix_g.reshape(1, D), _pack_w1(w_in[0]), mla_cq_norm_g.reshape(1, -1),
        mla_ckv_norm_g.reshape(1, -1), _pack_wuq(w_mla_uq[0]), _pack_wukv(w_mla_ukv[0]), ct, st, S)

    o_a = _mla_call(qm, km, vmt, B, S)
    o_b = _dsa_call(qi, wi, qb, ki, kb, vbt, _bias_tiles(rel_bias), B, S, k_sel)

    pad_r = _ROUTER_ROWS - N_EXPERTS - N_GROUPS
    w_r = jnp.concatenate([w_router_expert[0].T, w_router_group[0].T, jnp.zeros((pad_r, D), F32)], axis=0)
    b_r = jnp.concatenate([b_router_expert[0], b_router_group[0],
                           jnp.zeros((pad_r,), F32)]).reshape(_ROUTER_ROWS, 1)
    gf = norm_ffn_g.reshape(1, D)
    h1, route, gates, cnt = _merge_call(x2d, meta_tile, o_a, o_b, ga, gb, w_branch_a[0].astype(BF16),
                                        w_branch_b[0].astype(BF16), w_out[0].astype(BF16), gf, w_r, b_r)
    out = _sparse_moe(h1, route, gates, cnt, gf, w_exp_gate[0], w_exp_up[0], w_exp_down[0],
                      norm_final_g.reshape(1, D), B * S)
    return out.reshape(B, S, D)
```

```python
import functools
import math

import numpy as np
import jax
import jax.numpy as jnp
from jax import lax
from jax.experimental import pallas as pl
from jax.experimental.pallas import tpu as pltpu

CHUNK = 64
N_META = 16
NEG_INF = -1e30
RMS_EPS = 1e-6
ROPE_BASE = 10000.0
MLA_HEADS = 8
MLA_Q_LORA = 256
MLA_KV_LORA = 128
MLA_NOPE = 64
MLA_ROPE = 32
MLA_V = 64
DSA_HEADS = 8
DSA_HEAD_DIM = 64
IDX_HEADS = 8
IDX_DIM = 64
K_SEL_MAX = 256
REL_BUCKETS = 32
REL_MAX_DIST = 128
N_GROUPS = 4
EXPERTS_PER_GROUP = 8
N_EXPERTS = N_GROUPS * EXPERTS_PER_GROUP
D_EXPERT = 256

LANES = 128
TQ = 256
HEAD_PAD = 128
VMEM_LIMIT = 56 * 1024 * 1024
INT_MIN = -2 ** 31
LOG2E = math.log2(math.e)

F32 = jnp.float32
BF16 = jnp.bfloat16


def _params(sem):
    return pltpu.CompilerParams(dimension_semantics=sem, vmem_limit_bytes=VMEM_LIMIT)


def _rms(x, g):
    return x * lax.rsqrt(jnp.mean(x * x, axis=-1, keepdims=True) + RMS_EPS) * g


def _dot(a, b):
    return jnp.dot(a, b, preferred_element_type=F32)


def _dot_t(a, b):
    return lax.dot_general(a, b, (((1,), (1,)), ((), ())), preferred_element_type=F32)


_C_CQ = (0, 256)
_C_CKV = (256, 384)
_C_KRX = (384, 512)
_C_KRY = (512, 640)
_C_QB = (640, 1152)
_C_KB = (1152, 1664)
_C_VB = (1664, 2176)
_C_QI = (2176, 2688)
_C_KI = (2688, 2816)
_C_WI = (2816, 2944)
_C_GA = (2944, 3968)
_C_GB = (3968, 4992)
_W1_COLS = 4992


def _proj_kernel(x_ref, meta_ref, g_ref, w1_ref, gcq_ref, gckv_ref, wuq_ref, wukv_ref, ct_ref, st_ref,
                 qm_ref, km_ref, vm_ref, qb_ref, kb_ref, vb_ref, qi_ref, ki_ref, wi_ref,
                 ga_ref, gb_ref):
    is_meta = pl.program_id(0) == pl.num_programs(0) - 1
    h = jnp.where(is_meta, meta_ref[...], x_ref[...])
    xb = _rms(h, g_ref[...]).astype(BF16)

    def seg(c):
        return _dot(xb, w1_ref[:, c[0]:c[1]])

    ct = ct_ref[...]
    st = st_ref[...]
    nq = _rms(seg(_C_CQ), gcq_ref[...]).astype(BF16)
    qa = _dot(nq, wuq_ref[...])
    half = MLA_HEADS * HEAD_PAD
    for hd in range(MLA_HEADS):
        lo, hi = hd * HEAD_PAD, (hd + 1) * HEAD_PAD
        qm_ref[:, lo:hi] = (qa[:, lo:hi] * ct + qa[:, half + lo:half + hi] * st).astype(BF16)
    nkv = _rms(seg(_C_CKV), gckv_ref[...]).astype(BF16)
    kva = _dot(nkv, wukv_ref[...])
    kr = seg(_C_KRX) * ct + seg(_C_KRY) * st
    for hd in range(MLA_HEADS):
        lo, hi = hd * HEAD_PAD, (hd + 1) * HEAD_PAD
        km_ref[:, lo:hi] = (kva[:, lo:hi] + kr).astype(BF16)
    vm_ref[0] = kva[:, half:].T.astype(BF16)
    qb_ref[...] = seg(_C_QB).astype(BF16)
    kb_ref[...] = seg(_C_KB).astype(BF16)
    vb_ref[0] = seg(_C_VB).T.astype(BF16)
    qi_ref[...] = seg(_C_QI).astype(BF16)
    ki_ref[...] = seg(_C_KI).astype(BF16)
    wi_ref[...] = seg(_C_WI)
    ga_ref[...] = seg(_C_GA)
    gb_ref[...] = seg(_C_GB)


def _pack_w1(w_in):
    D = w_in.shape[0]
    offs = np.cumsum([0, MLA_Q_LORA, MLA_KV_LORA, MLA_ROPE, 512, 512, 512, 512, IDX_DIM, IDX_HEADS, D, D])
    c = lambda i: w_in[:, offs[i]:offs[i + 1]]
    z = lambda n: jnp.zeros((D, n), w_in.dtype)
    k_r = c(2)
    hr = MLA_ROPE // 2
    k_r_sw = jnp.concatenate([k_r[:, hr:], k_r[:, :hr]], axis=1)
    cols = [
        c(0), c(1),
        z(MLA_NOPE), k_r, z(LANES - MLA_NOPE - MLA_ROPE),
        z(MLA_NOPE), k_r_sw, z(LANES - MLA_NOPE - MLA_ROPE),
        c(3), c(4), c(5),
        c(6) * (IDX_DIM ** -0.5),
        c(7), c(7),
        c(8), z(LANES - IDX_HEADS),
        c(9), c(10),
    ]
    w1 = jnp.concatenate(cols, axis=1)
    assert w1.shape[1] == _W1_COLS
    return w1.astype(BF16)


def _pack_wuq(w_uq):
    R = w_uq.shape[0]
    w = w_uq.reshape(R, MLA_HEADS, MLA_NOPE + MLA_ROPE)
    nope, rope = w[..., :MLA_NOPE], w[..., MLA_NOPE:]
    hr = MLA_ROPE // 2
    rope_sw = jnp.concatenate([rope[..., hr:], rope[..., :hr]], axis=-1)
    zt = jnp.zeros((R, MLA_HEADS, HEAD_PAD - MLA_NOPE - MLA_ROPE), w.dtype)
    a = jnp.concatenate([nope, rope, zt], axis=-1).reshape(R, MLA_HEADS * HEAD_PAD)
    b = jnp.concatenate([jnp.zeros_like(nope), rope_sw, zt], axis=-1).reshape(R, MLA_HEADS * HEAD_PAD)
    return jnp.concatenate([a, b], axis=1).astype(BF16)


def _pack_wukv(w_ukv):
    R = w_ukv.shape[0]
    w = w_ukv.reshape(R, MLA_HEADS, MLA_NOPE + MLA_V)
    k = jnp.concatenate([w[..., :MLA_NOPE], jnp.zeros((R, MLA_HEADS, HEAD_PAD - MLA_NOPE), w.dtype)], axis=-1)
    v = w[..., MLA_NOPE:]
    return jnp.concatenate([k.reshape(R, -1), v.reshape(R, -1)], axis=1).astype(BF16)


def _rope_tables(pos):
    half = MLA_ROPE // 2
    inv = ROPE_BASE ** (-jnp.arange(half, dtype=F32) / half)
    ang = pos.astype(F32)[:, None] * inv[None, :]
    cos, sin = jnp.cos(ang), jnp.sin(ang)
    n = pos.shape[0]
    tail = jnp.zeros((n, HEAD_PAD - MLA_NOPE - MLA_ROPE), F32)
    ct = jnp.concatenate([jnp.ones((n, MLA_NOPE), F32), cos, cos, tail], axis=1)
    st = jnp.concatenate([jnp.zeros((n, MLA_NOPE), F32), -sin, sin, tail], axis=1)
    return ct, st


def _proj_call(x2d, meta_tile, g, w1, gcq, gckv, wuq, wukv, ct, st, S):
    BS, D = x2d.shape
    nt = BS // TQ + 1
    per_b = S // TQ
    row = lambda w: pl.BlockSpec((TQ, w), lambda i: (i, 0))
    full = lambda a: pl.BlockSpec(a.shape, lambda i: (0, 0))
    tab = pl.BlockSpec((TQ, HEAD_PAD), lambda i: (jnp.where(i == nt - 1, per_b, i % per_b), 0))
    widths = [(MLA_HEADS * HEAD_PAD, BF16), (MLA_HEADS * HEAD_PAD, BF16), (None, BF16),
              (512, BF16), (512, BF16), (None, BF16), (512, BF16), (LANES, BF16), (LANES, F32),
              (D, F32), (D, F32)]
    vt_rows = MLA_HEADS * MLA_V
    vt_spec = pl.BlockSpec((1, vt_rows, TQ), lambda i: (i, 0, 0))
    vt_shape = jax.ShapeDtypeStruct((nt, vt_rows, TQ), BF16)
    return pl.pallas_call(
        _proj_kernel,
        grid=(nt,),
        in_specs=[pl.BlockSpec((TQ, D), lambda i: (jnp.minimum(i, nt - 2), 0)), full(meta_tile),
                  full(g), full(w1), full(gcq), full(gckv), full(wuq), full(wukv), tab, tab],
        out_specs=[vt_spec if w is None else row(w) for w, _ in widths],
        out_shape=[vt_shape if w is None else jax.ShapeDtypeStruct((nt * TQ, w), dt) for w, dt in widths],
        compiler_params=_params(("parallel",)),
        name="proj",
    )(x2d, meta_tile, g, w1, gcq, gckv, wuq, wukv, ct, st)


def _colmax(z):
    return jnp.max(z, axis=0, keepdims=True)


def _pv(vt, z, m, l, acc):
    p = jnp.exp2(z - m)
    return l + jnp.sum(p, axis=0, keepdims=True), acc + _dot(vt, p.astype(BF16))


def _tile_ids(nq):
    g = pl.program_id(1)
    is_meta = g == pl.num_programs(1) - 1
    return is_meta, jnp.where(is_meta, 0, g % nq)


def _chunk_mask():
    key = lax.broadcasted_iota(jnp.int32, (TQ, TQ), 0)
    qry = lax.broadcasted_iota(jnp.int32, (TQ, TQ), 1)
    shift = CHUNK.bit_length() - 1
    return (key >> shift) <= (qry >> shift)


_CHUNKS = (8, 4, 2, 1)


def _fori_chunks(lo, hi, body_n, carry):
    for n in _CHUNKS:
        cnt = (hi - lo) // n
        carry = lax.fori_loop(0, cnt, lambda u, c, lo=lo, n=n: body_n(lo + n * u, c, n), carry)
        lo = lo + cnt * n
    return carry


def _fori_each(lo, hi, body, carry):
    def body_n(t, c, n):
        for i in range(n):
            c = body(t + i, c)
        return c
    return _fori_chunks(lo, hi, body_n, carry)


def _loop(lo, hi, fn):
    _fori_each(lo, hi, lambda t, c: (fn(t), c)[1], 0)


def _mla_kernel(q_ref, k_ref, vt_ref, km_ref, vtm_ref, o_ref, s_scr, sm_scr, *, nq):
    is_meta, j = _tile_ids(nq)
    n_full = jnp.where(is_meta, 0, j)
    n_vis = jnp.where(is_meta, 0, j + 1)
    c = (MLA_NOPE + MLA_ROPE) ** -0.5 * LOG2E
    meta_ok = lax.broadcasted_iota(jnp.int32, (LANES, TQ), 0) < N_META
    diag_ok = _chunk_mask()
    heads = lambda p: (2 * p, 2 * p + 1)
    hcol = lambda h: slice(h * HEAD_PAD, (h + 1) * HEAD_PAD)
    vrow = lambda h: slice(h * MLA_V, (h + 1) * MLA_V)

    def score_meta(p):
        ms = []
        for hh, h in enumerate(heads(p)):
            z = jnp.where(meta_ok, _dot_t(km_ref[:, hcol(h)], q_ref[:, hcol(h)]) * c, NEG_INF)
            sm_scr[p % 2, hh] = z
            ms.append(_colmax(z))
        return tuple(ms)

    def score_tiles(p, t, ms, n, masked):
        ks = pl.multiple_of(t * TQ, TQ)
        out = []
        for hh, h in enumerate(heads(p)):
            z = _dot_t(k_ref[pl.ds(ks, n * TQ), hcol(h)], q_ref[:, hcol(h)]) * c
            if masked:
                z = jnp.where(diag_ok, z, NEG_INF)
            for i in range(n):
                s_scr[p % 2, hh, t + i] = z[i * TQ:(i + 1) * TQ]
            out.append(jnp.maximum(ms[hh], _colmax(z)))
        return tuple(out)

    def pv_meta(p, ms):
        st = []
        for hh, h in enumerate(heads(p)):
            st += list(_pv(vtm_ref[0, vrow(h), 0:LANES], sm_scr[p % 2, hh], ms[hh],
                           jnp.zeros((1, TQ), F32), jnp.zeros((MLA_V, TQ), F32)))
        return tuple(st)

    def pv_tiles(p, t, st, ms, n):
        for i in range(n):
            out = []
            for hh, h in enumerate(heads(p)):
                out += list(_pv(vt_ref[t + i, vrow(h), :], s_scr[p % 2, hh, t + i], ms[hh],
                                st[2 * hh], st[2 * hh + 1]))
            st = tuple(out)
        return st

    pairs = MLA_HEADS // 2
    ms = score_meta(0)
    ms = _fori_chunks(0, n_full, lambda t, m, n: score_tiles(0, t, m, n, False), ms)
    ms = lax.fori_loop(n_full, n_vis, lambda t, m: score_tiles(0, t, m, 1, True), ms)
    for p in range(pairs):
        st = pv_meta(p, ms)
        if p + 1 < pairs:
            ms_next = score_meta(p + 1)
            st, ms_next = _fori_chunks(
                0, n_full, lambda t, c, n, p=p, ms=ms: (pv_tiles(p, t, c[0], ms, n),
                                                        score_tiles(p + 1, t, c[1], n, False)), (st, ms_next))
            st, ms_next = lax.fori_loop(
                n_full, n_vis, lambda t, c, p=p, ms=ms: (pv_tiles(p, t, c[0], ms, 1),
                                                         score_tiles(p + 1, t, c[1], 1, True)), (st, ms_next))
        else:
            st = _fori_chunks(0, n_vis, lambda t, s, n, p=p, ms=ms: pv_tiles(p, t, s, ms, n), st)
            ms_next = None
        ot = jnp.concatenate([st[1] / st[0], st[3] / st[2]], axis=0)
        o_ref[:, p * 2 * MLA_V:(p + 1) * 2 * MLA_V] = ot.T.astype(BF16)
        ms = ms_next


def _mla_call(qm, km, vmt, B, S):
    nq = S // TQ
    ng = B * nq + 1
    bidx = lambda g: jnp.minimum(g // nq, B - 1)
    mrow = B * S // LANES
    wk, wv = MLA_HEADS * HEAD_PAD, MLA_HEADS * MLA_V
    return pl.pallas_call(
        functools.partial(_mla_kernel, nq=nq),
        grid=(1, ng),
        in_specs=[
            pl.BlockSpec((TQ, wk), lambda _, g: (g, 0)),
            pl.BlockSpec((S, wk), lambda _, g: (bidx(g), 0)),
            pl.BlockSpec((nq, wv, TQ), lambda _, g: (bidx(g), 0, 0)),
            pl.BlockSpec((LANES, wk), lambda _, g: (mrow, 0)),
            pl.BlockSpec((1, wv, TQ), lambda _, g: (ng - 1, 0, 0)),
        ],
        out_specs=pl.BlockSpec((TQ, wv), lambda _, g: (g, 0)),
        out_shape=jax.ShapeDtypeStruct((ng * TQ, wv), BF16),
        scratch_shapes=[pltpu.VMEM((2, 2, nq, TQ, TQ), F32),
                        pltpu.VMEM((2, 2, LANES, TQ), F32)],
        compiler_params=_params(("arbitrary", "arbitrary")),
        name="mla_attn",
    )(qm, km, vmt, km, vmt)


_IDX_BITS = 13
_B_DIAG, _B_PREV, _B_META_FIRST, _B_META_META = 0, 1, 2, 3


def _dsa_kernel(qi_ref, wi_ref, qb_ref, ki_ref, kb_ref, vbt_ref, kim_ref, kbm_ref, vbtm_ref, bias_ref, o_ref,
                keys_scr, keym_scr, selm_scr, s_scr, sm_scr, qs_scr,
                dig_scr, dm_scr, *, nq, k_sel):
    is_meta, j = _tile_ids(nq)
    n_full = jnp.where(is_meta, 0, j)
    n_vis = jnp.where(is_meta, 0, j + 1)
    lane = lax.broadcasted_iota(jnp.int32, (TQ, LANES), 1)
    low = lane < DSA_HEAD_DIM
    krow_m = lax.broadcasted_iota(jnp.int32, (LANES, TQ), 0)
    krow = lax.broadcasted_iota(jnp.int32, (TQ, TQ), 0)
    meta_ok = krow_m < N_META
    diag_ok = _chunk_mask()

    for p in range(IDX_HEADS // 2):
        qp = qi_ref[:, p * LANES:(p + 1) * LANES]
        qs_scr[2 * p] = jnp.where(low, qp, jnp.zeros_like(qp))
        qs_scr[2 * p + 1] = jnp.where(low, jnp.zeros_like(qp), qp)
    wt = (wi_ref[...] * (IDX_HEADS ** -0.5)).T

    def sort_key(score):
        bits = pltpu.bitcast(score + 0.0, jnp.int32)
        return bits ^ ((bits >> 31) & 0x7FFFFFFF)

    def score_of(kt):
        score = jnp.zeros((kt.shape[0], TQ), F32)
        for h in range(IDX_HEADS):
            score = score + wt[h:h + 1, :] * jnp.maximum(_dot_t(kt, qs_scr[h]), 0.0)
        return sort_key(score)

    keym_scr[...] = jnp.where(meta_ok, score_of(kim_ref[...]), INT_MIN)

    def score_tiles(t, n, masked):
        key = score_of(ki_ref[pl.ds(pl.multiple_of(t * TQ, TQ), n * TQ), :])
        if masked:
            key = jnp.where(diag_ok, key, INT_MIN)
        for i in range(n):
            keys_scr[t + i] = key[i * TQ:(i + 1) * TQ]

    _fori_chunks(0, n_full, lambda t, c, n: (score_tiles(t, n, False), c)[1], 0)
    lax.fori_loop(n_full, n_vis, lambda t, c: (score_tiles(t, 1, True), c)[1], 0)

    one = lambda m: jnp.where(m, 1, 0)
    fold = lambda a: jnp.sum(a.reshape(-1, 8, TQ), axis=0)

    def count(pred):
        def cbody(t, acc):
            return acc + fold(pred(keys_scr[t], LANES + t * TQ + krow))
        acc = _fori_each(0, n_vis, cbody, fold(pred(keym_scr[...], krow_m)))
        return jnp.sum(acc, axis=0, keepdims=True)

    def fold16(m):
        parts = [m[r:r + 16] for r in range(0, m.shape[0], 16)]
        while len(parts) > 1:
            parts = [a + b for a, b in zip(parts[::2], parts[1::2])]
        return parts[0].astype(F32)

    one16 = lambda m: jnp.where(m, jnp.ones((), BF16), jnp.zeros((), BF16))

    def count16(meta_scr, tile_scr, pred):
        acc = _fori_each(0, n_vis, lambda t, a: a + fold16(one16(pred(tile_scr[t]))),
                         fold16(one16(pred(meta_scr[...]))))
        return jnp.sum(acc, axis=0, keepdims=True)

    def digit_search(d_prev, shift, need):
        top, last = shift == 24, shift == 0

        def digits(key, prev):
            d = ((key >> shift) + 128 if top else (key >> shift) & 0xFF).astype(F32).astype(BF16)
            if top:
                return d
            return jnp.where(prev == d_prev.astype(F32).astype(BF16), d, -jnp.ones((), BF16))

        dm_scr[...] = digits(keym_scr[...], None if top else dm_scr[...])
        _loop(0, n_vis, lambda t: dig_scr.__setitem__(t, digits(keys_scr[t], None if top else dig_scr[t])))

        def d_body(i, c):
            d, above, at_least = c
            cand = d + lax.shift_left(jnp.int32(1), 7 - i)
            cb = cand.astype(F32).astype(BF16)
            cnt = count16(dm_scr, dig_scr, lambda v: v >= cb)
            ok = cnt >= need
            return jnp.where(ok, cand, d), jnp.where(ok, above, cnt), jnp.where(ok, cnt, at_least)

        zero = jnp.zeros((1, TQ), F32)
        all_cands = count16(dm_scr, dig_scr, lambda v: v >= jnp.zeros((), BF16)) if last else zero
        return lax.fori_loop(0, 8, d_body, (jnp.zeros((1, TQ), jnp.int32), zero, all_cands))

    t = jnp.zeros((1, TQ), jnp.int32)
    need_d = jnp.full((1, TQ), float(k_sel), F32)
    d = None
    for shift in (24, 16, 8, 0):
        d, above, at_least = digit_search(d, shift, need_d)
        t = (d - 128) if shift == 24 else (t << 8) | d
        need_d = need_d - above
    need = need_d.astype(jnp.int32)
    cnt_ge = k_sel - need + (at_least - above).astype(jnp.int32)

    tied = jnp.max(jnp.where((cnt_ge > k_sel) & (t != INT_MIN), 1, 0))

    def write_masks(sel_mask):
        selm_scr[...] = sel_mask(keym_scr[...], krow_m)
        _loop(0, n_vis, lambda tt: keys_scr.__setitem__(
            tt, pltpu.bitcast(sel_mask(keys_scr[tt], LANES + tt * TQ + krow), jnp.int32)))

    @pl.when(tied == 0)
    def _():
        t_min = jnp.maximum(t, INT_MIN + 1)
        write_masks(lambda b, _: jnp.where(b >= t_min, 0.0, NEG_INF))

    @pl.when(tied > 0)
    def _():
        def tie_body(i, r):
            cand = r | lax.shift_left(jnp.int32(1), _IDX_BITS - 1 - i)
            f = count(lambda b, idx: jnp.where(b == t, one(idx < cand), 0))
            return jnp.where(f < need, cand, r)
        mstar = lax.fori_loop(0, _IDX_BITS, tie_body, jnp.zeros((1, TQ), jnp.int32)) + 1

        def sel_mask(b, idx):
            tie_ok = jnp.where(b == t, one(idx < mstar), 0)
            sel = jnp.where(b == INT_MIN, 0, jnp.where(b > t, 1, tie_ok))
            return jnp.where(sel > 0, 0.0, NEG_INF)

        write_masks(sel_mask)

    c = DSA_HEAD_DIM ** -0.5 * LOG2E
    n_far = jnp.maximum(n_full - 1, 0)
    use_mb = jnp.where(is_meta | (j == 0), 1.0, 0.0)
    mb_idx = jnp.where(is_meta, _B_META_META, _B_META_FIRST)
    pcols = lambda p: slice(p * LANES, (p + 1) * LANES)
    vrow = lambda h: slice(h * DSA_HEAD_DIM, (h + 1) * DSA_HEAD_DIM)

    def masked_q(p):
        qp = qb_ref[:, pcols(p)]
        return [jnp.where(low, qp, jnp.zeros_like(qp)), jnp.where(low, jnp.zeros_like(qp), qp)]

    def score_meta(p):
        qs = masked_q(p)
        ms = []
        for hh in range(2):
            z = (_dot_t(kbm_ref[:, pcols(p)], qs[hh]) * c + bias_ref[2 * p + hh, mb_idx][:LANES, :] * use_mb
                 + selm_scr[...])
            sm_scr[p % 2, hh] = z
            ms.append(_colmax(z))
        return tuple(ms)

    def score_tiles(p, t, ms, n, near):
        qs = masked_q(p)
        k = kb_ref[pl.ds(pl.multiple_of(t * TQ, TQ), n * TQ), pcols(p)]
        out = []
        for hh in range(2):
            zz = _dot_t(k, qs[hh]) * c
            m = ms[hh]
            for i in range(n):
                z = zz[i * TQ:(i + 1) * TQ] + pltpu.bitcast(keys_scr[t + i], F32)
                if near is not None:
                    z = z + bias_ref[2 * p + hh, near]
                s_scr[p % 2, hh, t + i] = z
                m = jnp.maximum(m, _colmax(z))
            out.append(m)
        return tuple(out)

    def pv_meta(p, ms):
        st = []
        for hh in range(2):
            st += list(_pv(vbtm_ref[0, vrow(2 * p + hh), 0:LANES], sm_scr[p % 2, hh], ms[hh],
                           jnp.zeros((1, TQ), F32), jnp.zeros((DSA_HEAD_DIM, TQ), F32)))
        return tuple(st)

    def pv_tiles(p, t, st, ms, n):
        for i in range(n):
            out = []
            for hh in range(2):
                out += list(_pv(vbt_ref[t + i, vrow(2 * p + hh), :], s_scr[p % 2, hh, t + i], ms[hh],
                                st[2 * hh], st[2 * hh + 1]))
            st = tuple(out)
        return st

    def score_all(p, ms):
        ms = _fori_chunks(0, n_far, lambda t, m, n: score_tiles(p, t, m, n, None), ms)
        ms = lax.fori_loop(n_far, n_full, lambda t, m: score_tiles(p, t, m, 1, _B_PREV), ms)
        return lax.fori_loop(n_full, n_vis, lambda t, m: score_tiles(p, t, m, 1, _B_DIAG), ms)

    pairs = DSA_HEADS // 2
    ms = score_all(0, score_meta(0))
    for p in range(pairs):
        st = pv_meta(p, ms)
        if p + 1 < pairs:
            both = lambda near, n=1, p=p, ms=ms: (
                lambda t, c, n=n: (pv_tiles(p, t, c[0], ms, n), score_tiles(p + 1, t, c[1], n, near)))
            carry = (st, score_meta(p + 1))
            carry = _fori_chunks(0, n_far, lambda t, c, n, p=p, ms=ms: (
                pv_tiles(p, t, c[0], ms, n), score_tiles(p + 1, t, c[1], n, None)), carry)
            carry = lax.fori_loop(n_far, n_full, both(_B_PREV), carry)
            st, ms_next = lax.fori_loop(n_full, n_vis, both(_B_DIAG), carry)
        else:
            st = _fori_chunks(0, n_vis, lambda t, s, n, p=p, ms=ms: pv_tiles(p, t, s, ms, n), st)
            ms_next = None
        ot = jnp.concatenate([st[1] / st[0], st[3] / st[2]], axis=0)
        o_ref[:, pcols(p)] = ot.T.astype(BF16)
        ms = ms_next


def _dsa_call(qi, wi, qb, ki, kb, vbt, bias, B, S, k_sel):
    nq = S // TQ
    ng = B * nq + 1
    W = DSA_HEADS * DSA_HEAD_DIM
    bidx = lambda g: jnp.minimum(g // nq, B - 1)
    mrow = B * S // LANES
    qrow = lambda w: pl.BlockSpec((TQ, w), lambda _, g: (g, 0))
    kv = lambda w: pl.BlockSpec((S, w), lambda _, g: (bidx(g), 0))
    kvm = lambda w: pl.BlockSpec((LANES, w), lambda _, g: (mrow, 0))
    return pl.pallas_call(
        functools.partial(_dsa_kernel, nq=nq, k_sel=k_sel),
        grid=(1, ng),
        in_specs=[qrow(W), qrow(LANES), qrow(W), kv(LANES), kv(W),
                  pl.BlockSpec((nq, W, TQ), lambda _, g: (bidx(g), 0, 0)),
                  kvm(LANES), kvm(W),
                  pl.BlockSpec((1, W, TQ), lambda _, g: (ng - 1, 0, 0)),
                  pl.BlockSpec(bias.shape, lambda _, g: (0, 0, 0, 0), pipeline_mode=pl.Buffered(1))],
        out_specs=qrow(W),
        out_shape=jax.ShapeDtypeStruct((ng * TQ, W), BF16),
        scratch_shapes=[
            pltpu.VMEM((nq, TQ, TQ), jnp.int32),
            pltpu.VMEM((LANES, TQ), jnp.int32),
            pltpu.VMEM((LANES, TQ), F32),
            pltpu.VMEM((2, 2, nq, TQ, TQ), F32),
            pltpu.VMEM((2, 2, LANES, TQ), F32),
            pltpu.VMEM((IDX_HEADS, TQ, LANES), BF16),
            pltpu.VMEM((nq, TQ, TQ), BF16),
            pltpu.VMEM((LANES, TQ), BF16),
        ],
        compiler_params=_params(("arbitrary", "arbitrary")),
        name="dsa",
    )(qi, wi, qb, ki, kb, vbt, ki, kb, vbt, bias)


def _t5_bucket(rel):
    nb = REL_BUCKETS // 2
    max_exact = nb // 2
    n = jnp.abs(rel)
    large = max_exact + (jnp.log(jnp.maximum(n, 1).astype(F32) / max_exact)
                         / math.log(REL_MAX_DIST / max_exact) * (nb - max_exact)).astype(jnp.int32)
    large = jnp.minimum(large, nb - 1)
    return jnp.where(rel > 0, nb, 0) + jnp.where(n < max_exact, n, large)


def _bias_kernel(bucket_ref, rb_ref, o_ref, *, far_bucket):
    h = pl.program_id(0)
    far = rb_ref[far_bucket, h]
    for tile in range(bucket_ref.shape[0]):
        bkt = bucket_ref[tile]
        acc = jnp.zeros(bkt.shape, F32)
        for b in range(REL_BUCKETS):
            acc = jnp.where(bkt == b, rb_ref[b, h], acc)
        o_ref[0, tile] = (acc - far) * LOG2E


def _bias_tiles(rel_bias):
    k = jnp.arange(TQ, dtype=jnp.int32)[:, None]
    q = jnp.arange(TQ, dtype=jnp.int32)[None, :]
    rels = jnp.stack([k - q, k - q - TQ, k - (q + N_META), k - q])
    far_bucket = REL_BUCKETS // 2 - 1
    return pl.pallas_call(
        functools.partial(_bias_kernel, far_bucket=far_bucket),
        grid=(DSA_HEADS,),
        in_specs=[pl.BlockSpec((4, TQ, TQ), lambda h: (0, 0, 0)),
                  pl.BlockSpec(memory_space=pltpu.SMEM)],
        out_specs=pl.BlockSpec((1, 4, TQ, TQ), lambda h: (h, 0, 0, 0)),
        out_shape=jax.ShapeDtypeStruct((DSA_HEADS, 4, TQ, TQ), F32),
        compiler_params=_params(("parallel",)),
        name="bias_tiles",
    )(_t5_bucket(rels), rel_bias)


_MERGE_ROWS = 128


def _merge_kernel(x_ref, meta_ref, oa_ref, ob_ref, ga_ref, gb_ref, wa_ref, wb_ref, wo_ref, gf_ref, wrt_ref, brt_ref,
                  h1_ref, route_ref, gate_ref, cnt_ref, run_scr):
    is_meta = pl.program_id(0) == pl.num_programs(0) - 1
    for rows in (slice(s, s + _MERGE_ROWS) for s in range(0, x_ref.shape[0], _MERGE_ROWS)):
        h = jnp.where(is_meta, meta_ref[rows, :], x_ref[rows, :])
        y = (jax.nn.sigmoid(ga_ref[rows, :]) * _dot(oa_ref[rows, :], wa_ref[...])
             + jax.nn.sigmoid(gb_ref[rows, :]) * _dot(ob_ref[rows, :], wb_ref[...]))
        h1_ref[rows, :] = h + _dot(y.astype(BF16), wo_ref[...])
    _rank_tile(h1_ref[...], gf_ref[...], wrt_ref[...], brt_ref[...], route_ref, gate_ref, cnt_ref, run_scr)


def _merge_call(x2d, meta_tile, oa, ob, ga, gb, wa, wb, wo, gf, wrt, brt):
    BS, D = x2d.shape
    nt = BS // TQ + 1
    R = nt * TQ
    row = lambda w: pl.BlockSpec((TQ, w), lambda i: (i, 0))
    full = lambda a: pl.BlockSpec(a.shape, lambda i: (0, 0))
    return pl.pallas_call(
        _merge_kernel,
        grid=(nt,),
        in_specs=[pl.BlockSpec((TQ, D), lambda i: (jnp.minimum(i, nt - 2), 0)), full(meta_tile),
                  row(oa.shape[1]), row(ob.shape[1]), row(D), row(D), full(wa), full(wb), full(wo),
                  full(gf), full(wrt), full(brt)],
        out_specs=[row(D), pl.BlockSpec((8, TQ), lambda i: (0, i)), row(LANES),
                   pl.BlockSpec((N_EXPERTS, LANES), lambda i: (0, 0))],
        out_shape=[jax.ShapeDtypeStruct((R, D), F32), jax.ShapeDtypeStruct((8, R), jnp.int32),
                   jax.ShapeDtypeStruct((R, LANES), F32), jax.ShapeDtypeStruct((N_EXPERTS, LANES), jnp.int32)],
        scratch_shapes=[pltpu.VMEM((N_EXPERTS, LANES), F32)],
        compiler_params=_params(("arbitrary",)),
        name="merge_route",
    )(x2d, meta_tile, oa, ob, ga, gb, wa, wb, wo, gf, wrt, brt)


_ROUTER_ROWS = 40
TE = 256
_ISSUE_UNROLL = TQ


def _route_rows(xn, wrt, brt):
    nr = _ROUTER_ROWS
    tm = xn.shape[0]
    def split(a):
        hi = a.astype(BF16)
        return hi, (a - hi.astype(F32)).astype(BF16)

    (w_hi, w_lo), (x_hi, x_lo) = split(wrt), split(xn)
    lt = _dot_t(w_hi, x_hi) + (_dot_t(w_hi, x_lo) + _dot_t(w_lo, x_hi)) + brt
    row = lax.broadcasted_iota(jnp.int32, (nr, tm), 0)
    ninf = -jnp.inf
    cmax = lambda a: jnp.max(a, axis=0, keepdims=True)
    cmin = lambda a: jnp.min(a, axis=0, keepdims=True)
    gl = jnp.where((row >= N_EXPERTS) & (row < N_EXPERTS + N_GROUPS), lt, ninf)
    gmax = cmax(gl)
    gsel = cmin(jnp.where(gl == gmax, row, nr)) - N_EXPERTS
    p_group = 1.0 / jnp.sum(jnp.exp(gl - gmax), axis=0, keepdims=True)
    lo = gsel * EXPERTS_PER_GROUP
    el = jnp.where((row >= lo) & (row < lo + EXPERTS_PER_GROUP), lt, ninf)
    m1 = cmax(el)
    i1 = cmin(jnp.where(el == m1, row, nr))
    el2 = jnp.where(row == i1, ninf, el)
    m2 = cmax(el2)
    i2 = cmin(jnp.where(el2 == m2, row, nr))
    e2 = jnp.exp(m2 - m1)
    return i1, i2, p_group / (1.0 + e2), p_group * e2 / (1.0 + e2)


def _rank_tile(h1, gf, wrt, brt, route_ref, gate_ref, cnt_ref, run_scr):
    @pl.when(pl.program_id(0) == 0)
    def _():
        run_scr[...] = jnp.zeros_like(run_scr)

    tm = h1.shape[0]
    i1, i2, w1, w2 = _route_rows(_rms(h1, gf), wrt, brt)
    row = lax.broadcasted_iota(jnp.int32, (N_EXPERTS, tm), 0)
    o1 = jnp.where(row == i1, 1.0, 0.0)
    o2 = jnp.where(row == i2, 1.0, 0.0)
    a = lax.broadcasted_iota(jnp.int32, (tm, tm), 0)
    b = lax.broadcasted_iota(jnp.int32, (tm, tm), 1)
    before = jnp.where(a < b, 1.0, 0.0).astype(BF16)
    p1 = _dot(o1.astype(BF16), before)
    p2 = _dot(o2.astype(BF16), before)
    run = run_scr[:, 0:1]
    c1 = jnp.sum(o1, axis=1, keepdims=True)
    c2 = jnp.sum(o2, axis=1, keepdims=True)
    r1 = jnp.sum(o1 * (run + p1), axis=0, keepdims=True)
    r2 = jnp.sum(o2 * (run + c1 + p2), axis=0, keepdims=True)
    new_run = run + c1 + c2
    run_scr[...] = jnp.broadcast_to(new_run, run_scr.shape)
    cnt_ref[...] = jnp.broadcast_to(new_run, cnt_ref.shape).astype(jnp.int32)
    z = jnp.zeros((4, tm), jnp.int32)
    route_ref[...] = jnp.concatenate([i1, r1.astype(jnp.int32), i2, r2.astype(jnp.int32), z], axis=0)
    gt = jnp.concatenate([w1, w2, jnp.zeros((LANES - 2, tm), F32)], axis=0)
    gate_ref[...] = gt.T


_PAD_BITS = tuple(1 << b for b in reversed(range(TE.bit_length() - 1)))


def _dispatch_kernel(pos_ref, pad_ref, h1_ref, gf_ref, xs_hbm, buf, zbuf, sem, zsem, *, n_rows):
    i = pl.program_id(0)
    nt = pl.num_programs(0)
    slot = i % 2
    tm = h1_ref.shape[0]

    @pl.when(i == 0)
    def _():
        zbuf[...] = jnp.zeros_like(zbuf)

        def pad_copies(e, wait):
            start, n = pad_ref[e], pad_ref[N_EXPERTS + e]

            def copy(dst, rows):
                cp = pltpu.make_async_copy(zbuf.at[pl.ds(0, rows), :], xs_hbm.at[pl.ds(dst, rows), :], zsem)
                cp.wait() if wait else cp.start()

            end = start + n
            for b in (b for b in _PAD_BITS if b >= 8):
                end = end - (n & b)
                pl.when((n & b) != 0)(functools.partial(copy, pl.multiple_of(end, 8), b))
            for r in range(7):
                pl.when(r < (n & 7))(functools.partial(copy, start + r, 1))

        def tail_copies(blk, wait):
            for h in range(0, TE, zbuf.shape[0]):
                cp = pltpu.make_async_copy(
                    zbuf, xs_hbm.at[pl.ds(pl.multiple_of(blk * TE + h, zbuf.shape[0]), zbuf.shape[0]), :], zsem)
                cp.wait() if wait else cp.start()

        n_used, n_blk = pad_ref[2 * N_EXPERTS], xs_hbm.shape[0] // TE
        for wait in (False, True):
            _loop_plain(N_EXPERTS, lambda e: pad_copies(e, wait))
            lax.fori_loop(n_used, n_blk, lambda b, c: (tail_copies(b, wait), c)[1], 0)

    def wait_slot(s):
        for _ in range(2):
            pltpu.make_async_copy(buf.at[s], xs_hbm.at[pl.ds(0, tm), :], sem.at[s]).wait()

    @pl.when(i >= 2)
    def _():
        wait_slot(slot)

    buf[slot] = _rms(h1_ref[...], gf_ref[...])

    def issue(u, c):
        for v in range(_ISSUE_UNROLL):
            r = u * _ISSUE_UNROLL + v
            src = buf.at[slot, pl.ds(r, 1), :]
            for k in range(2):
                p = pos_ref[k * n_rows + i * tm + r]
                pltpu.make_async_copy(src, xs_hbm.at[pl.ds(p, 1), :], sem.at[slot]).start(priority=k)
        return c

    lax.fori_loop(0, tm // _ISSUE_UNROLL, issue, 0)

    @pl.when(i == nt - 1)
    def _():
        wait_slot(slot)

        @pl.when(nt >= 2)
        def _():
            wait_slot(1 - slot)


def _loop_plain(n, fn):
    lax.fori_loop(0, n, lambda e, c: (fn(e), c)[1], 0)


def _dispatch_call(pos, pad_runs, h1, gf, n_slots):
    R, D = h1.shape
    return pl.pallas_call(
        functools.partial(_dispatch_kernel, n_rows=R),
        grid_spec=pltpu.PrefetchScalarGridSpec(
            num_scalar_prefetch=2, grid=(R // TQ,),
            in_specs=[pl.BlockSpec((TQ, D), lambda i, pos, pad: (i, 0)),
                      pl.BlockSpec(gf.shape, lambda i, pos, pad: (0, 0))],
            out_specs=pl.BlockSpec(memory_space=pl.ANY),
            scratch_shapes=[pltpu.VMEM((2, TQ, D), F32), pltpu.VMEM((_PAD_BITS[0], D), F32),
                            pltpu.SemaphoreType.DMA((2,)), pltpu.SemaphoreType.DMA]),
        out_shape=jax.ShapeDtypeStruct((n_slots, D), F32),
        compiler_params=_params(("arbitrary",)),
        name="moe_dispatch",
    )(pos, pad_runs, h1, gf)


def _ffn_kernel(be_ref, nu_ref, x_ref, wg_ref, wu_ref, wd_ref, y_ref):
    i = pl.program_id(0)

    @pl.when(i < nu_ref[0])
    def _():
        x = x_ref[...].astype(BF16)
        a = _dot(x, wg_ref[0].astype(BF16))
        u = _dot(x, wu_ref[0].astype(BF16))
        hmid = (a * jax.nn.sigmoid(a) * u).astype(BF16)
        y_ref[...] = _dot(hmid, wd_ref[0].astype(BF16))

    @pl.when(i >= nu_ref[0])
    def _():
        y_ref[...] = jnp.zeros_like(y_ref)


def _ffn_call(blk_expert, n_used, xs, wg, wu, wd):
    NS, D = xs.shape
    wspec = lambda shp: pl.BlockSpec((1,) + shp, lambda i, be, nu: (be[i], 0, 0))
    return pl.pallas_call(
        _ffn_kernel,
        grid_spec=pltpu.PrefetchScalarGridSpec(
            num_scalar_prefetch=2, grid=(NS // TE,),
            in_specs=[pl.BlockSpec((TE, D), lambda i, be, nu: (jnp.minimum(i, nu[0] - 1), 0)),
                      wspec((D, D_EXPERT)), wspec((D, D_EXPERT)), wspec((D_EXPERT, D))],
            out_specs=pl.BlockSpec((TE, D), lambda i, be, nu: (i, 0))),
        out_shape=jax.ShapeDtypeStruct((NS, D), F32),
        compiler_params=_params(("arbitrary",)),
        name="moe_ffn",
    )(blk_expert, n_used, xs, wg, wu, wd)


def _combine_kernel(pos_ref, h1_ref, gate_ref, gfin_ref, ys_hbm, o_ref, buf, sem, *, n_rows):
    i = pl.program_id(0)
    nt = pl.num_programs(0)
    slot = i % 2
    tm = h1_ref.shape[0]

    def fetch(tile, s):
        def issue(u, c):
            for v in range(_ISSUE_UNROLL):
                r = u * _ISSUE_UNROLL + v
                for k in range(2):
                    p = pos_ref[k * n_rows + tile * tm + r]
                    pltpu.make_async_copy(ys_hbm.at[pl.ds(p, 1), :], buf.at[s, k, pl.ds(r, 1), :],
                                          sem.at[s]).start(priority=k)
            return c
        lax.fori_loop(0, tm // _ISSUE_UNROLL, issue, 0)

    @pl.when(i == 0)
    def _():
        fetch(0, 0)

    @pl.when(i + 1 < nt)
    def _():
        fetch(i + 1, 1 - slot)

    for k in range(2):
        pltpu.make_async_copy(ys_hbm.at[pl.ds(0, tm), :], buf.at[slot, k], sem.at[slot]).wait()

    g = gate_ref[...]
    ffn = g[:, 0:1] * buf[slot, 0] + g[:, 1:2] * buf[slot, 1]

    @pl.when(i < nt - 1)
    def _():
        o_ref[...] = _rms(h1_ref[...] + ffn, gfin_ref[...])


def _combine_call(pos, h1, gates, gfin, ys, n_out):
    R, D = h1.shape
    nt = R // TQ
    return pl.pallas_call(
        functools.partial(_combine_kernel, n_rows=R),
        grid_spec=pltpu.PrefetchScalarGridSpec(
            num_scalar_prefetch=1, grid=(nt,),
            in_specs=[pl.BlockSpec((TQ, D), lambda i, pos: (i, 0)),
                      pl.BlockSpec((TQ, LANES), lambda i, pos: (i, 0)),
                      pl.BlockSpec(gfin.shape, lambda i, pos: (0, 0)),
                      pl.BlockSpec(memory_space=pl.ANY)],
            out_specs=pl.BlockSpec((TQ, D), lambda i, pos: (jnp.minimum(i, nt - 2), 0)),
            scratch_shapes=[pltpu.VMEM((2, 2, TQ, D), F32), pltpu.SemaphoreType.DMA((2,))]),
        out_shape=jax.ShapeDtypeStruct((n_out, D), F32),
        compiler_params=_params(("arbitrary",)),
        name="moe_combine",
    )(pos, h1, gates, gfin, ys)


def _sparse_moe(h1, route, gates, cnt, gf, wg, wu, wd, gfin, n_out):
    R, D = h1.shape
    counts = cnt[:, 0]
    padded = (counts + TE - 1) // TE * TE
    pad_end = jnp.cumsum(padded)
    offs = pad_end - padded
    nb = -(-2 * R // TE) + N_EXPERTS
    eids = jnp.arange(N_EXPERTS, dtype=jnp.int32)
    slot_of = lambda e, r: r + jnp.sum(jnp.where(e[:, None] == eids[None, :], offs[None, :], 0), axis=1)
    pos = jnp.concatenate([slot_of(route[0], route[1]), slot_of(route[2], route[3])]).astype(jnp.int32)
    blk_start = jnp.arange(nb, dtype=jnp.int32) * TE
    blk_expert = jnp.minimum(jnp.sum((blk_start[:, None] >= pad_end[None, :]).astype(jnp.int32), axis=1),
                             N_EXPERTS - 1).astype(jnp.int32)
    n_used = (pad_end[-1] // TE).astype(jnp.int32).reshape(1)
    pad_runs = jnp.concatenate([offs + counts, padded - counts, n_used]).astype(jnp.int32)
    xs = _dispatch_call(pos, pad_runs, h1, gf, nb * TE)
    ys = _ffn_call(blk_expert, n_used, xs, wg, wu, wd)
    return _combine_call(pos, h1, gates, gfin, ys, n_out)


def kernel(x, meta_tokens, norm_mix_g, w_in, mla_cq_norm_g, mla_ckv_norm_g, w_mla_uq, w_mla_ukv,
           w_branch_a, w_branch_b, w_out, rel_bias, norm_ffn_g, w_router_group, b_router_group,
           w_router_expert, b_router_expert, w_exp_gate, w_exp_up, w_exp_down, norm_final_g):
    B, S, D = x.shape
    assert S % TQ == 0 and norm_mix_g.shape[0] == 1
    k_sel = min(K_SEL_MAX, S // 4)
    x2d = x.reshape(B * S, D)
    meta_tile = jnp.concatenate([meta_tokens.astype(x.dtype), jnp.zeros((TQ - N_META, D), x.dtype)], axis=0)

    pos = np.concatenate([N_META + np.arange(S), np.minimum(np.arange(TQ), N_META)]).astype(np.int32)
    ct, st = _rope_tables(jnp.asarray(pos))

    qm, km, vmt, qb, kb, vbt, qi, ki, wi, ga, gb = _proj_call(
        x2d, meta_tile, norm_mix_g.reshape(1, D), _pack_w1(w_in[0]), mla_cq_norm_g.reshape(1, -1),
        mla_ckv_norm_g.reshape(1, -1), _pack_wuq(w_mla_uq[0]), _pack_wukv(w_mla_ukv[0]), ct, st, S)

    o_a = _mla_call(qm, km, vmt, B, S)
    o_b = _dsa_call(qi, wi, qb, ki, kb, vbt, _bias_tiles(rel_bias), B, S, k_sel)

    pad_r = _ROUTER_ROWS - N_EXPERTS - N_GROUPS
    w_r = jnp.concatenate([w_router_expert[0].T, w_router_group[0].T, jnp.zeros((pad_r, D), F32)], axis=0)
    b_r = jnp.concatenate([b_router_expert[0], b_router_group[0],
                           jnp.zeros((pad_r,), F32)]).reshape(_ROUTER_ROWS, 1)
    gf = norm_ffn_g.reshape(1, D)
    h1, route, gates, cnt = _merge_call(x2d, meta_tile, o_a, o_b, ga, gb, w_branch_a[0].astype(BF16),
                                        w_branch_b[0].astype(BF16), w_out[0].astype(BF16), gf, w_r, b_r)
    out = _sparse_moe(h1, route, gates, cnt, gf, w_exp_gate[0], w_exp_up[0], w_exp_down[0],
                      norm_final_g.reshape(1, D), B * S)
    return out.reshape(B, S, D)
```
